```python
import math
import jax
import jax.numpy as jnp
from jax import lax
import numpy as np

D_MODEL = 1024
BATCH = 4
SEQ = 4096
DEPTH = 2

GRID_W = 64
CTX_LEN = 256
Q_BLOCK = 128
ROPE_THETA = 10000.0
NORM_EPS = 1e-6

GQA_HEADS = 8
GQA_KV_HEADS = 2
GQA_HEAD_DIM = 64
GQA_GROUP = GQA_HEADS // GQA_KV_HEADS
GQA_SCALE = 1.0 / math.sqrt(GQA_HEAD_DIM)

MLA_HEADS = 8
MLA_NOPE_DIM = 64
MLA_ROPE_DIM = 32
MLA_V_DIM = 64
MLA_Q_RANK = 384
MLA_KV_RANK = 256
MLA_SCALE = 1.0 / math.sqrt(MLA_NOPE_DIM + MLA_ROPE_DIM)

S5_WIDTH = 512
S5_GROUP = 16
S5_GROUPS = S5_WIDTH // S5_GROUP
S5_STATE = 64

BRANCH_WIDTH = 512
N_BRANCHES = 3

D_FF = 2816
N_EXPERTS = 8
TOP_K = 2
N_DENSE = (DEPTH + 1) // 2
N_MOE = DEPTH // 2

Q_SIZES = (GQA_HEADS * GQA_HEAD_DIM, MLA_Q_RANK)
KV_SIZES = (GQA_KV_HEADS * GQA_HEAD_DIM, GQA_KV_HEADS * GQA_HEAD_DIM, MLA_KV_RANK, MLA_ROPE_DIM, S5_WIDTH)
Q_COLS = GQA_HEADS * GQA_HEAD_DIM + MLA_Q_RANK
N_IN = Q_COLS + 2 * GQA_KV_HEADS * GQA_HEAD_DIM + MLA_KV_RANK + MLA_ROPE_DIM + S5_WIDTH

kernel_name = 'hybrid_gqa_mla_s5_moe_diffusion_block'


def _rmsnorm(x, g):
    xf = x.astype(jnp.float32)
    y = xf * lax.rsqrt(jnp.mean(xf * xf, axis=-1, keepdims=True) + NORM_EPS)
    return (y * g.astype(jnp.float32)).astype(x.dtype)


def _modulate(h, shift, scale):
    return h * (1 + scale) + shift


def _split_cols(z, sizes):
    return jnp.split(z, np.cumsum(sizes)[:-1].tolist(), axis=-1)


def _grid_rope(row_idx, col_idx, rot_dim):
    axis_dim = rot_dim // 2
    inv_freq = ROPE_THETA ** (-jnp.arange(0, axis_dim, 2, dtype=jnp.float32) / axis_dim)
    ang = jnp.concatenate([row_idx.astype(jnp.float32)[:, None] * inv_freq,
                           col_idx.astype(jnp.float32)[:, None] * inv_freq], axis=-1)
    return jnp.cos(ang), jnp.sin(ang)


def _apply_rope(x, cs):
    cos, sin = cs
    c = cos[None, :, None, :]
    s = sin[None, :, None, :]
    x1, x2 = jnp.split(x.astype(jnp.float32), 2, axis=-1)
    return jnp.concatenate([x1 * c - x2 * s, x1 * s + x2 * c], axis=-1).astype(x.dtype)


def _attend(q, k, v, scale):
    s = jnp.einsum('bqhgd,bkhd->bhgqk', q, k, preferred_element_type=jnp.float32) * scale
    p = jax.nn.softmax(s, axis=-1).astype(v.dtype)
    return jnp.einsum('bhgqk,bkhd->bqhgd', p, v)


def _blocked_attention(q, k, v, scale):
    B, L, Hk, G, dk = q.shape
    qb = q.reshape(B, L // Q_BLOCK, Q_BLOCK, Hk, G, dk).transpose(1, 0, 2, 3, 4, 5)
    ob = lax.map(lambda qi: _attend(qi, k, v, scale), qb)
    return ob.transpose(1, 0, 2, 3, 4, 5).reshape(B, L, -1)


def _gqa_q(q, g_q, cs):
    B, L, _ = q.shape
    q = _rmsnorm(q.reshape(B, L, GQA_HEADS, GQA_HEAD_DIM), g_q)
    if cs is not None:
        q = _apply_rope(q, cs)
    return q.reshape(B, L, GQA_KV_HEADS, GQA_GROUP, GQA_HEAD_DIM)


def _gqa_kv(k, v, g_k, cs):
    B, L, _ = k.shape
    k = _rmsnorm(k.reshape(B, L, GQA_KV_HEADS, GQA_HEAD_DIM), g_k)
    if cs is not None:
        k = _apply_rope(k, cs)
    return k, v.reshape(B, L, GQA_KV_HEADS, GQA_HEAD_DIM)


def _mla_q(cq, g_cq, w_uq, cs):
    B, L, _ = cq.shape
    q = (_rmsnorm(cq, g_cq) @ w_uq).reshape(B, L, MLA_HEADS, MLA_NOPE_DIM + MLA_ROPE_DIM)
    q_nope, q_rope = q[..., :MLA_NOPE_DIM], q[..., MLA_NOPE_DIM:]
    if cs is not None:
        q_rope = _apply_rope(q_rope, cs)
    return jnp.concatenate([q_nope, q_rope], axis=-1).reshape(B, L, MLA_HEADS, 1, MLA_NOPE_DIM + MLA_ROPE_DIM)


def _mla_kv(ckv, kr, g_ckv, w_ukv, cs):
    B, L, _ = ckv.shape
    kv = (_rmsnorm(ckv, g_ckv) @ w_ukv).reshape(B, L, MLA_HEADS, MLA_NOPE_DIM + MLA_V_DIM)
    k_nope, v = kv[..., :MLA_NOPE_DIM], kv[..., MLA_NOPE_DIM:]
    kr = kr.reshape(B, L, 1, MLA_ROPE_DIM)
    if cs is not None:
        kr = _apply_rope(kr, cs)
    k = jnp.concatenate([k_nope, jnp.broadcast_to(kr, (B, L, MLA_HEADS, MLA_ROPE_DIM))], axis=-1)
    return k, v


def _cmul(ar, ai, br, bi):
    return ar * br - ai * bi, ar * bi + ai * br


def _ssm_combine(e1, e2):
    a1r, a1i, b1r, b1i = e1
    a2r, a2i, b2r, b2i = e2
    ar, ai = _cmul(a2r, a2i, a1r, a1i)
    br, bi = _cmul(a2r, a2i, b1r, b1i)
    return ar, ai, br + b2r, bi + b2i


def _s5(u_c, u_l, lam_re, lam_im, log_dt, b_re, b_im, c_re, c_im, d_skip, w_glu, need_ctx):
    f32 = jnp.float32

    def groups(u):
        B, L, _ = u.shape
        return u.astype(f32).reshape(B, L, S5_GROUPS, S5_GROUP).transpose(1, 0, 2, 3)

    uc, ul = groups(u_c), groups(u_l)
    dsk = d_skip.astype(f32)
    y_l = dsk * ul
    y_c = dsk * uc if need_ctx else None
    for d in range(2):
        rev = d == 1
        lr = jnp.minimum(lam_re[d].astype(f32), -1e-4)
        li = lam_im[d].astype(f32)
        dt = jnp.exp(log_dt[d].astype(f32))[:, None]
        mag = jnp.exp(lr * dt)
        abr, abi = mag * jnp.cos(li * dt), mag * jnp.sin(li * dt)
        den = lr * lr + li * li
        fr = ((abr - 1.0) * lr + abi * li) / den
        fi = (abi * lr - (abr - 1.0) * li) / den
        br, bi = b_re[d].astype(f32), b_im[d].astype(f32)
        bbr = fr[..., None] * br - fi[..., None] * bi
        bbi = fr[..., None] * bi + fi[..., None] * br
        cr, ci = c_re[d].astype(f32), c_im[d].astype(f32)

        def scan(u, init):
            vr = jnp.einsum('gph,lbgh->lbgp', bbr, u)
            vi = jnp.einsum('gph,lbgh->lbgp', bbi, u)
            if init is not None:
                sr, si = _cmul(abr, abi, init[0], init[1])
                pos = u.shape[0] - 1 if rev else 0
                vr = vr.at[pos].add(sr)
                vi = vi.at[pos].add(si)
            shape = (u.shape[0], 1) + abr.shape
            ar = jnp.broadcast_to(abr, shape)
            ai = jnp.broadcast_to(abi, shape)
            _, _, xr, xi = lax.associative_scan(_ssm_combine, (ar, ai, vr, vi), reverse=rev, axis=0)
            return xr, xi

        def readout(xr, xi):
            return jnp.einsum('ghp,lbgp->lbgh', cr, xr) - jnp.einsum('ghp,lbgp->lbgh', ci, xi)

        xr_c, xi_c = scan(uc, None)
        end = 0 if rev else -1
        xr_l, xi_l = scan(ul, (xr_c[end], xi_c[end]))
        y_l = y_l + readout(xr_l, xi_l)
        if need_ctx:
            y_c = y_c + readout(xr_c, xi_c)

    def glu(y, dtype):
        L, B = y.shape[0], y.shape[1]
        y = jax.nn.gelu(y.transpose(1, 0, 2, 3).reshape(B, L, S5_WIDTH)).astype(dtype)
        a, g = jnp.split(y @ w_glu, 2, axis=-1)
        return a * jax.nn.sigmoid(g)

    return (glu(y_c, u_c.dtype) if need_ctx else None), glu(y_l, u_l.dtype)


def _token_mixer(h_c, h_l, w_in, g_q, g_k, g_cq, g_ckv, w_uq, w_ukv,
                 lam_re, lam_im, log_dt, b_re, b_im, c_re, c_im, d_skip, w_glu,
                 w_branch, w_mg, b_mg, w_out, gqa_cs, mla_cs, need_ctx):
    qg_l, qm_l, kg_l, vg_l, ckv_l, kr_l, u_l = _split_cols(h_l @ w_in, Q_SIZES + KV_SIZES)
    if need_ctx:
        qg_c, qm_c, kg_c, vg_c, ckv_c, kr_c, u_c = _split_cols(h_c @ w_in, Q_SIZES + KV_SIZES)
    else:
        kg_c, vg_c, ckv_c, kr_c, u_c = _split_cols(h_c @ w_in[:, Q_COLS:], KV_SIZES)
    B, Lc = h_c.shape[0], h_c.shape[1]

    ka_c, va_c = _gqa_kv(kg_c, vg_c, g_k, None)
    ka_l, va_l = _gqa_kv(kg_l, vg_l, g_k, gqa_cs)
    ya_l = _blocked_attention(_gqa_q(qg_l, g_q, gqa_cs),
                              jnp.concatenate([ka_c, ka_l], axis=1),
                              jnp.concatenate([va_c, va_l], axis=1), GQA_SCALE)

    kb_c, vb_c = _mla_kv(ckv_c, kr_c, g_ckv, w_ukv, None)
    kb_l, vb_l = _mla_kv(ckv_l, kr_l, g_ckv, w_ukv, mla_cs)
    yb_l = _blocked_attention(_mla_q(qm_l, g_cq, w_uq, mla_cs),
                              jnp.concatenate([kb_c, kb_l], axis=1),
                              jnp.concatenate([vb_c, vb_l], axis=1), MLA_SCALE)

    yc_c, yc_l = _s5(u_c, u_l, lam_re, lam_im, log_dt, b_re, b_im, c_re, c_im, d_skip, w_glu, need_ctx)

    def merge(h, branches):
        Bh, L = h.shape[0], h.shape[1]
        gates = jax.nn.sigmoid((h @ w_mg + b_mg).astype(jnp.float32)).astype(h.dtype)
        gates = gates.reshape(Bh, L, N_BRANCHES, D_MODEL)
        proj = jnp.einsum('blnw,nwd->blnd', jnp.stack(branches, axis=2), w_branch)
        return jnp.sum(gates * proj, axis=2) @ w_out

    y_l = merge(h_l, [ya_l, yb_l, yc_l])
    y_c = None
    if need_ctx:
        ya_c = _attend(_gqa_q(qg_c, g_q, None), ka_c, va_c, GQA_SCALE).reshape(B, Lc, -1)
        yb_c = _attend(_mla_q(qm_c, g_cq, w_uq, None), kb_c, vb_c, MLA_SCALE).reshape(B, Lc, -1)
        y_c = merge(h_c, [ya_c, yb_c, yc_c])
    return y_c, y_l


def _swiglu(h, w1, w3, w2):
    return (jax.nn.silu(h @ w1) * (h @ w3)) @ w2


def _moe(h, rw, rb, w1, w3, w2):
    logits = jnp.einsum('bld,de->ble', h, rw, preferred_element_type=jnp.float32) + rb.astype(jnp.float32)
    top_v, top_i = lax.top_k(logits, TOP_K)
    top_w = jax.nn.softmax(top_v, axis=-1)
    gate = jnp.einsum('blk,blke->ble', top_w, jax.nn.one_hot(top_i, N_EXPERTS, dtype=jnp.float32)).astype(h.dtype)
    out = jnp.zeros_like(h)
    for e in range(N_EXPERTS):
        out = out + gate[..., e:e + 1] * _swiglu(h, w1[e], w3[e], w2[e])
    return out


def _channel_mixer(h, layer, ffn_w1, ffn_w3, ffn_w2, router_w, router_b, moe_w1, moe_w3, moe_w2):
    i = layer // 2
    if layer % 2 == 0:
        return _swiglu(h, ffn_w1[i], ffn_w3[i], ffn_w2[i])
    return _moe(h, router_w[i], router_b[i], moe_w1[i], moe_w3[i], moe_w2[i])


def setup_inputs(seed: int = 0) -> dict:
    key = jax.random.key(seed)
    keys = iter(jax.random.split(key, 48))
    f32 = jnp.float32

    def normal(shape, scale):
        return jax.random.normal(next(keys), shape, f32) * scale

    def gain(shape):
        return 1.0 + normal(shape, 0.02)

    D, L = D_MODEL, DEPTH
    G, P, H = S5_GROUPS, S5_STATE, S5_GROUP
    return {
        'x': normal((BATCH, SEQ, D), 1.0),
        'c': normal((BATCH, D), 1.0),
        'ctx': normal((BATCH, CTX_LEN, D), 1.0),
        'c_ctx': normal((D,), 1.0),
        'w_mod': normal((L, D, 6 * D), 0.5 * D ** -0.5),
        'b_mod': normal((L, 6 * D), 0.02),
        'g_mix_pre': gain((L, D)),
        'g_mix_post': gain((L, D)),
        'g_ffn_pre': gain((L, D)),
        'g_ffn_post': gain((L, D)),
        'w_in': normal((L, D, N_IN), D ** -0.5),
        'g_q': gain((L, GQA_HEAD_DIM)),
        'g_k': gain((L, GQA_HEAD_DIM)),
        'g_cq': gain((L, MLA_Q_RANK)),
        'g_ckv': gain((L, MLA_KV_RANK)),
        'w_uq': normal((L, MLA_Q_RANK, MLA_HEADS * (MLA_NOPE_DIM + MLA_ROPE_DIM)), MLA_Q_RANK ** -0.5),
        'w_ukv': normal((L, MLA_KV_RANK, MLA_HEADS * (MLA_NOPE_DIM + MLA_V_DIM)), MLA_KV_RANK ** -0.5),
        's5_lambda_re': -0.5 + normal((L, 2, G, P), 0.01),
        's5_lambda_im': jnp.pi * jnp.arange(P, dtype=f32) + normal((L, 2, G, P), 0.01),
        's5_log_dt': jax.random.uniform(next(keys), (L, 2, G), f32, math.log(1e-3), math.log(1e-1)),
        's5_b_re': normal((L, 2, G, P, H), (2 * H) ** -0.5),
        's5_b_im': normal((L, 2, G, P, H), (2 * H) ** -0.5),
        's5_c_re': normal((L, 2, G, H, P), (2 * P) ** -0.5),
        's5_c_im': normal((L, 2, G, H, P), (2 * P) ** -0.5),
        's5_d': normal((L, G, H), 1.0),
        'w_glu': normal((L, S5_WIDTH, 2 * S5_WIDTH), S5_WIDTH ** -0.5),
        'w_branch': normal((L, N_BRANCHES, BRANCH_WIDTH, D), BRANCH_WIDTH ** -0.5),
        'w_merge_gate': normal((L, D, N_BRANCHES * D), D ** -0.5),
        'b_merge_gate': normal((L, N_BRANCHES * D), 0.02),
        'w_out': normal((L, D, D), D ** -0.5),
        'ffn_w1': normal((N_DENSE, D, D_FF), D ** -0.5),
        'ffn_w3': normal((N_DENSE, D, D_FF), D ** -0.5),
        'ffn_w2': normal((N_DENSE, D_FF, D), D_FF ** -0.5),
        'router_w': normal((N_MOE, D, N_EXPERTS), D ** -0.5),
        'router_b': normal((N_MOE, N_EXPERTS), 0.01),
        'moe_w1': normal((N_MOE, N_EXPERTS, D, D_FF), D ** -0.5),
        'moe_w3': normal((N_MOE, N_EXPERTS, D, D_FF), D ** -0.5),
        'moe_w2': normal((N_MOE, N_EXPERTS, D_FF, D), D_FF ** -0.5),
    }


def reference(x, c, ctx, c_ctx, w_mod, b_mod, g_mix_pre, g_mix_post, g_ffn_pre, g_ffn_post,
              w_in, g_q, g_k, g_cq, g_ckv, w_uq, w_ukv,
              s5_lambda_re, s5_lambda_im, s5_log_dt, s5_b_re, s5_b_im, s5_c_re, s5_c_im, s5_d, w_glu,
              w_branch, w_merge_gate, b_merge_gate, w_out,
              ffn_w1, ffn_w3, ffn_w2, router_w, router_b, moe_w1, moe_w3, moe_w2):
    n_lat = x.shape[1]
    rows = n_lat // GRID_W
    row_idx = jnp.repeat(jnp.arange(rows, dtype=jnp.int32), GRID_W)
    col_idx = jnp.tile(jnp.arange(GRID_W, dtype=jnp.int32), rows)
    gqa_cs = _grid_rope(row_idx, col_idx, GQA_HEAD_DIM)
    mla_cs = _grid_rope(row_idx, col_idx, MLA_ROPE_DIM)

    sc_l = jax.nn.silu(c)
    sc_c = jax.nn.silu(c_ctx)
    lat, cx = x, ctx
    for layer in range(DEPTH):
        last = layer == DEPTH - 1
        mod_l = jnp.split((sc_l @ w_mod[layer] + b_mod[layer])[:, None, :], 6, axis=-1)
        mod_c = jnp.split((sc_c @ w_mod[layer] + b_mod[layer])[None, None, :], 6, axis=-1)

        h_l = _modulate(_rmsnorm(lat, g_mix_pre[layer]), mod_l[0], mod_l[1])
        h_c = _modulate(_rmsnorm(cx, g_mix_pre[layer]), mod_c[0], mod_c[1])
        y_c, y_l = _token_mixer(
            h_c, h_l, w_in[layer], g_q[layer], g_k[layer], g_cq[layer], g_ckv[layer], w_uq[layer], w_ukv[layer],
            s5_lambda_re[layer], s5_lambda_im[layer], s5_log_dt[layer], s5_b_re[layer], s5_b_im[layer],
            s5_c_re[layer], s5_c_im[layer], s5_d[layer], w_glu[layer],
            w_branch[layer], w_merge_gate[layer], b_merge_gate[layer], w_out[layer],
            gqa_cs, mla_cs, not last)
        lat = lat + mod_l[2] * _rmsnorm(y_l, g_mix_post[layer])
        if not last:
            cx = cx + mod_c[2] * _rmsnorm(y_c, g_mix_post[layer])

        f_l = _channel_mixer(_modulate(_rmsnorm(lat, g_ffn_pre[layer]), mod_l[3], mod_l[4]), layer,
                             ffn_w1, ffn_w3, ffn_w2, router_w, router_b, moe_w1, moe_w3, moe_w2)
        lat = lat + mod_l[5] * _rmsnorm(f_l, g_ffn_post[layer])
        if not last:
            f_c = _channel_mixer(_modulate(_rmsnorm(cx, g_ffn_pre[layer]), mod_c[3], mod_c[4]), layer,
                                 ffn_w1, ffn_w3, ffn_w2, router_w, router_b, moe_w1, moe_w3, moe_w2)
            cx = cx + mod_c[5] * _rmsnorm(f_c, g_ffn_post[layer])
    return lat
```

```python
import functools
import math

import jax
import jax.numpy as jnp
import numpy as np
from jax import lax
from jax.experimental import pallas as pl
from jax.experimental.pallas import tpu as pltpu

F32 = jnp.float32
BF16 = jnp.bfloat16

GRID_W = 64
ROPE_THETA = 10000.0
NORM_EPS = 1e-6

GQA_HEADS = 8
GQA_KV_HEADS = 2
GQA_HEAD_DIM = 64
GQA_SCALE = 1.0 / math.sqrt(GQA_HEAD_DIM)

MLA_HEADS = 8
MLA_NOPE_DIM = 64
MLA_ROPE_DIM = 32
MLA_V_DIM = 64
MLA_Q_RANK = 384
MLA_KV_RANK = 256
MLA_SCALE = 1.0 / math.sqrt(MLA_NOPE_DIM + MLA_ROPE_DIM)

S5_WIDTH = 512
S5_GROUP = 16
S5_GROUPS = S5_WIDTH // S5_GROUP
S5_STATE = 64
S5_N = S5_GROUPS * S5_STATE
S5_SEGS = 8

N_BRANCHES = 3
N_EXPERTS = 8

LANES = 128
HEAD_SLOT = 128

C_QG = 0
C_QM = C_QG + GQA_HEADS * GQA_HEAD_DIM
C_KD = C_QM + MLA_Q_RANK
C_VD = C_KD + 2 * GQA_KV_HEADS * GQA_HEAD_DIM
C_CKV = C_VD + 2 * GQA_KV_HEADS * GQA_HEAD_DIM
C_U = C_CKV + MLA_KV_RANK
C_KR = C_U + S5_WIDTH
N_IN_PACKED = C_KR + LANES

VMEM_LIMIT = 56 * 1024 * 1024


def _cparams(n_axes):
    return pltpu.CompilerParams(dimension_semantics=("arbitrary",) * n_axes,
                                vmem_limit_bytes=VMEM_LIMIT)


def _dot(a, b):
    return jnp.dot(a, b, preferred_element_type=F32)


def _rms(x, g):
    return x * lax.rsqrt(jnp.mean(x * x, axis=-1, keepdims=True) + NORM_EPS) * g


def _full(shape):
    n = len(shape)
    return pl.BlockSpec(shape, lambda *_: (0,) * n)


def _mod_kernel(c_ref, w_ref, b_ref, o_ref):
    c = c_ref[...]
    sc = c * jax.nn.sigmoid(c)
    o_ref[...] = jnp.dot(sc, w_ref[...], preferred_element_type=F32,
                         precision=lax.Precision.HIGHEST) + b_ref[...]


def _modulation(cvec, w, b):
    R, D = cvec.shape
    N = w.shape[1]
    tn = 1536
    return pl.pallas_call(
        _mod_kernel,
        grid=(N // tn,),
        in_specs=[_full((R, D)),
                  pl.BlockSpec((D, tn), lambda j: (0, j)),
                  pl.BlockSpec((1, tn), lambda j: (0, j))],
        out_specs=pl.BlockSpec((R, tn), lambda j: (0, j)),
        out_shape=jax.ShapeDtypeStruct((R, N), F32),
        compiler_params=_cparams(1),
        name="modulation",
    )(cvec, w, b.reshape(1, N))


def _seg_meansq(x, bd_ref, width, seg):
    x2 = x * x
    hi = x2.astype(BF16)
    lo = (x2 - hi.astype(F32)).astype(BF16)
    bd = bd_ref[0:width, 0:width]
    return (_dot(hi, bd) + _dot(lo, bd)) * (1.0 / seg)


def _rope(x, tab_ref, sh1, sh2):
    return (x * tab_ref[0]
            + pltpu.roll(x, sh1, axis=1) * tab_ref[1]
            + pltpu.roll(x, sh2, axis=1) * tab_ref[2])


def _in_kernel(*refs, rope):
    (x_ref, sh_ref, sc_ref, gpre_ref, win_ref, gq_ref, gk_ref, gcq_ref, gckv_ref,
     wuq_ref, wuk_ref, wuv_ref, ekr_ref, bd_ref) = refs[:14]
    if rope:
        rg_ref, rq_ref, rk_ref = refs[14:17]
        outs = refs[17:]
    else:
        outs = refs[14:]
    h_out, qg_out, qm_out, kd_out, vd_out, km_out, vm_out, u_out = outs

    x = x_ref[...]
    h = _rms(x, gpre_ref[...]) * (1.0 + sc_ref[...]) + sh_ref[...]
    hb = h.astype(BF16)
    h_out[...] = hb
    z = _dot(hb, win_ref[...])

    qg = z[:, C_QG:C_QM]
    qg = qg * lax.rsqrt(_seg_meansq(qg, bd_ref, C_QM - C_QG, GQA_HEAD_DIM) + NORM_EPS) * gq_ref[...]
    for c in range((C_QM - C_QG) // LANES):
        blk = qg[:, c * LANES:(c + 1) * LANES]
        if rope:
            blk = _rope(blk, rg_ref, 96, 32)
        qg_out[:, c * LANES:(c + 1) * LANES] = (blk * GQA_SCALE).astype(BF16)

    kd = z[:, C_KD:C_VD]
    kd = kd * lax.rsqrt(_seg_meansq(kd, bd_ref, C_VD - C_KD, GQA_HEAD_DIM) + NORM_EPS) * gk_ref[...]
    for c in range((C_VD - C_KD) // LANES):
        blk = kd[:, c * LANES:(c + 1) * LANES]
        if rope:
            blk = _rope(blk, rg_ref, 96, 32)
        kd_out[:, c * LANES:(c + 1) * LANES] = blk.astype(BF16)
    vd_out[...] = z[:, C_VD:C_CKV].astype(BF16)

    qm = _rms(z[:, C_QM:C_KD], gcq_ref[...]).astype(BF16)
    qm = _dot(qm, wuq_ref[...])
    for c in range(MLA_HEADS):
        blk = qm[:, c * HEAD_SLOT:(c + 1) * HEAD_SLOT]
        if rope:
            blk = _rope(blk, rq_ref, 112, 16)
        qm_out[:, c * HEAD_SLOT:(c + 1) * HEAD_SLOT] = (blk * MLA_SCALE).astype(BF16)

    ckv = _rms(z[:, C_CKV:C_U], gckv_ref[...]).astype(BF16)
    kr = z[:, C_KR:C_KR + LANES]
    if rope:
        kr = _rope(kr, rk_ref, 112, 16)
    km = _dot(ckv, wuk_ref[...]) + _dot(kr.astype(BF16), ekr_ref[...])
    km_out[...] = km.astype(BF16)
    vm_out[...] = _dot(ckv, wuv_ref[...]).astype(BF16)

    u_out[...] = z[:, C_U:C_KR]


def _in_proj(x, shift, scale, gpre, lw, tabs, nseg):
    B, L, D = x.shape
    tm = min(256, L // nseg)
    tseg = L // nseg
    nt = tseg // tm
    rope = tabs is not None

    def tok(b, j, i):
        return (b, j * nt + i, 0)

    def tokspec(w):
        return pl.BlockSpec((None, tm, w), tok)

    def tabspec():
        return pl.BlockSpec((3, tm, LANES), lambda b, j, i: (0, j * nt + i, 0))

    vec = lambda w: pl.BlockSpec((None, 1, w), lambda b, j, i: (b, 0, 0))
    in_specs = [tokspec(D), vec(D), vec(D), _full((1, D)), _full((D, N_IN_PACKED)),
                _full((1, 512)), _full((1, 256)), _full((1, MLA_Q_RANK)), _full((1, MLA_KV_RANK)),
                _full((MLA_Q_RANK, 1024)), _full((MLA_KV_RANK, 1024)), _full((MLA_KV_RANK, 512)),
                _full((LANES, 1024)), _full((512, 512))]
    args = [x, shift, scale, gpre, lw["w_in"], lw["g_q"], lw["g_k"], lw["g_cq"], lw["g_ckv"],
            lw["w_uq"], lw["w_uk"], lw["w_uv"], lw["e_kr"], lw["bd64"]]
    if rope:
        in_specs += [tabspec(), tabspec(), tabspec()]
        args += list(tabs)
    widths = [D, 512, 1024, 256, 256, 1024, 512]
    out_specs = [tokspec(w) for w in widths]
    out_shape = [jax.ShapeDtypeStruct((B, L, w), BF16) for w in widths]
    out_specs.append(pl.BlockSpec((None, tm, S5_WIDTH), lambda b, j, i: (b, i, j)))
    out_shape.append(jax.ShapeDtypeStruct((B, tseg, nseg * S5_WIDTH), F32))
    return pl.pallas_call(
        functools.partial(_in_kernel, rope=rope),
        grid=(B, nseg, nt),
        in_specs=in_specs,
        out_specs=out_specs,
        out_shape=out_shape,
        compiler_params=_cparams(3),
        name="in_proj_rope" if rope else "in_proj",
    )(*args)


def _attn_kernel(q_ref, k_ref, v_ref, o_ref, ve_ref, vo_ref, *, gqa):
    low = lax.broadcasted_iota(jnp.int32, (1, LANES), 1) < (LANES // 2)

    @pl.when(pl.program_id(2) == 0)
    def _():
        v = v_ref[...].astype(F32)
        zero = jnp.zeros_like(v)
        one = jnp.ones_like(v)
        ve_ref[:, 0:LANES] = jnp.where(low, v, zero).astype(BF16)
        ve_ref[:, LANES:2 * LANES] = jnp.where(low, one, zero).astype(BF16)
        vo_ref[:, 0:LANES] = jnp.where(low, zero, v).astype(BF16)
        vo_ref[:, LANES:2 * LANES] = jnp.where(low, zero, one).astype(BF16)

    acc = None
    for par in range(2):
        if gqa:
            q = q_ref[...]
            keep = low if par == 0 else jnp.logical_not(low)
            qh = jnp.where(keep, q.astype(F32), 0.0).astype(BF16)
            kh = k_ref[...]
        else:
            qh = q_ref[:, par * HEAD_SLOT:(par + 1) * HEAD_SLOT]
            kh = k_ref[:, par * HEAD_SLOT:(par + 1) * HEAD_SLOT]
        s = lax.dot_general(qh, kh, (((1,), (1,)), ((), ())), preferred_element_type=F32)
        m = jnp.max(s, axis=1, keepdims=True)
        p = jnp.exp(s - m).astype(BF16)
        t = _dot(p, (ve_ref if par == 0 else vo_ref)[...])
        acc = t if acc is None else acc + t
    o_ref[...] = (acc[:, 0:LANES] / acc[:, LANES:2 * LANES]).astype(o_ref.dtype)


def _attention(q, k, v, gqa):
    B, Lq, _ = q.shape
    Lk = k.shape[1]
    tq = min(256, Lq)
    n_pairs = 4
    if gqa:
        q_spec = pl.BlockSpec((None, tq, LANES), lambda b, p, i: (b, i, p))
        k_spec = pl.BlockSpec((None, Lk, LANES), lambda b, p, i: (b, 0, p // 2))
        v_spec = pl.BlockSpec((None, Lk, LANES), lambda b, p, i: (b, 0, p // 2))
    else:
        q_spec = pl.BlockSpec((None, tq, 2 * HEAD_SLOT), lambda b, p, i: (b, i, p))
        k_spec = pl.BlockSpec((None, Lk, 2 * HEAD_SLOT), lambda b, p, i: (b, 0, p))
        v_spec = pl.BlockSpec((None, Lk, LANES), lambda b, p, i: (b, 0, p))
    return pl.pallas_call(
        functools.partial(_attn_kernel, gqa=gqa),
        grid=(B, n_pairs, Lq // tq),
        in_specs=[q_spec, k_spec, v_spec],
        out_specs=pl.BlockSpec((None, tq, LANES), lambda b, p, i: (b, i, p)),
        out_shape=jax.ShapeDtypeStruct((B, Lq, n_pairs * LANES), BF16),
        scratch_shapes=[pltpu.VMEM((Lk, 2 * LANES), BF16), pltpu.VMEM((Lk, 2 * LANES), BF16)],
        compiler_params=_cparams(3),
        name="attn_gqa" if gqa else "attn_mla",
    )(q, k, v)


S5_COLS = 512


def _s5_kernel(u_ref, bre_ref, bim_ref, cre_ref, cim_ref, a_ref, at_ref, init_ref,
               y_ref, fin_ref, vre, vim, st, *, rev, ti):
    ps = pl.program_id(1)
    ch = pl.program_id(2)
    n = S5_N

    @pl.when((ps == 0) & (ch == 0))
    def _():
        st[...] = jnp.zeros_like(st)

    @pl.when((ps == 1) & (ch == 0))
    def _():
        ends = [(st[0, j:j + 1, :], st[1, j:j + 1, :]) for j in range(S5_SEGS)]
        atr, ati = at_ref[0], at_ref[1]
        cr, ci = init_ref[0], init_ref[1]
        order = range(S5_SEGS - 1, -1, -1) if rev else range(S5_SEGS)
        for j in order:
            st[0, j:j + 1, :] = cr
            st[1, j:j + 1, :] = ci
            er, ei = ends[j]
            cr, ci = atr * cr - ati * ci + er, atr * ci + ati * cr + ei
        fin_ref[0] = cr
        fin_ref[1] = ci

    ub = u_ref[...].reshape(ti * S5_SEGS, S5_WIDTH).astype(BF16)
    vre[...] = _dot(ub, bre_ref[...]).reshape(ti, S5_SEGS, n)
    vim[...] = _dot(ub, bim_ref[...]).reshape(ti, S5_SEGS, n)

    def scan(store):
        for cb in range(n // S5_COLS):
            cols = slice(cb * S5_COLS, (cb + 1) * S5_COLS)
            ar = jnp.broadcast_to(a_ref[0, :, cols], (S5_SEGS, S5_COLS))
            ai = jnp.broadcast_to(a_ref[1, :, cols], (S5_SEGS, S5_COLS))

            def body(i, carry):
                xr, xi = carry
                idx = (ti - 1 - i) if rev else i
                nr = ar * xr - ai * xi + vre[idx, :, cols]
                ni = ar * xi + ai * xr + vim[idx, :, cols]
                if store:
                    vre[idx, :, cols] = nr
                    vim[idx, :, cols] = ni
                return nr, ni

            xr, xi = lax.fori_loop(0, ti, body, (st[0, :, cols], st[1, :, cols]), unroll=4)
            st[0, :, cols] = xr
            st[1, :, cols] = xi

    @pl.when(ps == 0)
    def _():
        scan(False)

    @pl.when(ps == 1)
    def _():
        scan(True)
        xr = vre[...].reshape(ti * S5_SEGS, n).astype(BF16)
        xi = vim[...].reshape(ti * S5_SEGS, n).astype(BF16)
        y = _dot(xr, cre_ref[...]) + _dot(xi, cim_ref[...])
        y_ref[...] = y.reshape(ti, S5_SEGS, S5_WIDTH)


def _s5_scan(u, sw, init, rev):
    B, T, _, _ = u.shape
    ti = min(64, T)
    nch = T // ti
    n = S5_N

    def chunk(c):
        return (nch - 1 - c) if rev else c

    d = 1 if rev else 0
    return pl.pallas_call(
        functools.partial(_s5_kernel, rev=rev, ti=ti),
        grid=(B, 2, nch),
        in_specs=[pl.BlockSpec((None, ti, S5_SEGS, S5_WIDTH), lambda b, p, c: (b, chunk(c), 0, 0)),
                  _full((S5_WIDTH, n)), _full((S5_WIDTH, n)), _full((n, S5_WIDTH)), _full((n, S5_WIDTH)),
                  _full((2, 1, n)), _full((2, 1, n)),
                  pl.BlockSpec((None, 2, 1, n), lambda b, p, c: (b, 0, 0, 0))],
        out_specs=[pl.BlockSpec((None, ti, S5_SEGS, S5_WIDTH),
                                lambda b, p, c: (b, chunk(c * p), 0, 0)),
                   pl.BlockSpec((None, 2, 1, n), lambda b, p, c: (b, 0, 0, 0))],
        out_shape=[jax.ShapeDtypeStruct(u.shape, F32), jax.ShapeDtypeStruct((B, 2, 1, n), F32)],
        scratch_shapes=[pltpu.VMEM((ti, S5_SEGS, n), F32), pltpu.VMEM((ti, S5_SEGS, n), F32),
                        pltpu.VMEM((2, S5_SEGS, n), F32)],
        compiler_params=_cparams(3),
        name="s5_bwd" if rev else "s5_fwd",
    )(u, sw["b_re"][d], sw["b_im"][d], sw["c_re"][d], sw["c_im_neg"][d],
      sw["a"][d], sw["a_t"][(d, T)], init)


def _glu_kernel(u_ref, yf_ref, yb_ref, d_ref, w_ref, o_ref):
    y = d_ref[...] * u_ref[...] + yf_ref[...] + yb_ref[...]
    g = 0.5 * y * (1.0 + jnp.tanh(math.sqrt(2.0 / math.pi) * (y + 0.044715 * (y * y * y))))
    t = _dot(g.astype(BF16), w_ref[...])
    o_ref[...] = (t[:, 0:S5_WIDTH] * jax.nn.sigmoid(t[:, S5_WIDTH:2 * S5_WIDTH])).astype(o_ref.dtype)


def _s5_glu(u, yf, yb, dskip, wglu, nseg):
    B, T, _ = u.shape
    tm = min(256, T)
    nt = T // tm
    seg = pl.BlockSpec((None, tm, S5_WIDTH), lambda b, j, i: (b, i, j))
    return pl.pallas_call(
        _glu_kernel,
        grid=(B, nseg, nt),
        in_specs=[seg, seg, seg, _full((1, S5_WIDTH)), _full((S5_WIDTH, 2 * S5_WIDTH))],
        out_specs=pl.BlockSpec((None, tm, S5_WIDTH), lambda b, j, i: (b, j * nt + i, 0)),
        out_shape=jax.ShapeDtypeStruct((B, T * nseg, S5_WIDTH), BF16),
        compiler_params=_cparams(3),
        name="s5_glu",
    )(u, yf, yb, dskip, wglu)


def _merge_kernel(h_ref, ya_ref, yb_ref, yc_ref, x_ref, gate_ref, wmg_ref, bmg_ref, wbr_ref, wout_ref,
                  gpost_ref, o_ref):
    h = h_ref[...]
    d = x_ref.shape[-1]
    acc = None
    for n, y_ref in enumerate((ya_ref, yb_ref, yc_ref)):
        g = jax.nn.sigmoid(_dot(h, wmg_ref[:, n * d:(n + 1) * d]) + bmg_ref[:, n * d:(n + 1) * d])
        t = g * _dot(y_ref[...], wbr_ref[n])
        acc = t if acc is None else acc + t
    y = _dot(acc.astype(BF16), wout_ref[...])
    o_ref[...] = x_ref[...] + gate_ref[...] * _rms(y, gpost_ref[...])


def _merge(h, ya, yb, yc, x, gate, lw):
    B, L, D = x.shape
    tm = min(256, L)
    tok = lambda w: pl.BlockSpec((None, tm, w), lambda b, i: (b, i, 0))
    return pl.pallas_call(
        _merge_kernel,
        grid=(B, L // tm),
        in_specs=[tok(D), tok(512), tok(512), tok(512), tok(D),
                  pl.BlockSpec((None, 1, D), lambda b, i: (b, 0, 0)),
                  _full((D, N_BRANCHES * D)), _full((1, N_BRANCHES * D)), _full((N_BRANCHES, 512, D)),
                  _full((D, D)), _full((1, D))],
        out_specs=tok(D),
        out_shape=jax.ShapeDtypeStruct((B, L, D), F32),
        compiler_params=_cparams(2),
        name="merge",
    )(h, ya, yb, yc, x, gate, lw["w_mg"], lw["b_mg"], lw["w_branch"], lw["w_out"], lw["g_mix_post"])


def _ffn_kernel(x_ref, sh_ref, sc_ref, gate_ref, gpre_ref, gpost_ref, w1_ref, w3_ref, w2_ref, o_ref,
                h_s, acc_s):
    f = pl.program_id(2)

    @pl.when(f == 0)
    def _():
        h = _rms(x_ref[...], gpre_ref[...]) * (1.0 + sc_ref[...]) + sh_ref[...]
        h_s[...] = h.astype(BF16)
        acc_s[...] = jnp.zeros_like(acc_s)

    h = h_s[...]
    a = _dot(h, w1_ref[...])
    g = (a * jax.nn.sigmoid(a)) * _dot(h, w3_ref[...])
    acc_s[...] += _dot(g.astype(BF16), w2_ref[...])

    @pl.when(f == pl.num_programs(2) - 1)
    def _():
        o_ref[...] = x_ref[...] + gate_ref[...] * _rms(acc_s[...], gpost_ref[...])


def _ffn(x, shift, scale, gate, gpre, gpost, w1, w3, w2):
    B, L, D = x.shape
    dff = w1.shape[1]
    tm = min(512, L)
    tf = dff // 2 if (dff // 2) % LANES == 0 else dff
    tok = pl.BlockSpec((None, tm, D), lambda b, i, f: (b, i, 0))
    vec = pl.BlockSpec((None, 1, D), lambda b, i, f: (b, 0, 0))
    return pl.pallas_call(
        _ffn_kernel,
        grid=(B, L // tm, dff // tf),
        in_specs=[tok, vec, vec, vec, _full((1, D)), _full((1, D)),
                  pl.BlockSpec((D, tf), lambda b, i, f: (0, f)),
                  pl.BlockSpec((D, tf), lambda b, i, f: (0, f)),
                  pl.BlockSpec((tf, D), lambda b, i, f: (f, 0))],
        out_specs=tok,
        out_shape=jax.ShapeDtypeStruct((B, L, D), F32),
        scratch_shapes=[pltpu.VMEM((tm, D), BF16), pltpu.VMEM((tm, D), F32)],
        compiler_params=_cparams(3),
        name="ffn",
    )(x, shift, scale, gate, gpre, gpost, w1, w3, w2)


def _moe_kernel(x_ref, sh_ref, sc_ref, gate_ref, gpre_ref, gpost_ref, rw_ref, rb_ref,
                w1_ref, w3_ref, w2_ref, o_ref, h_s, acc_s, gates_s):
    e = pl.program_id(2)
    f = pl.program_id(3)
    lane = lax.broadcasted_iota(jnp.int32, (1, LANES), 1)

    @pl.when((e == 0) & (f == 0))
    def _():
        h = _rms(x_ref[...], gpre_ref[...]) * (1.0 + sc_ref[...]) + sh_ref[...]
        h_s[...] = h.astype(BF16)
        acc_s[...] = jnp.zeros_like(acc_s)
        logits = jnp.dot(h, rw_ref[...], preferred_element_type=F32,
                         precision=lax.Precision.HIGHEST) + rb_ref[...]
        m1 = jnp.max(logits, axis=1, keepdims=True)
        i1 = jnp.min(jnp.where(logits == m1, lane, LANES), axis=1, keepdims=True)
        rest = jnp.where(lane == i1, -jnp.inf, logits)
        m2 = jnp.max(rest, axis=1, keepdims=True)
        i2 = jnp.min(jnp.where(rest == m2, lane, LANES), axis=1, keepdims=True)
        e2 = jnp.exp(m2 - m1)
        den = 1.0 + e2
        gates_s[...] = jnp.where(lane == i1, 1.0 / den, 0.0) + jnp.where(lane == i2, e2 / den, 0.0)

    ge = jnp.sum(jnp.where(lane == e, gates_s[...], 0.0), axis=1, keepdims=True)
    h = h_s[...]
    a = _dot(h, w1_ref[...])
    g = (a * jax.nn.sigmoid(a)) * _dot(h, w3_ref[...])
    acc_s[...] += ge * _dot(g.astype(BF16), w2_ref[...])

    @pl.when((e == pl.num_programs(2) - 1) & (f == pl.num_programs(3) - 1))
    def _():
        o_ref[...] = x_ref[...] + gate_ref[...] * _rms(acc_s[...], gpost_ref[...])


def _moe(x, shift, scale, gate, gpre, gpost, rw, rb, w1, w3, w2):
    B, L, D = x.shape
    ne, _, dff = w1.shape
    tm = min(1024, L)
    tf = dff // 2 if (dff // 2) % LANES == 0 else dff
    tok = pl.BlockSpec((None, tm, D), lambda b, i, e, f: (b, i, 0))
    vec = pl.BlockSpec((None, 1, D), lambda b, i, e, f: (b, 0, 0))
    return pl.pallas_call(
        _moe_kernel,
        grid=(B, L // tm, ne, dff // tf),
        in_specs=[tok, vec, vec, vec, _full((1, D)), _full((1, D)), _full((D, LANES)), _full((1, LANES)),
                  pl.BlockSpec((None, D, tf), lambda b, i, e, f: (e, 0, f)),
                  pl.BlockSpec((None, D, tf), lambda b, i, e, f: (e, 0, f)),
                  pl.BlockSpec((None, tf, D), lambda b, i, e, f: (e, f, 0))],
        out_specs=tok,
        out_shape=jax.ShapeDtypeStruct((B, L, D), F32),
        scratch_shapes=[pltpu.VMEM((tm, D), BF16), pltpu.VMEM((tm, D), F32), pltpu.VMEM((tm, LANES), F32)],
        compiler_params=_cparams(4),
        name="moe",
    )(x, shift, scale, gate, gpre, gpost, rw, rb, w1, w3, w2)


def _rope_tables(n_lat):
    pos = jnp.arange(n_lat, dtype=jnp.int32)
    row = (pos // GRID_W).astype(F32)
    col = (pos % GRID_W).astype(F32)

    def cos_sin(rot_dim):
        axis_dim = rot_dim // 2
        inv_freq = ROPE_THETA ** (-jnp.arange(0, axis_dim, 2, dtype=F32) / axis_dim)
        ang = jnp.concatenate([row[:, None] * inv_freq, col[:, None] * inv_freq], axis=-1)
        return jnp.cos(ang), jnp.sin(ang)

    lane = np.arange(LANES)

    def table(cos, sin, half, start, stop, fill):
        rel = (lane - start) % (2 * half)
        inside = (lane >= start) & (lane < stop)
        idx = rel % half
        c = jnp.where(inside, cos[:, idx], fill)
        s_up = jnp.where(inside & (rel < half), -sin[:, idx], 0.0)
        s_dn = jnp.where(inside & (rel >= half), sin[:, idx], 0.0)
        return jnp.stack([c, s_up, s_dn]).astype(F32)

    gc, gs = cos_sin(GQA_HEAD_DIM)
    mc, ms = cos_sin(MLA_ROPE_DIM)
    t_gqa = table(gc, gs, GQA_HEAD_DIM // 2, 0, LANES, 1.0)
    t_mq = table(mc, ms, MLA_ROPE_DIM // 2, MLA_NOPE_DIM, MLA_NOPE_DIM + MLA_ROPE_DIM, 1.0)
    t_kr = table(mc, ms, MLA_ROPE_DIM // 2, 0, MLA_ROPE_DIM, 1.0)
    return t_gqa, t_mq, t_kr


def _pack_layer(p, l):
    D = p["w_in"].shape[1]
    w = p["w_in"][l]
    offs = np.cumsum([0, 512, MLA_Q_RANK, 128, 128, MLA_KV_RANK, MLA_ROPE_DIM, S5_WIDTH])
    qg, qm, kg, vg, ckv, kr, u = [w[:, offs[i]:offs[i + 1]] for i in range(7)]
    dup = lambda t: jnp.concatenate([t[:, 0:64], t[:, 0:64], t[:, 64:128], t[:, 64:128]], axis=1)
    w_in = jnp.concatenate([qg, qm, dup(kg), dup(vg), ckv, u, kr,
                            jnp.zeros((D, LANES - MLA_ROPE_DIM), F32)], axis=1).astype(BF16)

    wq = p["w_uq"][l].reshape(MLA_Q_RANK, MLA_HEADS, MLA_NOPE_DIM + MLA_ROPE_DIM)
    w_uq = jnp.pad(wq, ((0, 0), (0, 0), (0, HEAD_SLOT - wq.shape[-1]))).reshape(MLA_Q_RANK, -1).astype(BF16)
    wkv = p["w_ukv"][l].reshape(MLA_KV_RANK, MLA_HEADS, MLA_NOPE_DIM + MLA_V_DIM)
    w_uk = jnp.pad(wkv[:, :, :MLA_NOPE_DIM], ((0, 0), (0, 0), (0, HEAD_SLOT - MLA_NOPE_DIM)))
    w_uk = w_uk.reshape(MLA_KV_RANK, -1).astype(BF16)
    w_uv = wkv[:, :, MLA_NOPE_DIM:].reshape(MLA_KV_RANK, -1).astype(BF16)
    e = np.zeros((LANES, MLA_HEADS, HEAD_SLOT), np.float32)
    for r in range(MLA_ROPE_DIM):
        e[r, :, MLA_NOPE_DIM + r] = 1.0
    bd = np.kron(np.eye(512 // GQA_HEAD_DIM, dtype=np.float32), np.ones((GQA_HEAD_DIM, GQA_HEAD_DIM), np.float32))
    return {
        "w_in": w_in,
        "g_q": jnp.tile(p["g_q"][l], GQA_HEADS).reshape(1, -1),
        "g_k": jnp.tile(p["g_k"][l], 2 * GQA_KV_HEADS).reshape(1, -1),
        "g_cq": p["g_cq"][l].reshape(1, -1),
        "g_ckv": p["g_ckv"][l].reshape(1, -1),
        "w_uq": w_uq, "w_uk": w_uk, "w_uv": w_uv,
        "e_kr": jnp.asarray(e.reshape(LANES, -1), BF16),
        "bd64": jnp.asarray(bd, BF16),
        "w_mg": p["w_merge_gate"][l].astype(BF16),
        "b_mg": p["b_merge_gate"][l].reshape(1, -1),
        "w_branch": p["w_branch"][l].astype(BF16),
        "w_out": p["w_out"][l].astype(BF16),
        "g_mix_pre": p["g_mix_pre"][l].reshape(1, -1),
        "g_mix_post": p["g_mix_post"][l].reshape(1, -1),
        "g_ffn_pre": p["g_ffn_pre"][l].reshape(1, -1),
        "g_ffn_post": p["g_ffn_post"][l].reshape(1, -1),
        "w_glu": p["w_glu"][l].astype(BF16),
        "s5_d": p["s5_d"][l].reshape(1, -1),
    }


def _cpow(ar, ai, n):
    rr, ri = None, None
    br, bi = ar, ai
    while n:
        if n & 1:
            rr, ri = (br, bi) if rr is None else (rr * br - ri * bi, rr * bi + ri * br)
        n >>= 1
        if n:
            br, bi = br * br - bi * bi, 2.0 * br * bi
    return rr, ri


def _s5_weights(p, l, seg_lens):
    G, P, H = S5_GROUPS, S5_STATE, S5_GROUP
    eye = jnp.eye(G, dtype=F32)
    out = {"b_re": [], "b_im": [], "c_re": [], "c_im_neg": [], "a": [], "a_t": {}}
    for d in range(2):
        lr = jnp.minimum(p["s5_lambda_re"][l, d], -1e-4)
        li = p["s5_lambda_im"][l, d]
        dt = jnp.exp(p["s5_log_dt"][l, d])[:, None]
        mag = jnp.exp(lr * dt)
        abr, abi = mag * jnp.cos(li * dt), mag * jnp.sin(li * dt)
        den = lr * lr + li * li
        fr = ((abr - 1.0) * lr + abi * li) / den
        fi = (abi * lr - (abr - 1.0) * li) / den
        br, bi = p["s5_b_re"][l, d], p["s5_b_im"][l, d]
        bbr = fr[..., None] * br - fi[..., None] * bi
        bbi = fr[..., None] * bi + fi[..., None] * br
        blk_b = lambda t: jnp.einsum("gph,gk->ghkp", t, eye).reshape(G * H, G * P).astype(BF16)
        blk_c = lambda t: jnp.einsum("ghp,gk->gpkh", t, eye).reshape(G * P, G * H).astype(BF16)
        out["b_re"].append(blk_b(bbr))
        out["b_im"].append(blk_b(bbi))
        out["c_re"].append(blk_c(p["s5_c_re"][l, d]))
        out["c_im_neg"].append(blk_c(-p["s5_c_im"][l, d]))
        out["a"].append(jnp.stack([abr.reshape(1, -1), abi.reshape(1, -1)]))
        for t in seg_lens:
            tr, ti = _cpow(abr, abi, t)
            out["a_t"][(d, t)] = jnp.stack([tr.reshape(1, -1), ti.reshape(1, -1)])
    return out


def _seg_view(t):
    B, T, _ = t.shape
    return t.reshape(B, T, S5_SEGS, S5_WIDTH)


def _to_segments(t):
    B, L, W = t.shape
    return t.reshape(B, S5_SEGS, L // S5_SEGS, W).transpose(0, 2, 1, 3).reshape(B, L // S5_SEGS, S5_SEGS * W)


def kernel(x, c, ctx, c_ctx, w_mod, b_mod, g_mix_pre, g_mix_post, g_ffn_pre, g_ffn_post, w_in, g_q, g_k, g_cq, g_ckv, w_uq, w_ukv, s5_lambda_re, s5_lambda_im, s5_log_dt, s5_b_re, s5_b_im, s5_c_re, s5_c_im, s5_d, w_glu, w_branch, w_merge_gate, b_merge_gate, w_out, ffn_w1, ffn_w3, ffn_w2, router_w, router_b, moe_w1, moe_w3, moe_w2):
    p = dict(w_in=w_in, g_q=g_q, g_k=g_k, g_cq=g_cq, g_ckv=g_ckv, w_uq=w_uq, w_ukv=w_ukv,
             s5_lambda_re=s5_lambda_re, s5_lambda_im=s5_lambda_im, s5_log_dt=s5_log_dt,
             s5_b_re=s5_b_re, s5_b_im=s5_b_im, s5_c_re=s5_c_re, s5_c_im=s5_c_im, s5_d=s5_d, w_glu=w_glu,
             w_branch=w_branch, w_merge_gate=w_merge_gate, b_merge_gate=b_merge_gate, w_out=w_out,
             g_mix_pre=g_mix_pre, g_mix_post=g_mix_post, g_ffn_pre=g_ffn_pre, g_ffn_post=g_ffn_post)
    B, L, D = x.shape
    Lc = ctx.shape[1]
    depth = w_mod.shape[0]
    tabs = _rope_tables(L)
    t_lat, t_ctx = L // S5_SEGS, Lc // S5_SEGS

    cvec = jnp.concatenate([c, c_ctx[None, :], jnp.zeros((8 - B - 1, D), F32)], axis=0)
    lat, cx = x, ctx
    for layer in range(depth):
        last = layer == depth - 1
        lw = _pack_layer(p, layer)
        sw = _s5_weights(p, layer, (t_lat, t_ctx))
        mod = _modulation(cvec, w_mod[layer], b_mod[layer])
        mod_l = [mod[:B, i * D:(i + 1) * D].reshape(B, 1, D) for i in range(6)]
        mod_c = [jnp.broadcast_to(mod[B, i * D:(i + 1) * D], (B, 1, D)) for i in range(6)]

        h_l, qg_l, qm_l, kd_l, vd_l, km_l, vm_l, u_l = _in_proj(
            lat, mod_l[0], mod_l[1], lw["g_mix_pre"], lw, tabs, S5_SEGS)
        h_c, qg_c, qm_c, kd_c, vd_c, km_c, vm_c, u_c = _in_proj(
            cx, mod_c[0], mod_c[1], lw["g_mix_pre"], lw, None, 1)
        cat = lambda a, b: jnp.concatenate([a, b], axis=1)
        ya_l = _attention(qg_l, cat(kd_c, kd_l), cat(vd_c, vd_l), gqa=True)
        yb_l = _attention(qm_l, cat(km_c, km_l), cat(vm_c, vm_l), gqa=False)

        u_cs = _to_segments(u_c)
        zero = jnp.zeros((B, 2, 1, S5_N), F32)
        yf_c, fin_f = _s5_scan(_seg_view(u_cs), sw, zero, rev=False)
        yb_c, fin_b = _s5_scan(_seg_view(u_cs), sw, zero, rev=True)
        yf_l, _ = _s5_scan(_seg_view(u_l), sw, fin_f, rev=False)
        yr_l, _ = _s5_scan(_seg_view(u_l), sw, fin_b, rev=True)
        flat = lambda t: t.reshape(t.shape[0], t.shape[1], S5_SEGS * S5_WIDTH)
        yc_l = _s5_glu(u_l, flat(yf_l), flat(yr_l), lw["s5_d"], lw["w_glu"], S5_SEGS)

        lat = _merge(h_l, ya_l, yb_l, yc_l, lat, mod_l[2], lw)
        if not last:
            ya_c = _attention(qg_c, kd_c, vd_c, gqa=True)
            yb_c2 = _attention(qm_c, km_c, vm_c, gqa=False)
            yc_c = _s5_glu(u_cs, flat(yf_c), flat(yb_c), lw["s5_d"], lw["w_glu"], S5_SEGS)
            cx = _merge(h_c, ya_c, yb_c2, yc_c, cx, mod_c[2], lw)

        i = layer // 2
        if layer % 2 == 0:
            w1, w3, w2 = ffn_w1[i].astype(BF16), ffn_w3[i].astype(BF16), ffn_w2[i].astype(BF16)
            mix = lambda t, m: _ffn(t, m[3], m[4], m[5], lw["g_ffn_pre"], lw["g_ffn_post"], w1, w3, w2)
        else:
            w1, w3, w2 = moe_w1[i].astype(BF16), moe_w3[i].astype(BF16), moe_w2[i].astype(BF16)
            rw = jnp.pad(router_w[i], ((0, 0), (0, LANES - N_EXPERTS)))
            rb = jnp.concatenate([router_b[i], jnp.full((LANES - N_EXPERTS,), -1e30, F32)]).reshape(1, LANES)
            mix = lambda t, m: _moe(t, m[3], m[4], m[5], lw["g_ffn_pre"], lw["g_ffn_post"], rw, rb, w1, w3, w2)
        lat = mix(lat, mod_l)
        if not last:
            cx = mix(cx, mod_c)
    return lat
```

```python
import functools
import math

import jax
import jax.numpy as jnp
import numpy as np
from jax import lax
from jax.experimental import pallas as pl
from jax.experimental.pallas import tpu as pltpu

F32 = jnp.float32
BF16 = jnp.bfloat16

GRID_W = 64
ROPE_THETA = 10000.0
NORM_EPS = 1e-6

GQA_HEADS = 8
GQA_KV_HEADS = 2
GQA_HEAD_DIM = 64
GQA_SCALE = 1.0 / math.sqrt(GQA_HEAD_DIM)

MLA_HEADS = 8
MLA_NOPE_DIM = 64
MLA_ROPE_DIM = 32
MLA_V_DIM = 64
MLA_Q_RANK = 384
MLA_KV_RANK = 256
MLA_SCALE = 1.0 / math.sqrt(MLA_NOPE_DIM + MLA_ROPE_DIM)
LOG2E = math.log2(math.e)

S5_WIDTH = 512
S5_GROUP = 16
S5_GROUPS = S5_WIDTH // S5_GROUP
S5_STATE = 64
S5_N = S5_GROUPS * S5_STATE
S5_SEGS = 8

N_BRANCHES = 3
N_EXPERTS = 8

LANES = 128
HEAD_SLOT = 128

C_QG = 0
C_QM = C_QG + GQA_HEADS * GQA_HEAD_DIM
C_KD = C_QM + MLA_Q_RANK
C_VD = C_KD + 2 * GQA_KV_HEADS * GQA_HEAD_DIM
C_CKV = C_VD + 2 * GQA_KV_HEADS * GQA_HEAD_DIM
C_U = C_CKV + MLA_KV_RANK
C_KR = C_U + S5_WIDTH
N_IN_PACKED = C_KR + LANES

VMEM_LIMIT = 56 * 1024 * 1024


def _cparams(n_axes):
    return pltpu.CompilerParams(dimension_semantics=("arbitrary",) * n_axes,
                                vmem_limit_bytes=VMEM_LIMIT)


def _dot(a, b):
    return jnp.dot(a, b, preferred_element_type=F32)


def _rms(x, g):
    return x * lax.rsqrt(jnp.mean(x * x, axis=-1, keepdims=True) + NORM_EPS) * g


def _full(shape):
    n = len(shape)
    return pl.BlockSpec(shape, lambda *_: (0,) * n)


def _mod_kernel(c_ref, w_ref, b_ref, o_ref):
    c = c_ref[...]
    sc = c * jax.nn.sigmoid(c)
    o_ref[...] = jnp.dot(sc, w_ref[...], preferred_element_type=F32,
                         precision=lax.Precision.HIGHEST) + b_ref[...]


def _modulation(cvec, w, b):
    R, D = cvec.shape
    N = w.shape[1]
    tn = 1536
    return pl.pallas_call(
        _mod_kernel,
        grid=(N // tn,),
        in_specs=[_full((R, D)),
                  pl.BlockSpec((D, tn), lambda j: (0, j)),
                  pl.BlockSpec((1, tn), lambda j: (0, j))],
        out_specs=pl.BlockSpec((R, tn), lambda j: (0, j)),
        out_shape=jax.ShapeDtypeStruct((R, N), F32),
        compiler_params=_cparams(1),
        name="modulation",
    )(cvec, w, b.reshape(1, N))


def _seg_meansq(x, bd_ref, width, seg):
    x2 = x * x
    hi = x2.astype(BF16)
    lo = (x2 - hi.astype(F32)).astype(BF16)
    bd = bd_ref[0:width, 0:width]
    return (_dot(hi, bd) + _dot(lo, bd)) * (1.0 / seg)


def _rope(x, tab_ref, sh1, sh2):
    return (x * tab_ref[0]
            + pltpu.roll(x, sh1, axis=1) * tab_ref[1]
            + pltpu.roll(x, sh2, axis=1) * tab_ref[2])


def _in_kernel(*refs, rope):
    (x_ref, sh_ref, sc_ref, gpre_ref, win_ref, gq_ref, gk_ref, gcq_ref, gckv_ref,
     wuq_ref, wuk_ref, wuv_ref, ekr_ref, bd_ref) = refs[:14]
    if rope:
        rg_ref, rq_ref, rk_ref = refs[14:17]
        outs = refs[17:]
    else:
        outs = refs[14:]
    h_out, qg_out, qm_out, kd_out, vd_out, km_out, vm_out, u_out = outs

    x = x_ref[...]
    h = _rms(x, gpre_ref[...]) * (1.0 + sc_ref[...]) + sh_ref[...]
    hb = h.astype(BF16)
    h_out[...] = hb
    z = _dot(hb, win_ref[...])

    qg = z[:, C_QG:C_QM]
    qg = qg * lax.rsqrt(_seg_meansq(qg, bd_ref, C_QM - C_QG, GQA_HEAD_DIM) + NORM_EPS) * gq_ref[...]
    for c in range((C_QM - C_QG) // LANES):
        blk = qg[:, c * LANES:(c + 1) * LANES]
        if rope:
            blk = _rope(blk, rg_ref, 96, 32)
        qg_out[:, c * LANES:(c + 1) * LANES] = (blk * (GQA_SCALE * LOG2E)).astype(BF16)

    kd = z[:, C_KD:C_VD]
    kd = kd * lax.rsqrt(_seg_meansq(kd, bd_ref, C_VD - C_KD, GQA_HEAD_DIM) + NORM_EPS) * gk_ref[...]
    for c in range((C_VD - C_KD) // LANES):
        blk = kd[:, c * LANES:(c + 1) * LANES]
        if rope:
            blk = _rope(blk, rg_ref, 96, 32)
        kd_out[:, c * LANES:(c + 1) * LANES] = blk.astype(BF16)
    vd_out[...] = z[:, C_VD:C_CKV].astype(BF16)

    qm = _rms(z[:, C_QM:C_KD], gcq_ref[...]).astype(BF16)
    qm = _dot(qm, wuq_ref[...])
    for c in range(MLA_HEADS):
        blk = qm[:, c * HEAD_SLOT:(c + 1) * HEAD_SLOT]
        if rope:
            blk = _rope(blk, rq_ref, 112, 16)
        qm_out[:, c * HEAD_SLOT:(c + 1) * HEAD_SLOT] = (blk * (MLA_SCALE * LOG2E)).astype(BF16)

    ckv = _rms(z[:, C_CKV:C_U], gckv_ref[...]).astype(BF16)
    kr = z[:, C_KR:C_KR + LANES]
    if rope:
        kr = _rope(kr, rk_ref, 112, 16)
    km = _dot(ckv, wuk_ref[...]) + _dot(kr.astype(BF16), ekr_ref[...])
    km_out[...] = km.astype(BF16)
    vm_out[...] = _dot(ckv, wuv_ref[...]).astype(BF16)

    u_out[...] = z[:, C_U:C_KR]


def _in_proj(x, shift, scale, gpre, lw, tabs, nseg):
    B, L, D = x.shape
    tm = min(256, L // nseg)
    tseg = L // nseg
    nt = tseg // tm
    rope = tabs is not None

    def tok(b, j, i):
        return (b, j * nt + i, 0)

    def tokspec(w):
        return pl.BlockSpec((None, tm, w), tok)

    def tabspec():
        return pl.BlockSpec((3, tm, LANES), lambda b, j, i: (0, j * nt + i, 0))

    vec = lambda w: pl.BlockSpec((None, 1, w), lambda b, j, i: (b, 0, 0))
    in_specs = [tokspec(D), vec(D), vec(D), _full((1, D)), _full((D, N_IN_PACKED)),
                _full((1, 512)), _full((1, 256)), _full((1, MLA_Q_RANK)), _full((1, MLA_KV_RANK)),
                _full((MLA_Q_RANK, 1024)), _full((MLA_KV_RANK, 1024)), _full((MLA_KV_RANK, 512)),
                _full((LANES, 1024)), _full((512, 512))]
    args = [x, shift, scale, gpre, lw["w_in"], lw["g_q"], lw["g_k"], lw["g_cq"], lw["g_ckv"],
            lw["w_uq"], lw["w_uk"], lw["w_uv"], lw["e_kr"], lw["bd64"]]
    if rope:
        in_specs += [tabspec(), tabspec(), tabspec()]
        args += list(tabs)
    widths = [D, 512, 1024, 256, 256, 1024, 512]
    out_specs = [tokspec(w) for w in widths]
    out_shape = [jax.ShapeDtypeStruct((B, L, w), BF16) for w in widths]
    out_specs.append(pl.BlockSpec((None, tm, S5_WIDTH), lambda b, j, i: (b, i, j)))
    out_shape.append(jax.ShapeDtypeStruct((B, tseg, nseg * S5_WIDTH), F32))
    return pl.pallas_call(
        functools.partial(_in_kernel, rope=rope),
        grid=(B, nseg, nt),
        in_specs=in_specs,
        out_specs=out_specs,
        out_shape=out_shape,
        compiler_params=_cparams(3),
        name="in_proj_rope" if rope else "in_proj",
    )(*args)


ONES_ROWS = 16
ATTN_KEY_CHUNK = 512


def _attn_kernel(q_ref, k_ref, v_ref, o_ref, vt_ref, s_ref, p_ref, *, gqa, chunks):
    hd = LANES // 2

    @pl.when(pl.program_id(2) == 0)
    def _():
        vt = v_ref[...].astype(F32).T
        ones = jnp.ones((ONES_ROWS, vt.shape[1]), BF16)
        for par in range(2):
            vt_ref[par, 0:hd, :] = vt[par * hd:(par + 1) * hd, :].astype(BF16)
            vt_ref[par, hd:hd + ONES_ROWS, :] = ones

    qt = q_ref[...].astype(F32).T
    qh = []
    for par in range(2):
        if gqa:
            row = lax.broadcasted_iota(jnp.int32, (LANES, 1), 0)
            keep = (row < hd) if par == 0 else (row >= hd)
            qh.append(jnp.where(keep, qt, 0.0).astype(BF16))
        else:
            qh.append(qt[par * HEAD_SLOT:(par + 1) * HEAD_SLOT, :].astype(BF16))

    def scores(par, c0, c1):
        kh = k_ref[c0:c1, :] if gqa else k_ref[c0:c1, par * HEAD_SLOT:(par + 1) * HEAD_SLOT]
        st = _dot(kh, qh[par])
        s_ref[par, c0:c1, :] = st
        return jnp.max(st, axis=0, keepdims=True)

    def probs(par, c0, c1, m):
        p_ref[par, c0:c1, :] = jnp.exp2(s_ref[par, c0:c1, :] - m).astype(BF16)

    def weighted(par, c0, c1):
        return _dot(vt_ref[par, :, c0:c1], p_ref[par, c0:c1, :])

    def fold(a, b):
        return b if a is None else a + b

    m0 = m1 = acc0 = acc1 = None
    for c0, c1 in chunks:
        sm = scores(0, c0, c1)
        m0 = sm if m0 is None else jnp.maximum(m0, sm)
    for c0, c1 in chunks:
        sm = scores(1, c0, c1)
        m1 = sm if m1 is None else jnp.maximum(m1, sm)
        probs(0, c0, c1, m0)
    for c0, c1 in chunks:
        acc0 = fold(acc0, weighted(0, c0, c1))
        probs(1, c0, c1, m1)
    for c0, c1 in chunks:
        acc1 = fold(acc1, weighted(1, c0, c1))
    outs = [a[0:hd, :] / a[hd:hd + 1, :] for a in (acc0, acc1)]
    o_ref[...] = jnp.concatenate(outs, axis=0).T.astype(o_ref.dtype)


def _attention(q, k, v, gqa):
    B, Lq, _ = q.shape
    Lk = k.shape[1]
    tq = min(512, Lq)
    n_pairs = 4
    if gqa:
        q_spec = pl.BlockSpec((None, tq, LANES), lambda b, p, i: (b, i, p))
        k_spec = pl.BlockSpec((None, Lk, LANES), lambda b, p, i: (b, 0, p // 2))
        v_spec = pl.BlockSpec((None, Lk, LANES), lambda b, p, i: (b, 0, p // 2))
    else:
        q_spec = pl.BlockSpec((None, tq, 2 * HEAD_SLOT), lambda b, p, i: (b, i, p))
        k_spec = pl.BlockSpec((None, Lk, 2 * HEAD_SLOT), lambda b, p, i: (b, 0, p))
        v_spec = pl.BlockSpec((None, Lk, LANES), lambda b, p, i: (b, 0, p))
    step = min(ATTN_KEY_CHUNK, Lk)
    chunks = tuple((c, min(c + step, Lk)) for c in range(0, Lk, step))
    return pl.pallas_call(
        functools.partial(_attn_kernel, gqa=gqa, chunks=chunks),
        grid=(B, n_pairs, Lq // tq),
        in_specs=[q_spec, k_spec, v_spec],
        out_specs=pl.BlockSpec((None, tq, LANES), lambda b, p, i: (b, i, p)),
        out_shape=jax.ShapeDtypeStruct((B, Lq, n_pairs * LANES), BF16),
        scratch_shapes=[pltpu.VMEM((2, LANES // 2 + ONES_ROWS, Lk), BF16),
                        pltpu.VMEM((2, Lk, tq), F32), pltpu.VMEM((2, Lk, tq), BF16)],
        compiler_params=_cparams(3),
        name="attn_gqa" if gqa else "attn_mla",
    )(q, k, v)


S5_COLS = 512


def _s5_kernel(u_ref, bre_ref, bim_ref, cre_ref, cim_ref, a_ref, at_ref, init_ref,
               y_ref, fin_ref, vre, vim, st, *, rev, ti):
    ps = pl.program_id(1)
    ch = pl.program_id(2)
    n = S5_N

    @pl.when((ps == 0) & (ch == 0))
    def _():
        st[...] = jnp.zeros_like(st)

    @pl.when((ps == 1) & (ch == 0))
    def _():
        ends = [(st[0, j:j + 1, :], st[1, j:j + 1, :]) for j in range(S5_SEGS)]
        atr, ati = at_ref[0], at_ref[1]
        cr, ci = init_ref[0], init_ref[1]
        order = range(S5_SEGS - 1, -1, -1) if rev else range(S5_SEGS)
        for j in order:
            st[0, j:j + 1, :] = cr
            st[1, j:j + 1, :] = ci
            er, ei = ends[j]
            cr, ci = atr * cr - ati * ci + er, atr * ci + ati * cr + ei
        fin_ref[0] = cr
        fin_ref[1] = ci

    ub = u_ref[...].reshape(ti * S5_SEGS, S5_WIDTH).astype(BF16)
    vre[...] = _dot(ub, bre_ref[...]).reshape(ti, S5_SEGS, n)
    vim[...] = _dot(ub, bim_ref[...]).reshape(ti, S5_SEGS, n)

    def scan(store):
        for cb in range(n // S5_COLS):
            cols = slice(cb * S5_COLS, (cb + 1) * S5_COLS)
            ar = jnp.broadcast_to(a_ref[0, :, cols], (S5_SEGS, S5_COLS))
            ai = jnp.broadcast_to(a_ref[1, :, cols], (S5_SEGS, S5_COLS))

            def body(i, carry):
                xr, xi = carry
                idx = (ti - 1 - i) if rev else i
                nr = ar * xr - ai * xi + vre[idx, :, cols]
                ni = ar * xi + ai * xr + vim[idx, :, cols]
                if store:
                    vre[idx, :, cols] = nr
                    vim[idx, :, cols] = ni
                return nr, ni

            xr, xi = lax.fori_loop(0, ti, body, (st[0, :, cols], st[1, :, cols]), unroll=4)
            st[0, :, cols] = xr
            st[1, :, cols] = xi

    @pl.when(ps == 0)
    def _():
        scan(False)

    @pl.when(ps == 1)
    def _():
        scan(True)
        xr = vre[...].reshape(ti * S5_SEGS, n).astype(BF16)
        xi = vim[...].reshape(ti * S5_SEGS, n).astype(BF16)
        y = _dot(xr, cre_ref[...]) + _dot(xi, cim_ref[...])
        y_ref[...] = y.reshape(ti, S5_SEGS, S5_WIDTH)


def _s5_scan(u, sw, init, rev):
    B, T, _, _ = u.shape
    ti = min(64, T)
    nch = T // ti
    n = S5_N

    def chunk(c):
        return (nch - 1 - c) if rev else c

    d = 1 if rev else 0
    return pl.pallas_call(
        functools.partial(_s5_kernel, rev=rev, ti=ti),
        grid=(B, 2, nch),
        in_specs=[pl.BlockSpec((None, ti, S5_SEGS, S5_WIDTH), lambda b, p, c: (b, chunk(c), 0, 0)),
                  _full((S5_WIDTH, n)), _full((S5_WIDTH, n)), _full((n, S5_WIDTH)), _full((n, S5_WIDTH)),
                  _full((2, 1, n)), _full((2, 1, n)),
                  pl.BlockSpec((None, 2, 1, n), lambda b, p, c: (b, 0, 0, 0))],
        out_specs=[pl.BlockSpec((None, ti, S5_SEGS, S5_WIDTH),
                                lambda b, p, c: (b, chunk(c * p), 0, 0)),
                   pl.BlockSpec((None, 2, 1, n), lambda b, p, c: (b, 0, 0, 0))],
        out_shape=[jax.ShapeDtypeStruct(u.shape, F32), jax.ShapeDtypeStruct((B, 2, 1, n), F32)],
        scratch_shapes=[pltpu.VMEM((ti, S5_SEGS, n), F32), pltpu.VMEM((ti, S5_SEGS, n), F32),
                        pltpu.VMEM((2, S5_SEGS, n), F32)],
        compiler_params=_cparams(3),
        name="s5_bwd" if rev else "s5_fwd",
    )(u, sw["b_re"][d], sw["b_im"][d], sw["c_re"][d], sw["c_im_neg"][d],
      sw["a"][d], sw["a_t"][(d, T)], init)


def _glu_kernel(u_ref, yf_ref, yb_ref, d_ref, w_ref, o_ref):
    y = d_ref[...] * u_ref[...] + yf_ref[...] + yb_ref[...]
    g = 0.5 * y * (1.0 + jnp.tanh(math.sqrt(2.0 / math.pi) * (y + 0.044715 * (y * y * y))))
    t = _dot(g.astype(BF16), w_ref[...])
    o_ref[...] = (t[:, 0:S5_WIDTH] * jax.nn.sigmoid(t[:, S5_WIDTH:2 * S5_WIDTH])).astype(o_ref.dtype)


def _s5_glu(u, yf, yb, dskip, wglu, nseg):
    B, T, _ = u.shape
    tm = min(256, T)
    nt = T // tm
    seg = pl.BlockSpec((None, tm, S5_WIDTH), lambda b, j, i: (b, i, j))
    return pl.pallas_call(
        _glu_kernel,
        grid=(B, nseg, nt),
        in_specs=[seg, seg, seg, _full((1, S5_WIDTH)), _full((S5_WIDTH, 2 * S5_WIDTH))],
        out_specs=pl.BlockSpec((None, tm, S5_WIDTH), lambda b, j, i: (b, j * nt + i, 0)),
        out_shape=jax.ShapeDtypeStruct((B, T * nseg, S5_WIDTH), BF16),
        compiler_params=_cparams(3),
        name="s5_glu",
    )(u, yf, yb, dskip, wglu)


def _merge_kernel(h_ref, ya_ref, yb_ref, yc_ref, x_ref, gate_ref, wmg_ref, bmg_ref, wbr_ref, wout_ref,
                  gpost_ref, o_ref):
    h = h_ref[...]
    d = x_ref.shape[-1]
    acc = None
    for n, y_ref in enumerate((ya_ref, yb_ref, yc_ref)):
        g = jax.nn.sigmoid(_dot(h, wmg_ref[:, n * d:(n + 1) * d]) + bmg_ref[:, n * d:(n + 1) * d])
        t = g * _dot(y_ref[...], wbr_ref[n])
        acc = t if acc is None else acc + t
    y = _dot(acc.astype(BF16), wout_ref[...])
    o_ref[...] = x_ref[...] + gate_ref[...] * _rms(y, gpost_ref[...])


def _merge(h, ya, yb, yc, x, gate, lw):
    B, L, D = x.shape
    tm = min(256, L)
    tok = lambda w: pl.BlockSpec((None, tm, w), lambda b, i: (b, i, 0))
    return pl.pallas_call(
        _merge_kernel,
        grid=(B, L // tm),
        in_specs=[tok(D), tok(512), tok(512), tok(512), tok(D),
                  pl.BlockSpec((None, 1, D), lambda b, i: (b, 0, 0)),
                  _full((D, N_BRANCHES * D)), _full((1, N_BRANCHES * D)), _full((N_BRANCHES, 512, D)),
                  _full((D, D)), _full((1, D))],
        out_specs=tok(D),
        out_shape=jax.ShapeDtypeStruct((B, L, D), F32),
        compiler_params=_cparams(2),
        name="merge",
    )(h, ya, yb, yc, x, gate, lw["w_mg"], lw["b_mg"], lw["w_branch"], lw["w_out"], lw["g_mix_post"])


def _ffn_kernel(x_ref, sh_ref, sc_ref, gate_ref, gpre_ref, gpost_ref, w1_ref, w3_ref, w2_ref, o_ref,
                h_s, acc_s):
    f = pl.program_id(2)

    @pl.when(f == 0)
    def _():
        h = _rms(x_ref[...], gpre_ref[...]) * (1.0 + sc_ref[...]) + sh_ref[...]
        h_s[...] = h.astype(BF16)
        acc_s[...] = jnp.zeros_like(acc_s)

    h = h_s[...]
    a = _dot(h, w1_ref[...])
    g = (a * jax.nn.sigmoid(a)) * _dot(h, w3_ref[...])
    acc_s[...] += _dot(g.astype(BF16), w2_ref[...])

    @pl.when(f == pl.num_programs(2) - 1)
    def _():
        o_ref[...] = x_ref[...] + gate_ref[...] * _rms(acc_s[...], gpost_ref[...])


def _ffn(x, shift, scale, gate, gpre, gpost, w1, w3, w2):
    B, L, D = x.shape
    dff = w1.shape[1]
    tm = min(512, L)
    tf = dff // 2 if (dff // 2) % LANES == 0 else dff
    tok = pl.BlockSpec((None, tm, D), lambda b, i, f: (b, i, 0))
    vec = pl.BlockSpec((None, 1, D), lambda b, i, f: (b, 0, 0))
    return pl.pallas_call(
        _ffn_kernel,
        grid=(B, L // tm, dff // tf),
        in_specs=[tok, vec, vec, vec, _full((1, D)), _full((1, D)),
                  pl.BlockSpec((D, tf), lambda b, i, f: (0, f)),
                  pl.BlockSpec((D, tf), lambda b, i, f: (0, f)),
                  pl.BlockSpec((tf, D), lambda b, i, f: (f, 0))],
        out_specs=tok,
        out_shape=jax.ShapeDtypeStruct((B, L, D), F32),
        scratch_shapes=[pltpu.VMEM((tm, D), BF16), pltpu.VMEM((tm, D), F32)],
        compiler_params=_cparams(3),
        name="ffn",
    )(x, shift, scale, gate, gpre, gpost, w1, w3, w2)


def _moe_kernel(x_ref, sh_ref, sc_ref, gate_ref, gpre_ref, gpost_ref, rw_ref, rb_ref,
                w1_ref, w3_ref, w2_ref, o_ref, h_s, acc_s, gates_s):
    e = pl.program_id(2)
    f = pl.program_id(3)
    lane = lax.broadcasted_iota(jnp.int32, (1, LANES), 1)

    @pl.when((e == 0) & (f == 0))
    def _():
        h = _rms(x_ref[...], gpre_ref[...]) * (1.0 + sc_ref[...]) + sh_ref[...]
        h_s[...] = h.astype(BF16)
        acc_s[...] = jnp.zeros_like(acc_s)
        logits = jnp.dot(h, rw_ref[...], preferred_element_type=F32,
                         precision=lax.Precision.HIGHEST) + rb_ref[...]
        m1 = jnp.max(logits, axis=1, keepdims=True)
        i1 = jnp.min(jnp.where(logits == m1, lane, LANES), axis=1, keepdims=True)
        rest = jnp.where(lane == i1, -jnp.inf, logits)
        m2 = jnp.max(rest, axis=1, keepdims=True)
        i2 = jnp.min(jnp.where(rest == m2, lane, LANES), axis=1, keepdims=True)
        e2 = jnp.exp(m2 - m1)
        den = 1.0 + e2
        gates_s[...] = jnp.where(lane == i1, 1.0 / den, 0.0) + jnp.where(lane == i2, e2 / den, 0.0)

    ge = jnp.sum(jnp.where(lane == e, gates_s[...], 0.0), axis=1, keepdims=True)
    h = h_s[...]
    a = _dot(h, w1_ref[...])
    g = (a * jax.nn.sigmoid(a)) * _dot(h, w3_ref[...])
    acc_s[...] += ge * _dot(g.astype(BF16), w2_ref[...])

    @pl.when((e == pl.num_programs(2) - 1) & (f == pl.num_programs(3) - 1))
    def _():
        o_ref[...] = x_ref[...] + gate_ref[...] * _rms(acc_s[...], gpost_ref[...])


def _moe(x, shift, scale, gate, gpre, gpost, rw, rb, w1, w3, w2):
    B, L, D = x.shape
    ne, _, dff = w1.shape
    tm = min(1024, L)
    tf = dff // 2 if (dff // 2) % LANES == 0 else dff
    tok = pl.BlockSpec((None, tm, D), lambda b, i, e, f: (b, i, 0))
    vec = pl.BlockSpec((None, 1, D), lambda b, i, e, f: (b, 0, 0))
    return pl.pallas_call(
        _moe_kernel,
        grid=(B, L // tm, ne, dff // tf),
        in_specs=[tok, vec, vec, vec, _full((1, D)), _full((1, D)), _full((D, LANES)), _full((1, LANES)),
                  pl.BlockSpec((None, D, tf), lambda b, i, e, f: (e, 0, f)),
                  pl.BlockSpec((None, D, tf), lambda b, i, e, f: (e, 0, f)),
                  pl.BlockSpec((None, tf, D), lambda b, i, e, f: (e, f, 0))],
        out_specs=tok,
        out_shape=jax.ShapeDtypeStruct((B, L, D), F32),
        scratch_shapes=[pltpu.VMEM((tm, D), BF16), pltpu.VMEM((tm, D), F32), pltpu.VMEM((tm, LANES), F32)],
        compiler_params=_cparams(4),
        name="moe",
    )(x, shift, scale, gate, gpre, gpost, rw, rb, w1, w3, w2)


def _rope_tables(n_lat):
    pos = jnp.arange(n_lat, dtype=jnp.int32)
    row = (pos // GRID_W).astype(F32)
    col = (pos % GRID_W).astype(F32)

    def cos_sin(rot_dim):
        axis_dim = rot_dim // 2
        inv_freq = ROPE_THETA ** (-jnp.arange(0, axis_dim, 2, dtype=F32) / axis_dim)
        ang = jnp.concatenate([row[:, None] * inv_freq, col[:, None] * inv_freq], axis=-1)
        return jnp.cos(ang), jnp.sin(ang)

    lane = np.arange(LANES)

    def table(cos, sin, half, start, stop, fill):
        rel = (lane - start) % (2 * half)
        inside = (lane >= start) & (lane < stop)
        idx = rel % half
        c = jnp.where(inside, cos[:, idx], fill)
        s_up = jnp.where(inside & (rel < half), -sin[:, idx], 0.0)
        s_dn = jnp.where(inside & (rel >= half), sin[:, idx], 0.0)
        return jnp.stack([c, s_up, s_dn]).astype(F32)

    gc, gs = cos_sin(GQA_HEAD_DIM)
    mc, ms = cos_sin(MLA_ROPE_DIM)
    t_gqa = table(gc, gs, GQA_HEAD_DIM // 2, 0, LANES, 1.0)
    t_mq = table(mc, ms, MLA_ROPE_DIM // 2, MLA_NOPE_DIM, MLA_NOPE_DIM + MLA_ROPE_DIM, 1.0)
    t_kr = table(mc, ms, MLA_ROPE_DIM // 2, 0, MLA_ROPE_DIM, 1.0)
    return t_gqa, t_mq, t_kr


def _pack_layer(p, l):
    D = p["w_in"].shape[1]
    w = p["w_in"][l]
    offs = np.cumsum([0, 512, MLA_Q_RANK, 128, 128, MLA_KV_RANK, MLA_ROPE_DIM, S5_WIDTH])
    qg, qm, kg, vg, ckv, kr, u = [w[:, offs[i]:offs[i + 1]] for i in range(7)]
    dup = lambda t: jnp.concatenate([t[:, 0:64], t[:, 0:64], t[:, 64:128], t[:, 64:128]], axis=1)
    w_in = jnp.concatenate([qg, qm, dup(kg), dup(vg), ckv, u, kr,
                            jnp.zeros((D, LANES - MLA_ROPE_DIM), F32)], axis=1).astype(BF16)

    wq = p["w_uq"][l].reshape(MLA_Q_RANK, MLA_HEADS, MLA_NOPE_DIM + MLA_ROPE_DIM)
    w_uq = jnp.pad(wq, ((0, 0), (0, 0), (0, HEAD_SLOT - wq.shape[-1]))).reshape(MLA_Q_RANK, -1).astype(BF16)
    wkv = p["w_ukv"][l].reshape(MLA_KV_RANK, MLA_HEADS, MLA_NOPE_DIM + MLA_V_DIM)
    w_uk = jnp.pad(wkv[:, :, :MLA_NOPE_DIM], ((0, 0), (0, 0), (0, HEAD_SLOT - MLA_NOPE_DIM)))
    w_uk = w_uk.reshape(MLA_KV_RANK, -1).astype(BF16)
    w_uv = wkv[:, :, MLA_NOPE_DIM:].reshape(MLA_KV_RANK, -1).astype(BF16)
    e = np.zeros((LANES, MLA_HEADS, HEAD_SLOT), np.float32)
    for r in range(MLA_ROPE_DIM):
        e[r, :, MLA_NOPE_DIM + r] = 1.0
    bd = np.kron(np.eye(512 // GQA_HEAD_DIM, dtype=np.float32), np.ones((GQA_HEAD_DIM, GQA_HEAD_DIM), np.float32))
    return {
        "w_in": w_in,
        "g_q": jnp.tile(p["g_q"][l], GQA_HEADS).reshape(1, -1),
        "g_k": jnp.tile(p["g_k"][l], 2 * GQA_KV_HEADS).reshape(1, -1),
        "g_cq": p["g_cq"][l].reshape(1, -1),
        "g_ckv": p["g_ckv"][l].reshape(1, -1),
        "w_uq": w_uq, "w_uk": w_uk, "w_uv": w_uv,
        "e_kr": jnp.asarray(e.reshape(LANES, -1), BF16),
        "bd64": jnp.asarray(bd, BF16),
        "w_mg": p["w_merge_gate"][l].astype(BF16),
        "b_mg": p["b_merge_gate"][l].reshape(1, -1),
        "w_branch": p["w_branch"][l].astype(BF16),
        "w_out": p["w_out"][l].astype(BF16),
        "g_mix_pre": p["g_mix_pre"][l].reshape(1, -1),
        "g_mix_post": p["g_mix_post"][l].reshape(1, -1),
        "g_ffn_pre": p["g_ffn_pre"][l].reshape(1, -1),
        "g_ffn_post": p["g_ffn_post"][l].reshape(1, -1),
        "w_glu": p["w_glu"][l].astype(BF16),
        "s5_d": p["s5_d"][l].reshape(1, -1),
    }


def _cpow(ar, ai, n):
    rr, ri = None, None
    br, bi = ar, ai
    while n:
        if n & 1:
            rr, ri = (br, bi) if rr is None else (rr * br - ri * bi, rr * bi + ri * br)
        n >>= 1
        if n:
            br, bi = br * br - bi * bi, 2.0 * br * bi
    return rr, ri


def _s5_weights(p, l, seg_lens):
    G, P, H = S5_GROUPS, S5_STATE, S5_GROUP
    eye = jnp.eye(G, dtype=F32)
    out = {"b_re": [], "b_im": [], "c_re": [], "c_im_neg": [], "a": [], "a_t": {}}
    for d in range(2):
        lr = jnp.minimum(p["s5_lambda_re"][l, d], -1e-4)
        li = p["s5_lambda_im"][l, d]
        dt = jnp.exp(p["s5_log_dt"][l, d])[:, None]
        mag = jnp.exp(lr * dt)
        abr, abi = mag * jnp.cos(li * dt), mag * jnp.sin(li * dt)
        den = lr * lr + li * li
        fr = ((abr - 1.0) * lr + abi * li) / den
        fi = (abi * lr - (abr - 1.0) * li) / den
        br, bi = p["s5_b_re"][l, d], p["s5_b_im"][l, d]
        bbr = fr[..., None] * br - fi[..., None] * bi
        bbi = fr[..., None] * bi + fi[..., None] * br
        blk_b = lambda t: jnp.einsum("gph,gk->ghkp", t, eye).reshape(G * H, G * P).astype(BF16)
        blk_c = lambda t: jnp.einsum("ghp,gk->gpkh", t, eye).reshape(G * P, G * H).astype(BF16)
        out["b_re"].append(blk_b(bbr))
        out["b_im"].append(blk_b(bbi))
        out["c_re"].append(blk_c(p["s5_c_re"][l, d]))
        out["c_im_neg"].append(blk_c(-p["s5_c_im"][l, d]))
        out["a"].append(jnp.stack([abr.reshape(1, -1), abi.reshape(1, -1)]))
        for t in seg_lens:
            tr, ti = _cpow(abr, abi, t)
            out["a_t"][(d, t)] = jnp.stack([tr.reshape(1, -1), ti.reshape(1, -1)])
    return out


def _seg_view(t):
    B, T, _ = t.shape
    return t.reshape(B, T, S5_SEGS, S5_WIDTH)


def _to_segments(t):
    B, L, W = t.shape
    return t.reshape(B, S5_SEGS, L // S5_SEGS, W).transpose(0, 2, 1, 3).reshape(B, L // S5_SEGS, S5_SEGS * W)


def kernel(x, c, ctx, c_ctx, w_mod, b_mod, g_mix_pre, g_mix_post, g_ffn_pre, g_ffn_post, w_in, g_q, g_k, g_cq, g_ckv, w_uq, w_ukv, s5_lambda_re, s5_lambda_im, s5_log_dt, s5_b_re, s5_b_im, s5_c_re, s5_c_im, s5_d, w_glu, w_branch, w_merge_gate, b_merge_gate, w_out, ffn_w1, ffn_w3, ffn_w2, router_w, router_b, moe_w1, moe_w3, moe_w2):
    p = dict(w_in=w_in, g_q=g_q, g_k=g_k, g_cq=g_cq, g_ckv=g_ckv, w_uq=w_uq, w_ukv=w_ukv,
             s5_lambda_re=s5_lambda_re, s5_lambda_im=s5_lambda_im, s5_log_dt=s5_log_dt,
             s5_b_re=s5_b_re, s5_b_im=s5_b_im, s5_c_re=s5_c_re, s5_c_im=s5_c_im, s5_d=s5_d, w_glu=w_glu,
             w_branch=w_branch, w_merge_gate=w_merge_gate, b_merge_gate=b_merge_gate, w_out=w_out,
             g_mix_pre=g_mix_pre, g_mix_post=g_mix_post, g_ffn_pre=g_ffn_pre, g_ffn_post=g_ffn_post)
    B, L, D = x.shape
    Lc = ctx.shape[1]
    depth = w_mod.shape[0]
    tabs = _rope_tables(L)
    t_lat, t_ctx = L // S5_SEGS, Lc // S5_SEGS

    cvec = jnp.concatenate([c, c_ctx[None, :], jnp.zeros((8 - B - 1, D), F32)], axis=0)
    lat, cx = x, ctx
    for layer in range(depth):
        last = layer == depth - 1
        lw = _pack_layer(p, layer)
        sw = _s5_weights(p, layer, (t_lat, t_ctx))
        mod = _modulation(cvec, w_mod[layer], b_mod[layer])
        mod_l = [mod[:B, i * D:(i + 1) * D].reshape(B, 1, D) for i in range(6)]
        mod_c = [jnp.broadcast_to(mod[B, i * D:(i + 1) * D], (B, 1, D)) for i in range(6)]

        h_l, qg_l, qm_l, kd_l, vd_l, km_l, vm_l, u_l = _in_proj(
            lat, mod_l[0], mod_l[1], lw["g_mix_pre"], lw, tabs, S5_SEGS)
        h_c, qg_c, qm_c, kd_c, vd_c, km_c, vm_c, u_c = _in_proj(
            cx, mod_c[0], mod_c[1], lw["g_mix_pre"], lw, None, 1)
        cat = lambda a, b: jnp.concatenate([a, b], axis=1)
        ya_l = _attention(qg_l, cat(kd_c, kd_l), cat(vd_c, vd_l), gqa=True)
        yb_l = _attention(qm_l, cat(km_c, km_l), cat(vm_c, vm_l), gqa=False)

        u_cs = _to_segments(u_c)
        zero = jnp.zeros((B, 2, 1, S5_N), F32)
        yf_c, fin_f = _s5_scan(_seg_view(u_cs), sw, zero, rev=False)
        yb_c, fin_b = _s5_scan(_seg_view(u_cs), sw, zero, rev=True)
        yf_l, _ = _s5_scan(_seg_view(u_l), sw, fin_f, rev=False)
        yr_l, _ = _s5_scan(_seg_view(u_l), sw, fin_b, rev=True)
        flat = lambda t: t.reshape(t.shape[0], t.shape[1], S5_SEGS * S5_WIDTH)
        yc_l = _s5_glu(u_l, flat(yf_l), flat(yr_l), lw["s5_d"], lw["w_glu"], S5_SEGS)

        lat = _merge(h_l, ya_l, yb_l, yc_l, lat, mod_l[2], lw)
        if not last:
            ya_c = _attention(qg_c, kd_c, vd_c, gqa=True)
            yb_c2 = _attention(qm_c, km_c, vm_c, gqa=False)
            yc_c = _s5_glu(u_cs, flat(yf_c), flat(yb_c), lw["s5_d"], lw["w_glu"], S5_SEGS)
            cx = _merge(h_c, ya_c, yb_c2, yc_c, cx, mod_c[2], lw)

        i = layer // 2
        if layer % 2 == 0:
            w1, w3, w2 = ffn_w1[i].astype(BF16), ffn_w3[i].astype(BF16), ffn_w2[i].astype(BF16)
            mix = lambda t, m: _ffn(t, m[3], m[4], m[5], lw["g_ffn_pre"], lw["g_ffn_post"], w1, w3, w2)
        else:
            w1, w3, w2 = moe_w1[i].astype(BF16), moe_w3[i].astype(BF16), moe_w2[i].astype(BF16)
            rw = jnp.pad(router_w[i], ((0, 0), (0, LANES - N_EXPERTS)))
            rb = jnp.concatenate([router_b[i], jnp.full((LANES - N_EXPERTS,), -1e30, F32)]).reshape(1, LANES)
            mix = lambda t, m: _moe(t, m[3], m[4], m[5], lw["g_ffn_pre"], lw["g_ffn_post"], rw, rb, w1, w3, w2)
        lat = mix(lat, mod_l)
        if not last:
            cx = mix(cx, mod_c)
    return lat
```

```python
import functools
import math

import jax
import jax.numpy as jnp
import numpy as np
from jax import lax
from jax.experimental import pallas as pl
from jax.experimental.pallas import tpu as pltpu

F32 = jnp.float32
BF16 = jnp.bfloat16

GRID_W = 64
ROPE_THETA = 10000.0
NORM_EPS = 1e-6

GQA_HEADS = 8
GQA_KV_HEADS = 2
GQA_HEAD_DIM = 64
GQA_SCALE = 1.0 / math.sqrt(GQA_HEAD_DIM)

MLA_HEADS = 8
MLA_NOPE_DIM = 64
MLA_ROPE_DIM = 32
MLA_V_DIM = 64
MLA_Q_RANK = 384
MLA_KV_RANK = 256
MLA_SCALE = 1.0 / math.sqrt(MLA_NOPE_DIM + MLA_ROPE_DIM)
LOG2E = math.log2(math.e)

S5_WIDTH = 512
S5_GROUP = 16
S5_GROUPS = S5_WIDTH // S5_GROUP
S5_STATE = 64
S5_N = S5_GROUPS * S5_STATE
S5_SEGS = 8

N_BRANCHES = 3
N_EXPERTS = 8

LANES = 128
HEAD_SLOT = 128

C_QG = 0
C_QM = C_QG + GQA_HEADS * GQA_HEAD_DIM
C_KD = C_QM + MLA_Q_RANK
C_VD = C_KD + 2 * GQA_KV_HEADS * GQA_HEAD_DIM
C_CKV = C_VD + 2 * GQA_KV_HEADS * GQA_HEAD_DIM
C_U = C_CKV + MLA_KV_RANK
C_KR = C_U + S5_WIDTH
N_IN_PACKED = C_KR + LANES

VMEM_LIMIT = 56 * 1024 * 1024


def _cparams(n_axes):
    return pltpu.CompilerParams(dimension_semantics=("arbitrary",) * n_axes,
                                vmem_limit_bytes=VMEM_LIMIT)


def _dot(a, b):
    return jnp.dot(a, b, preferred_element_type=F32)


def _rms(x, g):
    return x * lax.rsqrt(jnp.mean(x * x, axis=-1, keepdims=True) + NORM_EPS) * g


def _full(shape):
    n = len(shape)
    return pl.BlockSpec(shape, lambda *_: (0,) * n)


def _mod_kernel(c_ref, w_ref, b_ref, o_ref):
    c = c_ref[...]
    sc = c * jax.nn.sigmoid(c)
    o_ref[...] = jnp.dot(sc, w_ref[...], preferred_element_type=F32,
                         precision=lax.Precision.HIGHEST) + b_ref[...]


def _modulation(cvec, w, b):
    R, D = cvec.shape
    N = w.shape[1]
    tn = 1536
    return pl.pallas_call(
        _mod_kernel,
        grid=(N // tn,),
        in_specs=[_full((R, D)),
                  pl.BlockSpec((D, tn), lambda j: (0, j)),
                  pl.BlockSpec((1, tn), lambda j: (0, j))],
        out_specs=pl.BlockSpec((R, tn), lambda j: (0, j)),
        out_shape=jax.ShapeDtypeStruct((R, N), F32),
        compiler_params=_cparams(1),
        name="modulation",
    )(cvec, w, b.reshape(1, N))


def _seg_meansq(x, bd_ref, width, seg):
    x2 = x * x
    hi = x2.astype(BF16)
    lo = (x2 - hi.astype(F32)).astype(BF16)
    bd = bd_ref[0:width, 0:width]
    return (_dot(hi, bd) + _dot(lo, bd)) * (1.0 / seg)


def _rope(x, tab_ref, sh1, sh2):
    return (x * tab_ref[0]
            + pltpu.roll(x, sh1, axis=1) * tab_ref[1]
            + pltpu.roll(x, sh2, axis=1) * tab_ref[2])


def _in_kernel(*refs, rope):
    (x_ref, sh_ref, sc_ref, gpre_ref, win_ref, gq_ref, gk_ref, gcq_ref, gckv_ref,
     wuq_ref, wuk_ref, wuv_ref, ekr_ref, bd_ref) = refs[:14]
    if rope:
        rg_ref, rq_ref, rk_ref = refs[14:17]
        outs = refs[17:]
    else:
        outs = refs[14:]
    h_out, qg_out, qm_out, kd_out, vd_out, km_out, vm_out, u_out = outs

    x = x_ref[...]
    h = _rms(x, gpre_ref[...]) * (1.0 + sc_ref[...]) + sh_ref[...]
    hb = h.astype(BF16)
    h_out[...] = hb
    z = _dot(hb, win_ref[...])

    qg = z[:, C_QG:C_QM]
    qg = qg * lax.rsqrt(_seg_meansq(qg, bd_ref, C_QM - C_QG, GQA_HEAD_DIM) + NORM_EPS) * gq_ref[...]
    for c in range((C_QM - C_QG) // LANES):
        blk = qg[:, c * LANES:(c + 1) * LANES]
        if rope:
            blk = _rope(blk, rg_ref, 96, 32)
        qg_out[:, c * LANES:(c + 1) * LANES] = (blk * (GQA_SCALE * LOG2E)).astype(BF16)

    kd = z[:, C_KD:C_VD]
    kd = kd * lax.rsqrt(_seg_meansq(kd, bd_ref, C_VD - C_KD, GQA_HEAD_DIM) + NORM_EPS) * gk_ref[...]
    for c in range((C_VD - C_KD) // LANES):
        blk = kd[:, c * LANES:(c + 1) * LANES]
        if rope:
            blk = _rope(blk, rg_ref, 96, 32)
        kd_out[:, c * LANES:(c + 1) * LANES] = blk.astype(BF16)
    vd_out[...] = z[:, C_VD:C_CKV].astype(BF16)

    qm = _rms(z[:, C_QM:C_KD], gcq_ref[...]).astype(BF16)
    qm = _dot(qm, wuq_ref[...])
    for c in range(MLA_HEADS):
        blk = qm[:, c * HEAD_SLOT:(c + 1) * HEAD_SLOT]
        if rope:
            blk = _rope(blk, rq_ref, 112, 16)
        qm_out[:, c * HEAD_SLOT:(c + 1) * HEAD_SLOT] = (blk * (MLA_SCALE * LOG2E)).astype(BF16)

    ckv = _rms(z[:, C_CKV:C_U], gckv_ref[...]).astype(BF16)
    kr = z[:, C_KR:C_KR + LANES]
    if rope:
        kr = _rope(kr, rk_ref, 112, 16)
    km = _dot(ckv, wuk_ref[...]) + _dot(kr.astype(BF16), ekr_ref[...])
    km_out[...] = km.astype(BF16)
    vm_out[...] = _dot(ckv, wuv_ref[...]).astype(BF16)

    u_out[...] = z[:, C_U:C_KR]


def _in_proj(x, shift, scale, gpre, lw, tabs, nseg):
    B, L, D = x.shape
    tm = min(256, L // nseg)
    tseg = L // nseg
    nt = tseg // tm
    rope = tabs is not None

    def tok(b, j, i):
        return (b, j * nt + i, 0)

    def tokspec(w):
        return pl.BlockSpec((None, tm, w), tok)

    def tabspec():
        return pl.BlockSpec((3, tm, LANES), lambda b, j, i: (0, j * nt + i, 0))

    vec = lambda w: pl.BlockSpec((None, 1, w), lambda b, j, i: (b, 0, 0))
    in_specs = [tokspec(D), vec(D), vec(D), _full((1, D)), _full((D, N_IN_PACKED)),
                _full((1, 512)), _full((1, 256)), _full((1, MLA_Q_RANK)), _full((1, MLA_KV_RANK)),
                _full((MLA_Q_RANK, 1024)), _full((MLA_KV_RANK, 1024)), _full((MLA_KV_RANK, 512)),
                _full((LANES, 1024)), _full((512, 512))]
    args = [x, shift, scale, gpre, lw["w_in"], lw["g_q"], lw["g_k"], lw["g_cq"], lw["g_ckv"],
            lw["w_uq"], lw["w_uk"], lw["w_uv"], lw["e_kr"], lw["bd64"]]
    if rope:
        in_specs += [tabspec(), tabspec(), tabspec()]
        args += list(tabs)
    widths = [D, 512, 1024, 256, 256, 1024, 512]
    out_specs = [tokspec(w) for w in widths]
    out_shape = [jax.ShapeDtypeStruct((B, L, w), BF16) for w in widths]
    out_specs.append(pl.BlockSpec((None, tm, S5_WIDTH), lambda b, j, i: (b, i, j)))
    out_shape.append(jax.ShapeDtypeStruct((B, tseg, nseg * S5_WIDTH), F32))
    return pl.pallas_call(
        functools.partial(_in_kernel, rope=rope),
        grid=(B, nseg, nt),
        in_specs=in_specs,
        out_specs=out_specs,
        out_shape=out_shape,
        compiler_params=_cparams(3),
        name="in_proj_rope" if rope else "in_proj",
    )(*args)


ONES_ROWS = 16
ATTN_KEY_CHUNK = 512


def _attn_kernel(*refs, gqa, chunks, seg_starts):
    nseg = len(seg_starts)
    q_ref = refs[0]
    k_refs = refs[1:1 + nseg]
    v_refs = refs[1 + nseg:1 + 2 * nseg]
    o_ref, vt_ref, s_ref, p_ref = refs[1 + 2 * nseg:]
    hd = LANES // 2

    @pl.when(pl.program_id(2) == 0)
    def _():
        for v_ref, start in zip(v_refs, seg_starts):
            vt = v_ref[...].astype(F32).T
            ones = jnp.ones((ONES_ROWS, vt.shape[1]), BF16)
            for par in range(2):
                vt_ref[par, 0:hd, start:start + vt.shape[1]] = vt[par * hd:(par + 1) * hd, :].astype(BF16)
                vt_ref[par, hd:hd + ONES_ROWS, start:start + vt.shape[1]] = ones

    qt = q_ref[...].astype(F32).T
    qh = []
    for par in range(2):
        if gqa:
            row = lax.broadcasted_iota(jnp.int32, (LANES, 1), 0)
            keep = (row < hd) if par == 0 else (row >= hd)
            qh.append(jnp.where(keep, qt, 0.0).astype(BF16))
        else:
            qh.append(qt[par * HEAD_SLOT:(par + 1) * HEAD_SLOT, :].astype(BF16))

    def scores(par, seg, r0, rows):
        k_ref = k_refs[seg]
        kh = k_ref[r0:r0 + rows, :] if gqa else k_ref[r0:r0 + rows, par * HEAD_SLOT:(par + 1) * HEAD_SLOT]
        st = _dot(kh, qh[par])
        c0 = seg_starts[seg] + r0
        s_ref[par, c0:c0 + rows, :] = st
        return jnp.max(st, axis=0, keepdims=True)

    def probs(par, seg, r0, rows, m):
        c0 = seg_starts[seg] + r0
        p_ref[par, c0:c0 + rows, :] = jnp.exp2(s_ref[par, c0:c0 + rows, :] - m).astype(BF16)

    def weighted(par, seg, r0, rows):
        c0 = seg_starts[seg] + r0
        return _dot(vt_ref[par, :, c0:c0 + rows], p_ref[par, c0:c0 + rows, :])

    def fold(a, b):
        return b if a is None else a + b

    m0 = m1 = acc0 = acc1 = None
    for ch in chunks:
        sm = scores(0, *ch)
        m0 = sm if m0 is None else jnp.maximum(m0, sm)
    for ch in chunks:
        sm = scores(1, *ch)
        m1 = sm if m1 is None else jnp.maximum(m1, sm)
        probs(0, *ch, m0)
    for ch in chunks:
        acc0 = fold(acc0, weighted(0, *ch))
        probs(1, *ch, m1)
    for ch in chunks:
        acc1 = fold(acc1, weighted(1, *ch))
    outs = [a[0:hd, :] / a[hd:hd + 1, :] for a in (acc0, acc1)]
    o_ref[...] = jnp.concatenate(outs, axis=0).T.astype(o_ref.dtype)


def _attention(q, ks, vs, gqa):
    B, Lq, _ = q.shape
    lens = [k.shape[1] for k in ks]
    seg_starts = tuple(int(s) for s in np.cumsum([0] + lens[:-1]))
    Lk = sum(lens)
    tq = min(512, Lq)
    n_pairs = 4
    if gqa:
        q_spec = pl.BlockSpec((None, tq, LANES), lambda b, p, i: (b, i, p))
        k_specs = [pl.BlockSpec((None, n, LANES), lambda b, p, i: (b, 0, p // 2)) for n in lens]
        v_specs = [pl.BlockSpec((None, n, LANES), lambda b, p, i: (b, 0, p // 2)) for n in lens]
    else:
        q_spec = pl.BlockSpec((None, tq, 2 * HEAD_SLOT), lambda b, p, i: (b, i, p))
        k_specs = [pl.BlockSpec((None, n, 2 * HEAD_SLOT), lambda b, p, i: (b, 0, p)) for n in lens]
        v_specs = [pl.BlockSpec((None, n, LANES), lambda b, p, i: (b, 0, p)) for n in lens]
    chunks = tuple((seg, r0, min(ATTN_KEY_CHUNK, n - r0))
                   for seg, n in enumerate(lens) for r0 in range(0, n, ATTN_KEY_CHUNK))
    return pl.pallas_call(
        functools.partial(_attn_kernel, gqa=gqa, chunks=chunks, seg_starts=seg_starts),
        grid=(B, n_pairs, Lq // tq),
        in_specs=[q_spec] + k_specs + v_specs,
        out_specs=pl.BlockSpec((None, tq, LANES), lambda b, p, i: (b, i, p)),
        out_shape=jax.ShapeDtypeStruct((B, Lq, n_pairs * LANES), BF16),
        scratch_shapes=[pltpu.VMEM((2, LANES // 2 + ONES_ROWS, Lk), BF16),
                        pltpu.VMEM((2, Lk, tq), F32), pltpu.VMEM((2, Lk, tq), BF16)],
        compiler_params=_cparams(3),
        name="attn_gqa" if gqa else "attn_mla",
    )(q, *ks, *vs)


S5_COLS = 512
assert S5_COLS // S5_STATE * S5_GROUP == LANES


def _s5_kernel(u_ref, bre_ref, bim_ref, cre_ref, cim_ref, a_ref, at_ref, init_ref,
               y_ref, fin_ref, vre, vim, st, *, rev, ti):
    ps = pl.program_id(1)
    ch = pl.program_id(2)
    n = S5_N

    @pl.when((ps == 0) & (ch == 0))
    def _():
        st[...] = jnp.zeros_like(st)

    @pl.when((ps == 1) & (ch == 0))
    def _():
        ends = [(st[0, j:j + 1, :], st[1, j:j + 1, :]) for j in range(S5_SEGS)]
        atr, ati = at_ref[0], at_ref[1]
        cr, ci = init_ref[0], init_ref[1]
        order = range(S5_SEGS - 1, -1, -1) if rev else range(S5_SEGS)
        for j in order:
            st[0, j:j + 1, :] = cr
            st[1, j:j + 1, :] = ci
            er, ei = ends[j]
            cr, ci = atr * cr - ati * ci + er, atr * ci + ati * cr + ei
        fin_ref[0] = cr
        fin_ref[1] = ci

    ub = u_ref[...].reshape(ti * S5_SEGS, S5_WIDTH).astype(BF16)
    for m in range(n // S5_COLS):
        ch = slice(m * LANES, (m + 1) * LANES)
        cols = slice(m * S5_COLS, (m + 1) * S5_COLS)
        vre[:, :, cols] = _dot(ub[:, ch], bre_ref[ch, cols]).reshape(ti, S5_SEGS, S5_COLS)
        vim[:, :, cols] = _dot(ub[:, ch], bim_ref[ch, cols]).reshape(ti, S5_SEGS, S5_COLS)

    def scan(store):
        for cb in range(n // S5_COLS):
            cols = slice(cb * S5_COLS, (cb + 1) * S5_COLS)
            ar = jnp.broadcast_to(a_ref[0, :, cols], (S5_SEGS, S5_COLS))
            ai = jnp.broadcast_to(a_ref[1, :, cols], (S5_SEGS, S5_COLS))

            def body(i, carry):
                xr, xi = carry
                idx = (ti - 1 - i) if rev else i
                nr = ar * xr - ai * xi + vre[idx, :, cols]
                ni = ar * xi + ai * xr + vim[idx, :, cols]
                if store:
                    vre[idx, :, cols] = nr
                    vim[idx, :, cols] = ni
                return nr, ni

            xr, xi = lax.fori_loop(0, ti, body, (st[0, :, cols], st[1, :, cols]), unroll=4)
            st[0, :, cols] = xr
            st[1, :, cols] = xi

    @pl.when(ps == 0)
    def _():
        scan(False)

    @pl.when(ps == 1)
    def _():
        scan(True)
        xr = vre[...].reshape(ti * S5_SEGS, n).astype(BF16)
        xi = vim[...].reshape(ti * S5_SEGS, n).astype(BF16)
        y = _dot(xr, cre_ref[...]) + _dot(xi, cim_ref[...])
        y_ref[...] = y.reshape(ti, S5_SEGS, S5_WIDTH)


def _s5_scan(u, sw, init, rev):
    B, T, _, _ = u.shape
    ti = min(64, T)
    nch = T // ti
    n = S5_N

    def chunk(c):
        return (nch - 1 - c) if rev else c

    d = 1 if rev else 0
    return pl.pallas_call(
        functools.partial(_s5_kernel, rev=rev, ti=ti),
        grid=(B, 2, nch),
        in_specs=[pl.BlockSpec((None, ti, S5_SEGS, S5_WIDTH), lambda b, p, c: (b, chunk(c), 0, 0)),
                  _full((S5_WIDTH, n)), _full((S5_WIDTH, n)), _full((n, S5_WIDTH)), _full((n, S5_WIDTH)),
                  _full((2, 1, n)), _full((2, 1, n)),
                  pl.BlockSpec((None, 2, 1, n), lambda b, p, c: (b, 0, 0, 0))],
        out_specs=[pl.BlockSpec((None, ti, S5_SEGS, S5_WIDTH),
                                lambda b, p, c: (b, chunk(c * p), 0, 0)),
                   pl.BlockSpec((None, 2, 1, n), lambda b, p, c: (b, 0, 0, 0))],
        out_shape=[jax.ShapeDtypeStruct(u.shape, F32), jax.ShapeDtypeStruct((B, 2, 1, n), F32)],
        scratch_shapes=[pltpu.VMEM((ti, S5_SEGS, n), F32), pltpu.VMEM((ti, S5_SEGS, n), F32),
                        pltpu.VMEM((2, S5_SEGS, n), F32)],
        compiler_params=_cparams(3),
        name="s5_bwd" if rev else "s5_fwd",
    )(u, sw["b_re"][d], sw["b_im"][d], sw["c_re"][d], sw["c_im_neg"][d],
      sw["a"][d], sw["a_t"][(d, T)], init)


def _glu_kernel(u_ref, yf_ref, yb_ref, d_ref, w_ref, o_ref):
    y = d_ref[...] * u_ref[...] + yf_ref[...] + yb_ref[...]
    g = 0.5 * y * (1.0 + jnp.tanh(math.sqrt(2.0 / math.pi) * (y + 0.044715 * (y * y * y))))
    t = _dot(g.astype(BF16), w_ref[...])
    o_ref[...] = (t[:, 0:S5_WIDTH] * jax.nn.sigmoid(t[:, S5_WIDTH:2 * S5_WIDTH])).astype(o_ref.dtype)


def _s5_glu(u, yf, yb, dskip, wglu, nseg):
    B, T, _ = u.shape
    tm = min(256, T)
    nt = T // tm
    seg = pl.BlockSpec((None, tm, S5_WIDTH), lambda b, j, i: (b, i, j))
    return pl.pallas_call(
        _glu_kernel,
        grid=(B, nseg, nt),
        in_specs=[seg, seg, seg, _full((1, S5_WIDTH)), _full((S5_WIDTH, 2 * S5_WIDTH))],
        out_specs=pl.BlockSpec((None, tm, S5_WIDTH), lambda b, j, i: (b, j * nt + i, 0)),
        out_shape=jax.ShapeDtypeStruct((B, T * nseg, S5_WIDTH), BF16),
        compiler_params=_cparams(3),
        name="s5_glu",
    )(u, yf, yb, dskip, wglu)


def _merge_kernel(h_ref, ya_ref, yb_ref, yc_ref, x_ref, gate_ref, wmg_ref, bmg_ref, wbr_ref, wout_ref,
                  gpost_ref, o_ref):
    h = h_ref[...]
    d = x_ref.shape[-1]
    acc = None
    for n, y_ref in enumerate((ya_ref, yb_ref, yc_ref)):
        g = jax.nn.sigmoid(_dot(h, wmg_ref[:, n * d:(n + 1) * d]) + bmg_ref[:, n * d:(n + 1) * d])
        t = g * _dot(y_ref[...], wbr_ref[n])
        acc = t if acc is None else acc + t
    y = _dot(acc.astype(BF16), wout_ref[...])
    o_ref[...] = x_ref[...] + gate_ref[...] * _rms(y, gpost_ref[...])


def _merge(h, ya, yb, yc, x, gate, lw):
    B, L, D = x.shape
    tm = min(256, L)
    tok = lambda w: pl.BlockSpec((None, tm, w), lambda b, i: (b, i, 0))
    return pl.pallas_call(
        _merge_kernel,
        grid=(B, L // tm),
        in_specs=[tok(D), tok(512), tok(512), tok(512), tok(D),
                  pl.BlockSpec((None, 1, D), lambda b, i: (b, 0, 0)),
                  _full((D, N_BRANCHES * D)), _full((1, N_BRANCHES * D)), _full((N_BRANCHES, 512, D)),
                  _full((D, D)), _full((1, D))],
        out_specs=tok(D),
        out_shape=jax.ShapeDtypeStruct((B, L, D), F32),
        compiler_params=_cparams(2),
        name="merge",
    )(h, ya, yb, yc, x, gate, lw["w_mg"], lw["b_mg"], lw["w_branch"], lw["w_out"], lw["g_mix_post"])


def _ffn_kernel(x_ref, sh_ref, sc_ref, gate_ref, gpre_ref, gpost_ref, w1_ref, w3_ref, w2_ref, o_ref,
                h_s, acc_s):
    f = pl.program_id(2)

    @pl.when(f == 0)
    def _():
        h = _rms(x_ref[...], gpre_ref[...]) * (1.0 + sc_ref[...]) + sh_ref[...]
        h_s[...] = h.astype(BF16)
        acc_s[...] = jnp.zeros_like(acc_s)

    h = h_s[...]
    a = _dot(h, w1_ref[...])
    g = (a * jax.nn.sigmoid(a)) * _dot(h, w3_ref[...])
    acc_s[...] += _dot(g.astype(BF16), w2_ref[...])

    @pl.when(f == pl.num_programs(2) - 1)
    def _():
        o_ref[...] = x_ref[...] + gate_ref[...] * _rms(acc_s[...], gpost_ref[...])


def _ffn(x, shift, scale, gate, gpre, gpost, w1, w3, w2):
    B, L, D = x.shape
    dff = w1.shape[1]
    tm = min(512, L)
    tf = dff // 2 if (dff // 2) % LANES == 0 else dff
    tok = pl.BlockSpec((None, tm, D), lambda b, i, f: (b, i, 0))
    vec = pl.BlockSpec((None, 1, D), lambda b, i, f: (b, 0, 0))
    return pl.pallas_call(
        _ffn_kernel,
        grid=(B, L // tm, dff // tf),
        in_specs=[tok, vec, vec, vec, _full((1, D)), _full((1, D)),
                  pl.BlockSpec((D, tf), lambda b, i, f: (0, f)),
                  pl.BlockSpec((D, tf), lambda b, i, f: (0, f)),
                  pl.BlockSpec((tf, D), lambda b, i, f: (f, 0))],
        out_specs=tok,
        out_shape=jax.ShapeDtypeStruct((B, L, D), F32),
        scratch_shapes=[pltpu.VMEM((tm, D), BF16), pltpu.VMEM((tm, D), F32)],
        compiler_params=_cparams(3),
        name="ffn",
    )(x, shift, scale, gate, gpre, gpost, w1, w3, w2)


MOE_BLOCK = 1024
MOE_CHUNK = 128
MOE_TILE = 512
MOE_SORT_ROWS = 512


def _lane_col(a, lane, k):
    return jnp.sum(jnp.where(lane == k, a, 0.0), axis=1, keepdims=True)


def _router_kernel(x_ref, sh_ref, sc_ref, gpre_ref, rw_ref, rb_ref, h_out, route_out, cnt_out):
    lane = lax.broadcasted_iota(jnp.int32, (1, LANES), 1)
    h = _rms(x_ref[...], gpre_ref[...]) * (1.0 + sc_ref[...]) + sh_ref[...]
    h_out[...] = h.astype(BF16)
    logits = jnp.dot(h, rw_ref[...], preferred_element_type=F32,
                     precision=lax.Precision.HIGHEST) + rb_ref[...]
    m1 = jnp.max(logits, axis=1, keepdims=True)
    i1 = jnp.min(jnp.where(logits == m1, lane, LANES), axis=1, keepdims=True)
    rest = jnp.where(lane == i1, -jnp.inf, logits)
    m2 = jnp.max(rest, axis=1, keepdims=True)
    i2 = jnp.min(jnp.where(rest == m2, lane, LANES), axis=1, keepdims=True)
    e2 = jnp.exp(m2 - m1)
    den = 1.0 + e2
    route_out[...] = (jnp.where(lane == 0, i1.astype(F32), 0.0) + jnp.where(lane == 1, i2.astype(F32), 0.0)
                      + jnp.where(lane == 2, 1.0 / den, 0.0) + jnp.where(lane == 3, e2 / den, 0.0))
    member = jnp.where((lane == i1) | (lane == i2), 1.0, 0.0)
    cnt_out[...] = jnp.sum(member, axis=0, keepdims=True)


def _bf16_pieces(w):
    hi = w.astype(BF16).astype(F32)
    r1 = w - hi
    lo = r1.astype(BF16).astype(F32)
    return hi, lo, r1 - lo


def _dispatch_kernel(dest_ref, nv_ref, h_ref, route_ref, ltri_ref, ustr_ref, stage_ref, xs_out, pos_out, xs_s,
                     *, rc):
    del dest_ref, stage_ref
    r = pl.program_id(1)
    d = h_ref.shape[1]

    @pl.when(r == 0)
    def _():
        lane = lax.broadcasted_iota(jnp.int32, (1, LANES), 1)
        lanef = lane.astype(F32)
        route = route_ref[...]
        i1, i2 = _lane_col(route, lane, 0), _lane_col(route, lane, 1)
        g1, g2 = _lane_col(route, lane, 2), _lane_col(route, lane, 3)
        m1, m2 = lanef == i1, lanef == i2
        member = jnp.where(m1 | m2, 1.0, 0.0)
        before = _dot(ltri_ref[...], member.astype(BF16))
        counts = jnp.sum(member, axis=0, keepdims=True)
        padded = jnp.floor((counts + (MOE_CHUNK - 1.0)) * (1.0 / MOE_CHUNK)) * MOE_CHUNK
        seg = _dot(jnp.broadcast_to(padded, (8, LANES)).astype(BF16), ustr_ref[...])[0:1, :]
        slot = before + seg
        pos1 = jnp.sum(jnp.where(m1, slot, 0.0), axis=1, keepdims=True)
        pos2 = jnp.sum(jnp.where(m2, slot, 0.0), axis=1, keepdims=True)
        pos = jnp.where(lane == 0, pos1, 0.0) + jnp.where(lane == 1, pos2, 0.0)
        pos_out[...] = pos
        post = pos.T
        row1, row2 = post[0:1, :], post[1:2, :]

        def gate_cols(g):
            a, b, c = _bf16_pieces(g)
            return (jnp.where(lane == 0, a, 0.0) + jnp.where(lane == 1, b, 0.0)
                    + jnp.where(lane == 2, c, 0.0)).astype(BF16)

        gc1, gc2 = gate_cols(g1), gate_cols(g2)
        h = h_ref[...]
        for c in range(rc * MOE_CHUNK // MOE_SORT_ROWS):
            rows = (lax.broadcasted_iota(jnp.int32, (MOE_SORT_ROWS, 1), 0) + c * MOE_SORT_ROWS).astype(F32)
            q1, q2 = rows == row1, rows == row2
            sl = slice(c * MOE_SORT_ROWS, (c + 1) * MOE_SORT_ROWS)
            xs_s[sl, 0:d] = _dot(jnp.where(q1 | q2, 1.0, 0.0).astype(BF16), h).astype(BF16)
            gates = (_dot(jnp.where(q1, 1.0, 0.0).astype(BF16), gc1)
                     + _dot(jnp.where(q2, 1.0, 0.0).astype(BF16), gc2))
            xs_s[sl, d:d + LANES] = gates.astype(BF16)

    @pl.when(r < nv_ref[pl.program_id(0)])
    def _():
        xs_out[...] = xs_s[pl.ds(pl.multiple_of(r * MOE_CHUNK, MOE_CHUNK), MOE_CHUNK), :]


def _expert_kernel(te_ref, nu_ref, x_ref, w1_ref, w3_ref, w2_ref, o_ref, acc_s):
    del te_ref
    i = pl.program_id(0)
    f = pl.program_id(1)
    d = o_ref.shape[1]

    @pl.when(i < nu_ref[0])
    def _():
        @pl.when(f == 0)
        def _():
            acc_s[...] = jnp.zeros_like(acc_s)

        x = x_ref[:, 0:d]
        a = _dot(x, w1_ref[...])
        g = (a * jax.nn.sigmoid(a)) * _dot(x, w3_ref[...])
        acc_s[...] += _dot(g.astype(BF16), w2_ref[...])

        @pl.when(f == pl.num_programs(1) - 1)
        def _():
            gate = jnp.sum(x_ref[:, d:d + LANES].astype(F32), axis=1, keepdims=True)
            o_ref[...] = (acc_s[...] * gate).astype(o_ref.dtype)

    @pl.when(i >= nu_ref[0])
    def _():
        o_ref[...] = jnp.zeros_like(o_ref)


def _combine_kernel(src_ref, ys_ref, pos_ref, x_ref, gate_ref, gpost_ref, o_ref, ys_s, *, rc):
    del src_ref
    r = pl.program_id(1)
    ys_s[pl.ds(pl.multiple_of(r * MOE_CHUNK, MOE_CHUNK), MOE_CHUNK), :] = ys_ref[...]

    @pl.when(r == rc - 1)
    def _():
        lane = lax.broadcasted_iota(jnp.int32, (1, LANES), 1)
        pos = pos_ref[...]
        pos1, pos2 = _lane_col(pos, lane, 0), _lane_col(pos, lane, 1)
        acc = None
        for c in range(rc * MOE_CHUNK // MOE_SORT_ROWS):
            cols = (lax.broadcasted_iota(jnp.int32, (1, MOE_SORT_ROWS), 1) + c * MOE_SORT_ROWS).astype(F32)
            sel = jnp.where((cols == pos1) | (cols == pos2), 1.0, 0.0).astype(BF16)
            t = _dot(sel, ys_s[c * MOE_SORT_ROWS:(c + 1) * MOE_SORT_ROWS, :])
            acc = t if acc is None else acc + t
        o_ref[...] = x_ref[...] + gate_ref[...] * _rms(acc, gpost_ref[...])


def _moe_tables(counts, rc, nt):
    ne = counts.shape[1]
    per_tile = MOE_TILE // MOE_CHUNK
    cch = (counts + MOE_CHUNK - 1) // MOE_CHUNK
    tiles_e = (jnp.sum(cch, axis=0) + per_tile - 1) // per_tile
    tile_end = jnp.cumsum(tiles_e)
    n_used = tile_end[-1]
    chunk_start = (tile_end - tiles_e)[None, :] * per_tile + (jnp.cumsum(cch, axis=0) - cch)
    seg_end = jnp.cumsum(cch, axis=1)
    r = jnp.arange(rc)
    e_r = jnp.minimum(jnp.sum(r[None, :, None] >= seg_end[:, None, :], axis=-1), ne - 1)
    onehot = (e_r[:, :, None] == jnp.arange(ne)[None, None, :]).astype(jnp.int32)
    pick = lambda t: jnp.sum(onehot * t[:, None, :], axis=-1)
    dest = pick(chunk_start) + r[None, :] - pick(seg_end - cch)
    n_valid = seg_end[:, -1]
    last = jnp.sum(jnp.where(r[None, :] == n_valid[:, None] - 1, dest, 0), axis=1, keepdims=True)
    dest = jnp.where(r[None, :] < n_valid[:, None], dest, last)
    t = jnp.arange(nt)
    tile_e = jnp.sum(jnp.minimum(t, n_used - 1)[:, None] >= tile_end[None, :], axis=-1)
    i32 = lambda a: a.reshape(-1).astype(jnp.int32)
    return i32(dest), i32(n_valid), i32(tile_e), i32(n_used)


def _moe(x, shift, scale, gate, gpre, gpost, rw, rb, w1, w3, w2):
    B, L, D = x.shape
    ne, _, dff = w1.shape
    n = B * L
    t = min(MOE_BLOCK, L)
    nblk, per_batch = n // t, L // t
    rc = (2 * t + ne * MOE_CHUNK) // MOE_CHUNK
    per_tile = MOE_TILE // MOE_CHUNK
    nt = (2 * n // MOE_CHUNK + nblk * ne + per_tile - 1) // per_tile + ne
    tf = dff // 2 if (dff // 2) % LANES == 0 else dff
    nf = dff // tf
    xf = x.reshape(n, D)
    wide = D + LANES

    blk = lambda w: pl.BlockSpec((t, w), lambda i: (i, 0))
    vec1 = pl.BlockSpec((None, 1, D), lambda i: (i // per_batch, 0, 0))
    h, route, cnt = pl.pallas_call(
        _router_kernel,
        grid=(nblk,),
        in_specs=[blk(D), vec1, vec1, _full((1, D)), _full((D, LANES)), _full((1, LANES))],
        out_specs=[blk(D), blk(LANES), pl.BlockSpec((None, 1, LANES), lambda i: (i, 0, 0))],
        out_shape=[jax.ShapeDtypeStruct((n, D), BF16), jax.ShapeDtypeStruct((n, LANES), F32),
                   jax.ShapeDtypeStruct((nblk, 1, LANES), F32)],
        compiler_params=_cparams(1),
        name="moe_router",
    )(xf, shift, scale, gpre, rw, rb)

    counts = cnt[:, 0, :ne].astype(jnp.int32)
    dest, n_valid, tile_e, n_used = _moe_tables(counts, rc, nt)

    ltri = jnp.asarray(np.tril(np.ones((t, t), np.float32), -1), BF16)
    ustr = jnp.asarray(np.triu(np.ones((LANES, LANES), np.float32), 1), BF16)
    stage_rows = nt * MOE_TILE
    xs, pos = pl.pallas_call(
        functools.partial(_dispatch_kernel, rc=rc),
        grid_spec=pltpu.PrefetchScalarGridSpec(
            num_scalar_prefetch=2,
            grid=(nblk, rc),
            in_specs=[pl.BlockSpec((t, D), lambda b, r, dest, nv: (b, 0)),
                      pl.BlockSpec((t, LANES), lambda b, r, dest, nv: (b, 0)),
                      pl.BlockSpec((t, t), lambda b, r, dest, nv: (0, 0)),
                      pl.BlockSpec((LANES, LANES), lambda b, r, dest, nv: (0, 0)),
                      pl.BlockSpec(memory_space=pl.ANY)],
            out_specs=[pl.BlockSpec((MOE_CHUNK, wide), lambda b, r, dest, nv: (dest[b * rc + r], 0)),
                       pl.BlockSpec((t, LANES), lambda b, r, dest, nv: (b, 0))],
            scratch_shapes=[pltpu.VMEM((rc * MOE_CHUNK, wide), BF16)]),
        out_shape=[jax.ShapeDtypeStruct((stage_rows, wide), BF16), jax.ShapeDtypeStruct((n, LANES), F32)],
        input_output_aliases={6: 0},
        compiler_params=_cparams(2),
        name="moe_dispatch",
    )(dest, n_valid, h, route, ltri, ustr, jnp.zeros((stage_rows, wide), BF16))

    def used(i, nu):
        return jnp.minimum(i, nu[0] - 1)

    def fcol(i, f, nu):
        return jnp.where(i < nu[0], f, nf - 1)

    ys = pl.pallas_call(
        _expert_kernel,
        grid_spec=pltpu.PrefetchScalarGridSpec(
            num_scalar_prefetch=2,
            grid=(nt, nf),
            in_specs=[pl.BlockSpec((MOE_TILE, wide), lambda i, f, te, nu: (used(i, nu), 0)),
                      pl.BlockSpec((None, D, tf), lambda i, f, te, nu: (te[i], 0, fcol(i, f, nu))),
                      pl.BlockSpec((None, D, tf), lambda i, f, te, nu: (te[i], 0, fcol(i, f, nu))),
                      pl.BlockSpec((None, tf, D), lambda i, f, te, nu: (te[i], fcol(i, f, nu), 0))],
            out_specs=pl.BlockSpec((MOE_TILE, D), lambda i, f, te, nu: (i, 0)),
            scratch_shapes=[pltpu.VMEM((MOE_TILE, D), F32)]),
        out_shape=jax.ShapeDtypeStruct((stage_rows, D), BF16),
        compiler_params=_cparams(2),
        name="moe_experts",
    )(tile_e, n_used, xs, w1, w3, w2)

    out = pl.pallas_call(
        functools.partial(_combine_kernel, rc=rc),
        grid_spec=pltpu.PrefetchScalarGridSpec(
            num_scalar_prefetch=1,
            grid=(nblk, rc),
            in_specs=[pl.BlockSpec((MOE_CHUNK, D), lambda b, r, src: (src[b * rc + r], 0)),
                      pl.BlockSpec((t, LANES), lambda b, r, src: (b, 0)),
                      pl.BlockSpec((t, D), lambda b, r, src: (b, 0)),
                      pl.BlockSpec((None, 1, D), lambda b, r, src: (b // per_batch, 0, 0)),
                      pl.BlockSpec((1, D), lambda b, r, src: (0, 0))],
            out_specs=pl.BlockSpec((t, D), lambda b, r, src: (b, 0)),
            scratch_shapes=[pltpu.VMEM((rc * MOE_CHUNK, D), BF16)]),
        out_shape=jax.ShapeDtypeStruct((n, D), F32),
        compiler_params=_cparams(2),
        name="moe_combine",
    )(dest, ys, pos, xf, gate, gpost)
    return out.reshape(B, L, D)


def _rope_tables(n_lat):
    pos = jnp.arange(n_lat, dtype=jnp.int32)
    row = (pos // GRID_W).astype(F32)
    col = (pos % GRID_W).astype(F32)

    def cos_sin(rot_dim):
        axis_dim = rot_dim // 2
        inv_freq = ROPE_THETA ** (-jnp.arange(0, axis_dim, 2, dtype=F32) / axis_dim)
        ang = jnp.concatenate([row[:, None] * inv_freq, col[:, None] * inv_freq], axis=-1)
        return jnp.cos(ang), jnp.sin(ang)

    def table(cos, sin, start, stop):
        reps = (stop - start) // (2 * cos.shape[1])
        zero = jnp.zeros_like(sin)

        def lanes(first, second, fill):
            body = jnp.tile(jnp.concatenate([first, second], axis=1), (1, reps))
            return jnp.pad(body, ((0, 0), (start, LANES - stop)), constant_values=fill)

        return jnp.stack([lanes(cos, cos, 1.0), lanes(-sin, zero, 0.0), lanes(zero, sin, 0.0)]).astype(F32)

    gc, gs = cos_sin(GQA_HEAD_DIM)
    mc, ms = cos_sin(MLA_ROPE_DIM)
    t_gqa = table(gc, gs, 0, LANES)
    t_mq = table(mc, ms, MLA_NOPE_DIM, MLA_NOPE_DIM + MLA_ROPE_DIM)
    t_kr = table(mc, ms, 0, MLA_ROPE_DIM)
    return t_gqa, t_mq, t_kr


def _pack_layer(p, l):
    D = p["w_in"].shape[1]
    w = p["w_in"][l]
    offs = np.cumsum([0, 512, MLA_Q_RANK, 128, 128, MLA_KV_RANK, MLA_ROPE_DIM, S5_WIDTH])
    qg, qm, kg, vg, ckv, kr, u = [w[:, offs[i]:offs[i + 1]] for i in range(7)]
    dup = lambda t: jnp.concatenate([t[:, 0:64], t[:, 0:64], t[:, 64:128], t[:, 64:128]], axis=1)
    w_in = jnp.concatenate([qg, qm, dup(kg), dup(vg), ckv, u, kr,
                            jnp.zeros((D, LANES - MLA_ROPE_DIM), F32)], axis=1).astype(BF16)

    wq = p["w_uq"][l].reshape(MLA_Q_RANK, MLA_HEADS, MLA_NOPE_DIM + MLA_ROPE_DIM)
    w_uq = jnp.pad(wq, ((0, 0), (0, 0), (0, HEAD_SLOT - wq.shape[-1]))).reshape(MLA_Q_RANK, -1).astype(BF16)
    wkv = p["w_ukv"][l].reshape(MLA_KV_RANK, MLA_HEADS, MLA_NOPE_DIM + MLA_V_DIM)
    w_uk = jnp.pad(wkv[:, :, :MLA_NOPE_DIM], ((0, 0), (0, 0), (0, HEAD_SLOT - MLA_NOPE_DIM)))
    w_uk = w_uk.reshape(MLA_KV_RANK, -1).astype(BF16)
    w_uv = wkv[:, :, MLA_NOPE_DIM:].reshape(MLA_KV_RANK, -1).astype(BF16)
    e = np.zeros((LANES, MLA_HEADS, HEAD_SLOT), np.float32)
    for r in range(MLA_ROPE_DIM):
        e[r, :, MLA_NOPE_DIM + r] = 1.0
    bd = np.kron(np.eye(512 // GQA_HEAD_DIM, dtype=np.float32), np.ones((GQA_HEAD_DIM, GQA_HEAD_DIM), np.float32))
    return {
        "w_in": w_in,
        "g_q": jnp.tile(p["g_q"][l], GQA_HEADS).reshape(1, -1),
        "g_k": jnp.tile(p["g_k"][l], 2 * GQA_KV_HEADS).reshape(1, -1),
        "g_cq": p["g_cq"][l].reshape(1, -1),
        "g_ckv": p["g_ckv"][l].reshape(1, -1),
        "w_uq": w_uq, "w_uk": w_uk, "w_uv": w_uv,
        "e_kr": jnp.asarray(e.reshape(LANES, -1), BF16),
        "bd64": jnp.asarray(bd, BF16),
        "w_mg": p["w_merge_gate"][l].astype(BF16),
        "b_mg": p["b_merge_gate"][l].reshape(1, -1),
        "w_branch": p["w_branch"][l].astype(BF16),
        "w_out": p["w_out"][l].astype(BF16),
        "g_mix_pre": p["g_mix_pre"][l].reshape(1, -1),
        "g_mix_post": p["g_mix_post"][l].reshape(1, -1),
        "g_ffn_pre": p["g_ffn_pre"][l].reshape(1, -1),
        "g_ffn_post": p["g_ffn_post"][l].reshape(1, -1),
        "w_glu": p["w_glu"][l].astype(BF16),
        "s5_d": p["s5_d"][l].reshape(1, -1),
    }


def _cpow(ar, ai, n):
    rr, ri = None, None
    br, bi = ar, ai
    while n:
        if n & 1:
            rr, ri = (br, bi) if rr is None else (rr * br - ri * bi, rr * bi + ri * br)
        n >>= 1
        if n:
            br, bi = br * br - bi * bi, 2.0 * br * bi
    return rr, ri


def _s5_weights(p, l, seg_lens):
    G, P, H = S5_GROUPS, S5_STATE, S5_GROUP
    eye = jnp.eye(G, dtype=F32)
    out = {"b_re": [], "b_im": [], "c_re": [], "c_im_neg": [], "a": [], "a_t": {}}
    for d in range(2):
        lr = jnp.minimum(p["s5_lambda_re"][l, d], -1e-4)
        li = p["s5_lambda_im"][l, d]
        dt = jnp.exp(p["s5_log_dt"][l, d])[:, None]
        mag = jnp.exp(lr * dt)
        abr, abi = mag * jnp.cos(li * dt), mag * jnp.sin(li * dt)
        den = lr * lr + li * li
        fr = ((abr - 1.0) * lr + abi * li) / den
        fi = (abi * lr - (abr - 1.0) * li) / den
        br, bi = p["s5_b_re"][l, d], p["s5_b_im"][l, d]
        bbr = fr[..., None] * br - fi[..., None] * bi
        bbi = fr[..., None] * bi + fi[..., None] * br
        blk_b = lambda t: jnp.einsum("gph,gk->ghkp", t, eye).reshape(G * H, G * P).astype(BF16)
        blk_c = lambda t: jnp.einsum("ghp,gk->gpkh", t, eye).reshape(G * P, G * H).astype(BF16)
        out["b_re"].append(blk_b(bbr))
        out["b_im"].append(blk_b(bbi))
        out["c_re"].append(blk_c(p["s5_c_re"][l, d]))
        out["c_im_neg"].append(blk_c(-p["s5_c_im"][l, d]))
        out["a"].append(jnp.stack([abr.reshape(1, -1), abi.reshape(1, -1)]))
        for t in seg_lens:
            tr, ti = _cpow(abr, abi, t)
            out["a_t"][(d, t)] = jnp.stack([tr.reshape(1, -1), ti.reshape(1, -1)])
    return out


def _seg_view(t):
    B, T, _ = t.shape
    return t.reshape(B, T, S5_SEGS, S5_WIDTH)


def _to_segments(t):
    B, L, W = t.shape
    return t.reshape(B, S5_SEGS, L // S5_SEGS, W).transpose(0, 2, 1, 3).reshape(B, L // S5_SEGS, S5_SEGS * W)


def kernel(x, c, ctx, c_ctx, w_mod, b_mod, g_mix_pre, g_mix_post, g_ffn_pre, g_ffn_post, w_in, g_q, g_k, g_cq, g_ckv, w_uq, w_ukv, s5_lambda_re, s5_lambda_im, s5_log_dt, s5_b_re, s5_b_im, s5_c_re, s5_c_im, s5_d, w_glu, w_branch, w_merge_gate, b_merge_gate, w_out, ffn_w1, ffn_w3, ffn_w2, router_w, router_b, moe_w1, moe_w3, moe_w2):
    p = dict(w_in=w_in, g_q=g_q, g_k=g_k, g_cq=g_cq, g_ckv=g_ckv, w_uq=w_uq, w_ukv=w_ukv,
             s5_lambda_re=s5_lambda_re, s5_lambda_im=s5_lambda_im, s5_log_dt=s5_log_dt,
             s5_b_re=s5_b_re, s5_b_im=s5_b_im, s5_c_re=s5_c_re, s5_c_im=s5_c_im, s5_d=s5_d, w_glu=w_glu,
             w_branch=w_branch, w_merge_gate=w_merge_gate, b_merge_gate=b_merge_gate, w_out=w_out,
             g_mix_pre=g_mix_pre, g_mix_post=g_mix_post, g_ffn_pre=g_ffn_pre, g_ffn_post=g_ffn_post)
    B, L, D = x.shape
    Lc = ctx.shape[1]
    depth = w_mod.shape[0]
    tabs = _rope_tables(L)
    t_lat, t_ctx = L // S5_SEGS, Lc // S5_SEGS

    cvec = jnp.concatenate([c, c_ctx[None, :], jnp.zeros((8 - B - 1, D), F32)], axis=0)
    lat, cx = x, ctx
    for layer in range(depth):
        last = layer == depth - 1
        lw = _pack_layer(p, layer)
        sw = _s5_weights(p, layer, (t_lat, t_ctx))
        mod = _modulation(cvec, w_mod[layer], b_mod[layer])
        mod_l = [mod[:B, i * D:(i + 1) * D].reshape(B, 1, D) for i in range(6)]
        mod_c = [jnp.broadcast_to(mod[B, i * D:(i + 1) * D], (B, 1, D)) for i in range(6)]

        h_l, qg_l, qm_l, kd_l, vd_l, km_l, vm_l, u_l = _in_proj(
            lat, mod_l[0], mod_l[1], lw["g_mix_pre"], lw, tabs, S5_SEGS)
        h_c, qg_c, qm_c, kd_c, vd_c, km_c, vm_c, u_c = _in_proj(
            cx, mod_c[0], mod_c[1], lw["g_mix_pre"], lw, None, 1)
        ya_l = _attention(qg_l, (kd_c, kd_l), (vd_c, vd_l), gqa=True)
        yb_l = _attention(qm_l, (km_c, km_l), (vm_c, vm_l), gqa=False)

        u_cs = _to_segments(u_c)
        zero = jnp.zeros((B, 2, 1, S5_N), F32)
        yf_c, fin_f = _s5_scan(_seg_view(u_cs), sw, zero, rev=False)
        yb_c, fin_b = _s5_scan(_seg_view(u_cs), sw, zero, rev=True)
        yf_l, _ = _s5_scan(_seg_view(u_l), sw, fin_f, rev=False)
        yr_l, _ = _s5_scan(_seg_view(u_l), sw, fin_b, rev=True)
        flat = lambda t: t.reshape(t.shape[0], t.shape[1], S5_SEGS * S5_WIDTH)
        yc_l = _s5_glu(u_l, flat(yf_l), flat(yr_l), lw["s5_d"], lw["w_glu"], S5_SEGS)

        lat = _merge(h_l, ya_l, yb_l, yc_l, lat, mod_l[2], lw)
        if not last:
            ya_c = _attention(qg_c, (kd_c,), (vd_c,), gqa=True)
            yb_c2 = _attention(qm_c, (km_c,), (vm_c,), gqa=False)
            yc_c = _s5_glu(u_cs, flat(yf_c), flat(yb_c), lw["s5_d"], lw["w_glu"], S5_SEGS)
            cx = _merge(h_c, ya_c, yb_c2, yc_c, cx, mod_c[2], lw)

        i = layer // 2
        if layer % 2 == 0:
            w1, w3, w2 = ffn_w1[i].astype(BF16), ffn_w3[i].astype(BF16), ffn_w2[i].astype(BF16)
            mix = lambda t, m: _ffn(t, m[3], m[4], m[5], lw["g_ffn_pre"], lw["g_ffn_post"], w1, w3, w2)
        else:
            w1, w3, w2 = moe_w1[i].astype(BF16), moe_w3[i].astype(BF16), moe_w2[i].astype(BF16)
            rw = jnp.pad(router_w[i], ((0, 0), (0, LANES - N_EXPERTS)))
            rb = jnp.concatenate([router_b[i], jnp.full((LANES - N_EXPERTS,), -1e30, F32)]).reshape(1, LANES)
            mix = lambda t, m: _moe(t, m[3], m[4], m[5], lw["g_ffn_pre"], lw["g_ffn_post"], rw, rb, w1, w3, w2)
        lat = mix(lat, mod_l)
        if not last:
            cx = mix(cx, mod_c)
    return lat
```

```python
import functools
import math

import jax
import jax.numpy as jnp
import numpy as np
from jax import lax
from jax.experimental import pallas as pl
from jax.experimental.pallas import tpu as pltpu

F32 = jnp.float32
BF16 = jnp.bfloat16

GRID_W = 64
ROPE_THETA = 10000.0
NORM_EPS = 1e-6

GQA_HEADS = 8
GQA_KV_HEADS = 2
GQA_HEAD_DIM = 64
GQA_SCALE = 1.0 / math.sqrt(GQA_HEAD_DIM)

MLA_HEADS = 8
MLA_NOPE_DIM = 64
MLA_ROPE_DIM = 32
MLA_V_DIM = 64
MLA_Q_RANK = 384
MLA_KV_RANK = 256
MLA_SCALE = 1.0 / math.sqrt(MLA_NOPE_DIM + MLA_ROPE_DIM)
LOG2E = math.log2(math.e)

S5_WIDTH = 512
S5_GROUP = 16
S5_GROUPS = S5_WIDTH // S5_GROUP
S5_STATE = 64
S5_N = S5_GROUPS * S5_STATE
S5_SEGS = 8

N_BRANCHES = 3
N_EXPERTS = 8

LANES = 128
HEAD_SLOT = 128

C_QG = 0
C_QM = C_QG + GQA_HEADS * GQA_HEAD_DIM
C_KD = C_QM + MLA_Q_RANK
C_VD = C_KD + 2 * GQA_KV_HEADS * GQA_HEAD_DIM
C_CKV = C_VD + 2 * GQA_KV_HEADS * GQA_HEAD_DIM
C_U = C_CKV + MLA_KV_RANK
C_KR = C_U + S5_WIDTH
N_IN_PACKED = C_KR + LANES

VMEM_LIMIT = 56 * 1024 * 1024


def _cparams(n_axes):
    return pltpu.CompilerParams(dimension_semantics=("arbitrary",) * n_axes,
                                vmem_limit_bytes=VMEM_LIMIT)


def _dot(a, b):
    return jnp.dot(a, b, preferred_element_type=F32)


def _rms(x, g):
    return x * lax.rsqrt(jnp.mean(x * x, axis=-1, keepdims=True) + NORM_EPS) * g


def _full(shape):
    n = len(shape)
    return pl.BlockSpec(shape, lambda *_: (0,) * n)


def _mod_kernel(c_ref, w_ref, b_ref, o_ref):
    c = c_ref[...]
    sc = c * jax.nn.sigmoid(c)
    o_ref[...] = jnp.dot(sc, w_ref[...], preferred_element_type=F32,
                         precision=lax.Precision.HIGHEST) + b_ref[...]


def _modulation(cvec, w, b):
    R, D = cvec.shape
    N = w.shape[1]
    tn = 1536
    return pl.pallas_call(
        _mod_kernel,
        grid=(N // tn,),
        in_specs=[_full((R, D)),
                  pl.BlockSpec((D, tn), lambda j: (0, j)),
                  pl.BlockSpec((1, tn), lambda j: (0, j))],
        out_specs=pl.BlockSpec((R, tn), lambda j: (0, j)),
        out_shape=jax.ShapeDtypeStruct((R, N), F32),
        compiler_params=_cparams(1),
        name="modulation",
    )(cvec, w, b.reshape(1, N))


def _seg_meansq(x, bd_ref, width, seg):
    x2 = x * x
    hi = x2.astype(BF16)
    lo = (x2 - hi.astype(F32)).astype(BF16)
    bd = bd_ref[0:width, 0:width]
    return (_dot(hi, bd) + _dot(lo, bd)) * (1.0 / seg)


def _rope(x, tab_ref, sh1, sh2):
    return (x * tab_ref[0]
            + pltpu.roll(x, sh1, axis=1) * tab_ref[1]
            + pltpu.roll(x, sh2, axis=1) * tab_ref[2])


def _in_kernel(*refs, rope):
    (x_ref, sh_ref, sc_ref, gpre_ref, win_ref, gq_ref, gk_ref, gcq_ref, gckv_ref,
     wuq_ref, wuk_ref, wuv_ref, ekr_ref, bd_ref) = refs[:14]
    if rope:
        rg_ref, rq_ref, rk_ref = refs[14:17]
        outs = refs[17:]
    else:
        outs = refs[14:]
    h_out, qg_out, qm_out, kd_out, vd_out, km_out, vm_out, u_out = outs

    x = x_ref[...]
    h = _rms(x, gpre_ref[...]) * (1.0 + sc_ref[...]) + sh_ref[...]
    hb = h.astype(BF16)
    h_out[...] = hb
    z = _dot(hb, win_ref[...])

    qg = z[:, C_QG:C_QM]
    qg = qg * lax.rsqrt(_seg_meansq(qg, bd_ref, C_QM - C_QG, GQA_HEAD_DIM) + NORM_EPS) * gq_ref[...]
    for c in range((C_QM - C_QG) // LANES):
        blk = qg[:, c * LANES:(c + 1) * LANES]
        if rope:
            blk = _rope(blk, rg_ref, 96, 32)
        qg_out[:, c * LANES:(c + 1) * LANES] = (blk * (GQA_SCALE * LOG2E)).astype(BF16)

    kd = z[:, C_KD:C_VD]
    kd = kd * lax.rsqrt(_seg_meansq(kd, bd_ref, C_VD - C_KD, GQA_HEAD_DIM) + NORM_EPS) * gk_ref[...]
    for c in range((C_VD - C_KD) // LANES):
        blk = kd[:, c * LANES:(c + 1) * LANES]
        if rope:
            blk = _rope(blk, rg_ref, 96, 32)
        kd_out[:, c * LANES:(c + 1) * LANES] = blk.astype(BF16)
    vd_out[...] = z[:, C_VD:C_CKV].astype(BF16)

    qm = _rms(z[:, C_QM:C_KD], gcq_ref[...]).astype(BF16)
    qm = _dot(qm, wuq_ref[...])
    for c in range(MLA_HEADS):
        blk = qm[:, c * HEAD_SLOT:(c + 1) * HEAD_SLOT]
        if rope:
            blk = _rope(blk, rq_ref, 112, 16)
        qm_out[:, c * HEAD_SLOT:(c + 1) * HEAD_SLOT] = (blk * (MLA_SCALE * LOG2E)).astype(BF16)

    ckv = _rms(z[:, C_CKV:C_U], gckv_ref[...]).astype(BF16)
    kr = z[:, C_KR:C_KR + LANES]
    if rope:
        kr = _rope(kr, rk_ref, 112, 16)
    km = _dot(ckv, wuk_ref[...]) + _dot(kr.astype(BF16), ekr_ref[...])
    km_out[...] = km.astype(BF16)
    vm_out[...] = _dot(ckv, wuv_ref[...]).astype(BF16)

    u_out[...] = z[:, C_U:C_KR]


def _in_proj(x, shift, scale, gpre, lw, tabs, nseg):
    B, L, D = x.shape
    tm = min(256, L // nseg)
    tseg = L // nseg
    nt = tseg // tm
    rope = tabs is not None

    def tok(b, j, i):
        return (b, j * nt + i, 0)

    def tokspec(w):
        return pl.BlockSpec((None, tm, w), tok)

    def tabspec():
        return pl.BlockSpec((3, tm, LANES), lambda b, j, i: (0, j * nt + i, 0))

    vec = lambda w: pl.BlockSpec((None, 1, w), lambda b, j, i: (b, 0, 0))
    in_specs = [tokspec(D), vec(D), vec(D), _full((1, D)), _full((D, N_IN_PACKED)),
                _full((1, 512)), _full((1, 256)), _full((1, MLA_Q_RANK)), _full((1, MLA_KV_RANK)),
                _full((MLA_Q_RANK, 1024)), _full((MLA_KV_RANK, 1024)), _full((MLA_KV_RANK, 512)),
                _full((LANES, 1024)), _full((512, 512))]
    args = [x, shift, scale, gpre, lw["w_in"], lw["g_q"], lw["g_k"], lw["g_cq"], lw["g_ckv"],
            lw["w_uq"], lw["w_uk"], lw["w_uv"], lw["e_kr"], lw["bd64"]]
    if rope:
        in_specs += [tabspec(), tabspec(), tabspec()]
        args += list(tabs)
    widths = [D, 512, 1024, 256, 256, 1024, 512]
    out_specs = [tokspec(w) for w in widths]
    out_shape = [jax.ShapeDtypeStruct((B, L, w), BF16) for w in widths]
    out_specs.append(pl.BlockSpec((None, tm, S5_WIDTH), lambda b, j, i: (b, i, j)))
    out_shape.append(jax.ShapeDtypeStruct((B, tseg, nseg * S5_WIDTH), F32))
    return pl.pallas_call(
        functools.partial(_in_kernel, rope=rope),
        grid=(B, nseg, nt),
        in_specs=in_specs,
        out_specs=out_specs,
        out_shape=out_shape,
        compiler_params=_cparams(3),
        name="in_proj_rope" if rope else "in_proj",
    )(*args)


ONES_ROWS = 16
ATTN_KEY_CHUNK = 512
ATTN_MIN_DENOM = 2.0 ** -60


def _attn_kernel(*refs, gqa, chunks, seg_starts):
    nseg = len(seg_starts)
    q_ref = refs[0]
    k_refs = refs[1:1 + nseg]
    v_refs = refs[1 + nseg:1 + 2 * nseg]
    o_ref, vt_ref, kn_ref, s0_ref, s1_ref, p0_ref, p1_ref = refs[1 + 2 * nseg:]
    s_refs, p_refs = (s0_ref, s1_ref), (p0_ref, p1_ref)
    hd = LANES // 2

    def keys(par, seg):
        return k_refs[seg][...] if gqa else k_refs[seg][:, par * HEAD_SLOT:(par + 1) * HEAD_SLOT]

    @pl.when(pl.program_id(2) == 0)
    def _():
        for v_ref, start in zip(v_refs, seg_starts):
            vt = v_ref[...].astype(F32).T
            ones = jnp.ones((ONES_ROWS, vt.shape[1]), BF16)
            for par in range(2):
                vt_ref[par, 0:hd, start:start + vt.shape[1]] = vt[par * hd:(par + 1) * hd, :].astype(BF16)
                vt_ref[par, hd:hd + ONES_ROWS, start:start + vt.shape[1]] = ones
        for par in range(2):
            big = None
            for seg in range(nseg):
                kf = keys(par, seg).astype(F32)
                n2 = jnp.max(jnp.sum(kf * kf, axis=1, keepdims=True), axis=0, keepdims=True)
                big = n2 if big is None else jnp.maximum(big, n2)
            kn_ref[par] = jnp.broadcast_to(jnp.sqrt(big), kn_ref.shape[1:])

    qt = q_ref[...].astype(F32).T
    qh = []
    for par in range(2):
        if gqa:
            row = lax.broadcasted_iota(jnp.int32, (LANES, 1), 0)
            keep = (row < hd) if par == 0 else (row >= hd)
            qh.append(jnp.where(keep, qt, 0.0).astype(BF16))
        else:
            qh.append(qt[par * HEAD_SLOT:(par + 1) * HEAD_SLOT, :].astype(BF16))

    outs, dens = [], []
    for par in range(2):
        qf = qh[par].astype(F32)
        bound = jnp.sqrt(jnp.sum(qf * qf, axis=0, keepdims=True)) * kn_ref[par, 0:1, 0:1]
        acc = None
        for seg, start in enumerate(seg_starts):
            kh = keys(par, seg)
            pt = jnp.exp2(_dot(kh, qh[par]) - bound).astype(BF16)
            t = _dot(vt_ref[par, :, start:start + kh.shape[0]], pt)
            acc = t if acc is None else acc + t
        outs.append(acc[0:hd, :] / acc[hd:hd + 1, :])
        dens.append(acc[hd:hd + 1, :])
    o_ref[...] = jnp.concatenate(outs, axis=0).T.astype(o_ref.dtype)
    underflow = jnp.min(jnp.minimum(dens[0], dens[1])) < ATTN_MIN_DENOM

    @pl.when(underflow)
    def _():
        _attn_exact(o_ref, k_refs, vt_ref, s_refs, p_refs, qh, gqa=gqa, chunks=chunks, seg_starts=seg_starts)


def _attn_exact(o_ref, k_refs, vt_ref, s_refs, p_refs, qh, *, gqa, chunks, seg_starts):
    hd = LANES // 2

    def scores(par, seg, r0, rows):
        k_ref = k_refs[seg]
        kh = k_ref[r0:r0 + rows, :] if gqa else k_ref[r0:r0 + rows, par * HEAD_SLOT:(par + 1) * HEAD_SLOT]
        st = _dot(kh, qh[par])
        c0 = seg_starts[seg] + r0
        s_refs[par][c0:c0 + rows, :] = st
        return jnp.max(st, axis=0, keepdims=True)

    def probs(par, seg, r0, rows, m):
        c0 = seg_starts[seg] + r0
        p_refs[par][c0:c0 + rows, :] = jnp.exp2(s_refs[par][c0:c0 + rows, :] - m).astype(BF16)

    def weighted(par, seg, r0, rows):
        c0 = seg_starts[seg] + r0
        return _dot(vt_ref[par, :, c0:c0 + rows], p_refs[par][c0:c0 + rows, :])

    def fold(a, b):
        return b if a is None else a + b

    m0 = m1 = acc0 = acc1 = None
    for ch in chunks:
        sm = scores(0, *ch)
        m0 = sm if m0 is None else jnp.maximum(m0, sm)
    for ch in chunks:
        sm = scores(1, *ch)
        m1 = sm if m1 is None else jnp.maximum(m1, sm)
        probs(0, *ch, m0)
    for ch in chunks:
        acc0 = fold(acc0, weighted(0, *ch))
        probs(1, *ch, m1)
    for ch in chunks:
        acc1 = fold(acc1, weighted(1, *ch))
    outs = [a[0:hd, :] / a[hd:hd + 1, :] for a in (acc0, acc1)]
    o_ref[...] = jnp.concatenate(outs, axis=0).T.astype(o_ref.dtype)


def _attention(q, ks, vs, gqa):
    B, Lq, _ = q.shape
    lens = [k.shape[1] for k in ks]
    seg_starts = tuple(int(s) for s in np.cumsum([0] + lens[:-1]))
    Lk = sum(lens)
    tq = min(512, Lq)
    n_pairs = 4
    if gqa:
        q_spec = pl.BlockSpec((None, tq, LANES), lambda b, p, i: (b, i, p))
        k_specs = [pl.BlockSpec((None, n, LANES), lambda b, p, i: (b, 0, p // 2)) for n in lens]
        v_specs = [pl.BlockSpec((None, n, LANES), lambda b, p, i: (b, 0, p // 2)) for n in lens]
    else:
        q_spec = pl.BlockSpec((None, tq, 2 * HEAD_SLOT), lambda b, p, i: (b, i, p))
        k_specs = [pl.BlockSpec((None, n, 2 * HEAD_SLOT), lambda b, p, i: (b, 0, p)) for n in lens]
        v_specs = [pl.BlockSpec((None, n, LANES), lambda b, p, i: (b, 0, p)) for n in lens]
    chunks = tuple((seg, r0, min(ATTN_KEY_CHUNK, n - r0))
                   for seg, n in enumerate(lens) for r0 in range(0, n, ATTN_KEY_CHUNK))
    return pl.pallas_call(
        functools.partial(_attn_kernel, gqa=gqa, chunks=chunks, seg_starts=seg_starts),
        grid=(B, n_pairs, Lq // tq),
        in_specs=[q_spec] + k_specs + v_specs,
        out_specs=pl.BlockSpec((None, tq, LANES), lambda b, p, i: (b, i, p)),
        out_shape=jax.ShapeDtypeStruct((B, Lq, n_pairs * LANES), BF16),
        scratch_shapes=[pltpu.VMEM((2, LANES // 2 + ONES_ROWS, Lk), BF16), pltpu.VMEM((2, 8, LANES), F32),
                        pltpu.VMEM((Lk, tq), F32), pltpu.VMEM((Lk, tq), F32),
                        pltpu.VMEM((Lk, tq), BF16), pltpu.VMEM((Lk, tq), BF16)],
        compiler_params=_cparams(3),
        name="attn_gqa" if gqa else "attn_mla",
    )(q, *ks, *vs)


S5_COLS = 512
assert S5_COLS // S5_STATE * S5_GROUP == LANES


def _s5_kernel(u_ref, bre_ref, bim_ref, cre_ref, cim_ref, a_ref, at_ref, init_ref,
               y_ref, fin_ref, vre, vim, st, *, rev, ti):
    ps = pl.program_id(1)
    ch = pl.program_id(2)
    n = S5_N

    @pl.when((ps == 0) & (ch == 0))
    def _():
        st[...] = jnp.zeros_like(st)

    @pl.when((ps == 1) & (ch == 0))
    def _():
        ends = [(st[0, j:j + 1, :], st[1, j:j + 1, :]) for j in range(S5_SEGS)]
        atr, ati = at_ref[0], at_ref[1]
        cr, ci = init_ref[0], init_ref[1]
        order = range(S5_SEGS - 1, -1, -1) if rev else range(S5_SEGS)
        for j in order:
            st[0, j:j + 1, :] = cr
            st[1, j:j + 1, :] = ci
            er, ei = ends[j]
            cr, ci = atr * cr - ati * ci + er, atr * ci + ati * cr + ei
        fin_ref[0] = cr
        fin_ref[1] = ci

    ub = u_ref[...].reshape(ti * S5_SEGS, S5_WIDTH).astype(BF16)
    for m in range(n // S5_COLS):
        ch = slice(m * LANES, (m + 1) * LANES)
        cols = slice(m * S5_COLS, (m + 1) * S5_COLS)
        vre[:, :, cols] = _dot(ub[:, ch], bre_ref[ch, cols]).reshape(ti, S5_SEGS, S5_COLS)
        vim[:, :, cols] = _dot(ub[:, ch], bim_ref[ch, cols]).reshape(ti, S5_SEGS, S5_COLS)

    def scan(store):
        for cb in range(n // S5_COLS):
            cols = slice(cb * S5_COLS, (cb + 1) * S5_COLS)
            ar = jnp.broadcast_to(a_ref[0, :, cols], (S5_SEGS, S5_COLS))
            ai = jnp.broadcast_to(a_ref[1, :, cols], (S5_SEGS, S5_COLS))

            def body(i, carry):
                xr, xi = carry
                idx = (ti - 1 - i) if rev else i
                nr = ar * xr - ai * xi + vre[idx, :, cols]
                ni = ar * xi + ai * xr + vim[idx, :, cols]
                if store:
                    vre[idx, :, cols] = nr
                    vim[idx, :, cols] = ni
                return nr, ni

            xr, xi = lax.fori_loop(0, ti, body, (st[0, :, cols], st[1, :, cols]), unroll=4)
            st[0, :, cols] = xr
            st[1, :, cols] = xi

    @pl.when(ps == 0)
    def _():
        scan(False)

    @pl.when(ps == 1)
    def _():
        scan(True)
        xr = vre[...].reshape(ti * S5_SEGS, n).astype(BF16)
        xi = vim[...].reshape(ti * S5_SEGS, n).astype(BF16)
        y = _dot(xr, cre_ref[...]) + _dot(xi, cim_ref[...])
        y_ref[...] = y.reshape(ti, S5_SEGS, S5_WIDTH)


def _s5_scan(u, sw, init, rev):
    B, T, _, _ = u.shape
    ti = min(64, T)
    nch = T // ti
    n = S5_N

    def chunk(c):
        return (nch - 1 - c) if rev else c

    d = 1 if rev else 0
    return pl.pallas_call(
        functools.partial(_s5_kernel, rev=rev, ti=ti),
        grid=(B, 2, nch),
        in_specs=[pl.BlockSpec((None, ti, S5_SEGS, S5_WIDTH), lambda b, p, c: (b, chunk(c), 0, 0)),
                  _full((S5_WIDTH, n)), _full((S5_WIDTH, n)), _full((n, S5_WIDTH)), _full((n, S5_WIDTH)),
                  _full((2, 1, n)), _full((2, 1, n)),
                  pl.BlockSpec((None, 2, 1, n), lambda b, p, c: (b, 0, 0, 0))],
        out_specs=[pl.BlockSpec((None, ti, S5_SEGS, S5_WIDTH),
                                lambda b, p, c: (b, chunk(c * p), 0, 0)),
                   pl.BlockSpec((None, 2, 1, n), lambda b, p, c: (b, 0, 0, 0))],
        out_shape=[jax.ShapeDtypeStruct(u.shape, F32), jax.ShapeDtypeStruct((B, 2, 1, n), F32)],
        scratch_shapes=[pltpu.VMEM((ti, S5_SEGS, n), F32), pltpu.VMEM((ti, S5_SEGS, n), F32),
                        pltpu.VMEM((2, S5_SEGS, n), F32)],
        compiler_params=_cparams(3),
        name="s5_bwd" if rev else "s5_fwd",
    )(u, sw["b_re"][d], sw["b_im"][d], sw["c_re"][d], sw["c_im_neg"][d],
      sw["a"][d], sw["a_t"][(d, T)], init)


def _glu_kernel(u_ref, yf_ref, yb_ref, d_ref, w_ref, o_ref):
    y = d_ref[...] * u_ref[...] + yf_ref[...] + yb_ref[...]
    g = 0.5 * y * (1.0 + jnp.tanh(math.sqrt(2.0 / math.pi) * (y + 0.044715 * (y * y * y))))
    t = _dot(g.astype(BF16), w_ref[...])
    o_ref[...] = (t[:, 0:S5_WIDTH] * jax.nn.sigmoid(t[:, S5_WIDTH:2 * S5_WIDTH])).astype(o_ref.dtype)


def _s5_glu(u, yf, yb, dskip, wglu, nseg):
    B, T, _ = u.shape
    tm = min(256, T)
    nt = T // tm
    seg = pl.BlockSpec((None, tm, S5_WIDTH), lambda b, j, i: (b, i, j))
    return pl.pallas_call(
        _glu_kernel,
        grid=(B, nseg, nt),
        in_specs=[seg, seg, seg, _full((1, S5_WIDTH)), _full((S5_WIDTH, 2 * S5_WIDTH))],
        out_specs=pl.BlockSpec((None, tm, S5_WIDTH), lambda b, j, i: (b, j * nt + i, 0)),
        out_shape=jax.ShapeDtypeStruct((B, T * nseg, S5_WIDTH), BF16),
        compiler_params=_cparams(3),
        name="s5_glu",
    )(u, yf, yb, dskip, wglu)


def _merge_kernel(h_ref, ya_ref, yb_ref, yc_ref, x_ref, gate_ref, wmg_ref, bmg_ref, wbr_ref, wout_ref,
                  gpost_ref, o_ref):
    h = h_ref[...]
    d = x_ref.shape[-1]
    acc = None
    for n, y_ref in enumerate((ya_ref, yb_ref, yc_ref)):
        g = jax.nn.sigmoid(_dot(h, wmg_ref[:, n * d:(n + 1) * d]) + bmg_ref[:, n * d:(n + 1) * d])
        t = g * _dot(y_ref[...], wbr_ref[n])
        acc = t if acc is None else acc + t
    y = _dot(acc.astype(BF16), wout_ref[...])
    o_ref[...] = x_ref[...] + gate_ref[...] * _rms(y, gpost_ref[...])


def _merge(h, ya, yb, yc, x, gate, lw):
    B, L, D = x.shape
    tm = min(256, L)
    tok = lambda w: pl.BlockSpec((None, tm, w), lambda b, i: (b, i, 0))
    return pl.pallas_call(
        _merge_kernel,
        grid=(B, L // tm),
        in_specs=[tok(D), tok(512), tok(512), tok(512), tok(D),
                  pl.BlockSpec((None, 1, D), lambda b, i: (b, 0, 0)),
                  _full((D, N_BRANCHES * D)), _full((1, N_BRANCHES * D)), _full((N_BRANCHES, 512, D)),
                  _full((D, D)), _full((1, D))],
        out_specs=tok(D),
        out_shape=jax.ShapeDtypeStruct((B, L, D), F32),
        compiler_params=_cparams(2),
        name="merge",
    )(h, ya, yb, yc, x, gate, lw["w_mg"], lw["b_mg"], lw["w_branch"], lw["w_out"], lw["g_mix_post"])


def _ffn_kernel(x_ref, sh_ref, sc_ref, gate_ref, gpre_ref, gpost_ref, w1_ref, w3_ref, w2_ref, o_ref,
                h_s, acc_s):
    f = pl.program_id(2)

    @pl.when(f == 0)
    def _():
        h = _rms(x_ref[...], gpre_ref[...]) * (1.0 + sc_ref[...]) + sh_ref[...]
        h_s[...] = h.astype(BF16)
        acc_s[...] = jnp.zeros_like(acc_s)

    h = h_s[...]
    a = _dot(h, w1_ref[...])
    g = (a * jax.nn.sigmoid(a)) * _dot(h, w3_ref[...])
    acc_s[...] += _dot(g.astype(BF16), w2_ref[...])

    @pl.when(f == pl.num_programs(2) - 1)
    def _():
        o_ref[...] = x_ref[...] + gate_ref[...] * _rms(acc_s[...], gpost_ref[...])


def _ffn(x, shift, scale, gate, gpre, gpost, w1, w3, w2):
    B, L, D = x.shape
    dff = w1.shape[1]
    tm = min(512, L)
    tf = dff // 2 if (dff // 2) % LANES == 0 else dff
    tok = pl.BlockSpec((None, tm, D), lambda b, i, f: (b, i, 0))
    vec = pl.BlockSpec((None, 1, D), lambda b, i, f: (b, 0, 0))
    return pl.pallas_call(
        _ffn_kernel,
        grid=(B, L // tm, dff // tf),
        in_specs=[tok, vec, vec, vec, _full((1, D)), _full((1, D)),
                  pl.BlockSpec((D, tf), lambda b, i, f: (0, f)),
                  pl.BlockSpec((D, tf), lambda b, i, f: (0, f)),
                  pl.BlockSpec((tf, D), lambda b, i, f: (f, 0))],
        out_specs=tok,
        out_shape=jax.ShapeDtypeStruct((B, L, D), F32),
        scratch_shapes=[pltpu.VMEM((tm, D), BF16), pltpu.VMEM((tm, D), F32)],
        compiler_params=_cparams(3),
        name="ffn",
    )(x, shift, scale, gate, gpre, gpost, w1, w3, w2)


MOE_BLOCK = 1024
MOE_CHUNK = 128
MOE_TILE = 512
MOE_SORT_ROWS = 512


def _lane_col(a, lane, k):
    return jnp.sum(jnp.where(lane == k, a, 0.0), axis=1, keepdims=True)


def _router_kernel(x_ref, sh_ref, sc_ref, gpre_ref, rw_ref, rb_ref, h_out, route_out, cnt_out):
    lane = lax.broadcasted_iota(jnp.int32, (1, LANES), 1)
    h = _rms(x_ref[...], gpre_ref[...]) * (1.0 + sc_ref[...]) + sh_ref[...]
    h_out[...] = h.astype(BF16)
    logits = jnp.dot(h, rw_ref[...], preferred_element_type=F32,
                     precision=lax.Precision.HIGHEST) + rb_ref[...]
    m1 = jnp.max(logits, axis=1, keepdims=True)
    i1 = jnp.min(jnp.where(logits == m1, lane, LANES), axis=1, keepdims=True)
    rest = jnp.where(lane == i1, -jnp.inf, logits)
    m2 = jnp.max(rest, axis=1, keepdims=True)
    i2 = jnp.min(jnp.where(rest == m2, lane, LANES), axis=1, keepdims=True)
    e2 = jnp.exp(m2 - m1)
    den = 1.0 + e2
    route_out[...] = (jnp.where(lane == 0, i1.astype(F32), 0.0) + jnp.where(lane == 1, i2.astype(F32), 0.0)
                      + jnp.where(lane == 2, 1.0 / den, 0.0) + jnp.where(lane == 3, e2 / den, 0.0))
    member = jnp.where((lane == i1) | (lane == i2), 1.0, 0.0)
    cnt_out[...] = jnp.sum(member, axis=0, keepdims=True)


def _bf16_pieces(w):
    hi = w.astype(BF16).astype(F32)
    r1 = w - hi
    lo = r1.astype(BF16).astype(F32)
    return hi, lo, r1 - lo


def _dispatch_kernel(dest_ref, nv_ref, h_ref, route_ref, ltri_ref, ustr_ref, stage_ref, xs_out, pos_out, xs_s,
                     *, rc):
    del dest_ref, stage_ref
    r = pl.program_id(1)
    d = h_ref.shape[1]

    @pl.when(r == 0)
    def _():
        lane = lax.broadcasted_iota(jnp.int32, (1, LANES), 1)
        lanef = lane.astype(F32)
        route = route_ref[...]
        i1, i2 = _lane_col(route, lane, 0), _lane_col(route, lane, 1)
        g1, g2 = _lane_col(route, lane, 2), _lane_col(route, lane, 3)
        m1, m2 = lanef == i1, lanef == i2
        member = jnp.where(m1 | m2, 1.0, 0.0)
        before = _dot(ltri_ref[...], member.astype(BF16))
        counts = jnp.sum(member, axis=0, keepdims=True)
        padded = jnp.floor((counts + (MOE_CHUNK - 1.0)) * (1.0 / MOE_CHUNK)) * MOE_CHUNK
        seg = _dot(jnp.broadcast_to(padded, (8, LANES)).astype(BF16), ustr_ref[...])[0:1, :]
        slot = before + seg
        pos1 = jnp.sum(jnp.where(m1, slot, 0.0), axis=1, keepdims=True)
        pos2 = jnp.sum(jnp.where(m2, slot, 0.0), axis=1, keepdims=True)
        pos = jnp.where(lane == 0, pos1, 0.0) + jnp.where(lane == 1, pos2, 0.0)
        pos_out[...] = pos
        post = pos.T
        row1, row2 = post[0:1, :], post[1:2, :]

        def gate_cols(g):
            a, b, c = _bf16_pieces(g)
            return (jnp.where(lane == 0, a, 0.0) + jnp.where(lane == 1, b, 0.0)
                    + jnp.where(lane == 2, c, 0.0)).astype(BF16)

        gc1, gc2 = gate_cols(g1), gate_cols(g2)
        h = h_ref[...]
        for c in range(rc * MOE_CHUNK // MOE_SORT_ROWS):
            rows = (lax.broadcasted_iota(jnp.int32, (MOE_SORT_ROWS, 1), 0) + c * MOE_SORT_ROWS).astype(F32)
            q1, q2 = rows == row1, rows == row2
            sl = slice(c * MOE_SORT_ROWS, (c + 1) * MOE_SORT_ROWS)
            xs_s[sl, 0:d] = _dot(jnp.where(q1 | q2, 1.0, 0.0).astype(BF16), h).astype(BF16)
            gates = (_dot(jnp.where(q1, 1.0, 0.0).astype(BF16), gc1)
                     + _dot(jnp.where(q2, 1.0, 0.0).astype(BF16), gc2))
            xs_s[sl, d:d + LANES] = gates.astype(BF16)

    @pl.when(r < nv_ref[pl.program_id(0)])
    def _():
        xs_out[...] = xs_s[pl.ds(pl.multiple_of(r * MOE_CHUNK, MOE_CHUNK), MOE_CHUNK), :]


def _expert_kernel(te_ref, nu_ref, x_ref, w1_ref, w3_ref, w2_ref, o_ref, acc_s):
    del te_ref
    i = pl.program_id(0)
    f = pl.program_id(1)
    d = o_ref.shape[1]

    @pl.when(i < nu_ref[0])
    def _():
        @pl.when(f == 0)
        def _():
            acc_s[...] = jnp.zeros_like(acc_s)

        x = x_ref[:, 0:d]
        a = _dot(x, w1_ref[...])
        g = (a * jax.nn.sigmoid(a)) * _dot(x, w3_ref[...])
        acc_s[...] += _dot(g.astype(BF16), w2_ref[...])

        @pl.when(f == pl.num_programs(1) - 1)
        def _():
            gate = jnp.sum(x_ref[:, d:d + LANES].astype(F32), axis=1, keepdims=True)
            o_ref[...] = (acc_s[...] * gate).astype(o_ref.dtype)

    @pl.when(i >= nu_ref[0])
    def _():
        o_ref[...] = jnp.zeros_like(o_ref)


def _combine_kernel(src_ref, ys_ref, pos_ref, x_ref, gate_ref, gpost_ref, o_ref, ys_s, *, rc):
    del src_ref
    r = pl.program_id(1)
    ys_s[pl.ds(pl.multiple_of(r * MOE_CHUNK, MOE_CHUNK), MOE_CHUNK), :] = ys_ref[...]

    @pl.when(r == rc - 1)
    def _():
        lane = lax.broadcasted_iota(jnp.int32, (1, LANES), 1)
        pos = pos_ref[...]
        pos1, pos2 = _lane_col(pos, lane, 0), _lane_col(pos, lane, 1)
        acc = None
        for c in range(rc * MOE_CHUNK // MOE_SORT_ROWS):
            cols = (lax.broadcasted_iota(jnp.int32, (1, MOE_SORT_ROWS), 1) + c * MOE_SORT_ROWS).astype(F32)
            sel = jnp.where((cols == pos1) | (cols == pos2), 1.0, 0.0).astype(BF16)
            t = _dot(sel, ys_s[c * MOE_SORT_ROWS:(c + 1) * MOE_SORT_ROWS, :])
            acc = t if acc is None else acc + t
        o_ref[...] = x_ref[...] + gate_ref[...] * _rms(acc, gpost_ref[...])


def _moe_tables(counts, rc, nt):
    ne = counts.shape[1]
    per_tile = MOE_TILE // MOE_CHUNK
    cch = (counts + MOE_CHUNK - 1) // MOE_CHUNK
    tiles_e = (jnp.sum(cch, axis=0) + per_tile - 1) // per_tile
    tile_end = jnp.cumsum(tiles_e)
    n_used = tile_end[-1]
    chunk_start = (tile_end - tiles_e)[None, :] * per_tile + (jnp.cumsum(cch, axis=0) - cch)
    seg_end = jnp.cumsum(cch, axis=1)
    r = jnp.arange(rc)
    e_r = jnp.minimum(jnp.sum(r[None, :, None] >= seg_end[:, None, :], axis=-1), ne - 1)
    onehot = (e_r[:, :, None] == jnp.arange(ne)[None, None, :]).astype(jnp.int32)
    pick = lambda t: jnp.sum(onehot * t[:, None, :], axis=-1)
    dest = pick(chunk_start) + r[None, :] - pick(seg_end - cch)
    n_valid = seg_end[:, -1]
    last = jnp.sum(jnp.where(r[None, :] == n_valid[:, None] - 1, dest, 0), axis=1, keepdims=True)
    dest = jnp.where(r[None, :] < n_valid[:, None], dest, last)
    t = jnp.arange(nt)
    tile_e = jnp.sum(jnp.minimum(t, n_used - 1)[:, None] >= tile_end[None, :], axis=-1)
    i32 = lambda a: a.reshape(-1).astype(jnp.int32)
    return i32(dest), i32(n_valid), i32(tile_e), i32(n_used)


def _moe(x, shift, scale, gate, gpre, gpost, rw, rb, w1, w3, w2):
    B, L, D = x.shape
    ne, _, dff = w1.shape
    n = B * L
    t = min(MOE_BLOCK, L)
    nblk, per_batch = n // t, L // t
    rc = (2 * t + ne * MOE_CHUNK) // MOE_CHUNK
    per_tile = MOE_TILE // MOE_CHUNK
    nt = (2 * n // MOE_CHUNK + nblk * ne + per_tile - 1) // per_tile + ne
    tf = dff // 2 if (dff // 2) % LANES == 0 else dff
    nf = dff // tf
    xf = x.reshape(n, D)
    wide = D + LANES

    blk = lambda w: pl.BlockSpec((t, w), lambda i: (i, 0))
    vec1 = pl.BlockSpec((None, 1, D), lambda i: (i // per_batch, 0, 0))
    h, route, cnt = pl.pallas_call(
        _router_kernel,
        grid=(nblk,),
        in_specs=[blk(D), vec1, vec1, _full((1, D)), _full((D, LANES)), _full((1, LANES))],
        out_specs=[blk(D), blk(LANES), pl.BlockSpec((None, 1, LANES), lambda i: (i, 0, 0))],
        out_shape=[jax.ShapeDtypeStruct((n, D), BF16), jax.ShapeDtypeStruct((n, LANES), F32),
                   jax.ShapeDtypeStruct((nblk, 1, LANES), F32)],
        compiler_params=_cparams(1),
        name="moe_router",
    )(xf, shift, scale, gpre, rw, rb)

    counts = cnt[:, 0, :ne].astype(jnp.int32)
    dest, n_valid, tile_e, n_used = _moe_tables(counts, rc, nt)

    ltri = jnp.asarray(np.tril(np.ones((t, t), np.float32), -1), BF16)
    ustr = jnp.asarray(np.triu(np.ones((LANES, LANES), np.float32), 1), BF16)
    stage_rows = nt * MOE_TILE
    xs, pos = pl.pallas_call(
        functools.partial(_dispatch_kernel, rc=rc),
        grid_spec=pltpu.PrefetchScalarGridSpec(
            num_scalar_prefetch=2,
            grid=(nblk, rc),
            in_specs=[pl.BlockSpec((t, D), lambda b, r, dest, nv: (b, 0)),
                      pl.BlockSpec((t, LANES), lambda b, r, dest, nv: (b, 0)),
                      pl.BlockSpec((t, t), lambda b, r, dest, nv: (0, 0)),
                      pl.BlockSpec((LANES, LANES), lambda b, r, dest, nv: (0, 0)),
                      pl.BlockSpec(memory_space=pl.ANY)],
            out_specs=[pl.BlockSpec((MOE_CHUNK, wide), lambda b, r, dest, nv: (dest[b * rc + r], 0)),
                       pl.BlockSpec((t, LANES), lambda b, r, dest, nv: (b, 0))],
            scratch_shapes=[pltpu.VMEM((rc * MOE_CHUNK, wide), BF16)]),
        out_shape=[jax.ShapeDtypeStruct((stage_rows, wide), BF16), jax.ShapeDtypeStruct((n, LANES), F32)],
        input_output_aliases={6: 0},
        compiler_params=_cparams(2),
        name="moe_dispatch",
    )(dest, n_valid, h, route, ltri, ustr, jnp.zeros((stage_rows, wide), BF16))

    def used(i, nu):
        return jnp.minimum(i, nu[0] - 1)

    def fcol(i, f, nu):
        return jnp.where(i < nu[0], f, nf - 1)

    ys = pl.pallas_call(
        _expert_kernel,
        grid_spec=pltpu.PrefetchScalarGridSpec(
            num_scalar_prefetch=2,
            grid=(nt, nf),
            in_specs=[pl.BlockSpec((MOE_TILE, wide), lambda i, f, te, nu: (used(i, nu), 0)),
                      pl.BlockSpec((None, D, tf), lambda i, f, te, nu: (te[i], 0, fcol(i, f, nu))),
                      pl.BlockSpec((None, D, tf), lambda i, f, te, nu: (te[i], 0, fcol(i, f, nu))),
                      pl.BlockSpec((None, tf, D), lambda i, f, te, nu: (te[i], fcol(i, f, nu), 0))],
            out_specs=pl.BlockSpec((MOE_TILE, D), lambda i, f, te, nu: (i, 0)),
            scratch_shapes=[pltpu.VMEM((MOE_TILE, D), F32)]),
        out_shape=jax.ShapeDtypeStruct((stage_rows, D), BF16),
        compiler_params=_cparams(2),
        name="moe_experts",
    )(tile_e, n_used, xs, w1, w3, w2)

    out = pl.pallas_call(
        functools.partial(_combine_kernel, rc=rc),
        grid_spec=pltpu.PrefetchScalarGridSpec(
            num_scalar_prefetch=1,
            grid=(nblk, rc),
            in_specs=[pl.BlockSpec((MOE_CHUNK, D), lambda b, r, src: (src[b * rc + r], 0)),
                      pl.BlockSpec((t, LANES), lambda b, r, src: (b, 0)),
                      pl.BlockSpec((t, D), lambda b, r, src: (b, 0)),
                      pl.BlockSpec((None, 1, D), lambda b, r, src: (b // per_batch, 0, 0)),
                      pl.BlockSpec((1, D), lambda b, r, src: (0, 0))],
            out_specs=pl.BlockSpec((t, D), lambda b, r, src: (b, 0)),
            scratch_shapes=[pltpu.VMEM((rc * MOE_CHUNK, D), BF16)]),
        out_shape=jax.ShapeDtypeStruct((n, D), F32),
        compiler_params=_cparams(2),
        name="moe_combine",
    )(dest, ys, pos, xf, gate, gpost)
    return out.reshape(B, L, D)


def _rope_tables(n_lat):
    pos = jnp.arange(n_lat, dtype=jnp.int32)
    row = (pos // GRID_W).astype(F32)
    col = (pos % GRID_W).astype(F32)

    def cos_sin(rot_dim):
        axis_dim = rot_dim // 2
        inv_freq = ROPE_THETA ** (-jnp.arange(0, axis_dim, 2, dtype=F32) / axis_dim)
        ang = jnp.concatenate([row[:, None] * inv_freq, col[:, None] * inv_freq], axis=-1)
        return jnp.cos(ang), jnp.sin(ang)

    def table(cos, sin, start, stop):
        reps = (stop - start) // (2 * cos.shape[1])
        zero = jnp.zeros_like(sin)

        def lanes(first, second, fill):
            body = jnp.tile(jnp.concatenate([first, second], axis=1), (1, reps))
            return jnp.pad(body, ((0, 0), (start, LANES - stop)), constant_values=fill)

        return jnp.stack([lanes(cos, cos, 1.0), lanes(-sin, zero, 0.0), lanes(zero, sin, 0.0)]).astype(F32)

    gc, gs = cos_sin(GQA_HEAD_DIM)
    mc, ms = cos_sin(MLA_ROPE_DIM)
    t_gqa = table(gc, gs, 0, LANES)
    t_mq = table(mc, ms, MLA_NOPE_DIM, MLA_NOPE_DIM + MLA_ROPE_DIM)
    t_kr = table(mc, ms, 0, MLA_ROPE_DIM)
    return t_gqa, t_mq, t_kr


def _pack_layer(p, l):
    D = p["w_in"].shape[1]
    w = p["w_in"][l]
    offs = np.cumsum([0, 512, MLA_Q_RANK, 128, 128, MLA_KV_RANK, MLA_ROPE_DIM, S5_WIDTH])
    qg, qm, kg, vg, ckv, kr, u = [w[:, offs[i]:offs[i + 1]] for i in range(7)]
    dup = lambda t: jnp.concatenate([t[:, 0:64], t[:, 0:64], t[:, 64:128], t[:, 64:128]], axis=1)
    w_in = jnp.concatenate([qg, qm, dup(kg), dup(vg), ckv, u, kr,
                            jnp.zeros((D, LANES - MLA_ROPE_DIM), F32)], axis=1).astype(BF16)

    wq = p["w_uq"][l].reshape(MLA_Q_RANK, MLA_HEADS, MLA_NOPE_DIM + MLA_ROPE_DIM)
    w_uq = jnp.pad(wq, ((0, 0), (0, 0), (0, HEAD_SLOT - wq.shape[-1]))).reshape(MLA_Q_RANK, -1).astype(BF16)
    wkv = p["w_ukv"][l].reshape(MLA_KV_RANK, MLA_HEADS, MLA_NOPE_DIM + MLA_V_DIM)
    w_uk = jnp.pad(wkv[:, :, :MLA_NOPE_DIM], ((0, 0), (0, 0), (0, HEAD_SLOT - MLA_NOPE_DIM)))
    w_uk = w_uk.reshape(MLA_KV_RANK, -1).astype(BF16)
    w_uv = wkv[:, :, MLA_NOPE_DIM:].reshape(MLA_KV_RANK, -1).astype(BF16)
    e = np.zeros((LANES, MLA_HEADS, HEAD_SLOT), np.float32)
    for r in range(MLA_ROPE_DIM):
        e[r, :, MLA_NOPE_DIM + r] = 1.0
    bd = np.kron(np.eye(512 // GQA_HEAD_DIM, dtype=np.float32), np.ones((GQA_HEAD_DIM, GQA_HEAD_DIM), np.float32))
    return {
        "w_in": w_in,
        "g_q": jnp.tile(p["g_q"][l], GQA_HEADS).reshape(1, -1),
        "g_k": jnp.tile(p["g_k"][l], 2 * GQA_KV_HEADS).reshape(1, -1),
        "g_cq": p["g_cq"][l].reshape(1, -1),
        "g_ckv": p["g_ckv"][l].reshape(1, -1),
        "w_uq": w_uq, "w_uk": w_uk, "w_uv": w_uv,
        "e_kr": jnp.asarray(e.reshape(LANES, -1), BF16),
        "bd64": jnp.asarray(bd, BF16),
        "w_mg": p["w_merge_gate"][l].astype(BF16),
        "b_mg": p["b_merge_gate"][l].reshape(1, -1),
        "w_branch": p["w_branch"][l].astype(BF16),
        "w_out": p["w_out"][l].astype(BF16),
        "g_mix_pre": p["g_mix_pre"][l].reshape(1, -1),
        "g_mix_post": p["g_mix_post"][l].reshape(1, -1),
        "g_ffn_pre": p["g_ffn_pre"][l].reshape(1, -1),
        "g_ffn_post": p["g_ffn_post"][l].reshape(1, -1),
        "w_glu": p["w_glu"][l].astype(BF16),
        "s5_d": p["s5_d"][l].reshape(1, -1),
    }


def _cpow(ar, ai, n):
    rr, ri = None, None
    br, bi = ar, ai
    while n:
        if n & 1:
            rr, ri = (br, bi) if rr is None else (rr * br - ri * bi, rr * bi + ri * br)
        n >>= 1
        if n:
            br, bi = br * br - bi * bi, 2.0 * br * bi
    return rr, ri


def _s5_weights(p, l, seg_lens):
    G, P, H = S5_GROUPS, S5_STATE, S5_GROUP
    eye = jnp.eye(G, dtype=F32)
    out = {"b_re": [], "b_im": [], "c_re": [], "c_im_neg": [], "a": [], "a_t": {}}
    for d in range(2):
        lr = jnp.minimum(p["s5_lambda_re"][l, d], -1e-4)
        li = p["s5_lambda_im"][l, d]
        dt = jnp.exp(p["s5_log_dt"][l, d])[:, None]
        mag = jnp.exp(lr * dt)
        abr, abi = mag * jnp.cos(li * dt), mag * jnp.sin(li * dt)
        den = lr * lr + li * li
        fr = ((abr - 1.0) * lr + abi * li) / den
        fi = (abi * lr - (abr - 1.0) * li) / den
        br, bi = p["s5_b_re"][l, d], p["s5_b_im"][l, d]
        bbr = fr[..., None] * br - fi[..., None] * bi
        bbi = fr[..., None] * bi + fi[..., None] * br
        blk_b = lambda t: jnp.einsum("gph,gk->ghkp", t, eye).reshape(G * H, G * P).astype(BF16)
        blk_c = lambda t: jnp.einsum("ghp,gk->gpkh", t, eye).reshape(G * P, G * H).astype(BF16)
        out["b_re"].append(blk_b(bbr))
        out["b_im"].append(blk_b(bbi))
        out["c_re"].append(blk_c(p["s5_c_re"][l, d]))
        out["c_im_neg"].append(blk_c(-p["s5_c_im"][l, d]))
        out["a"].append(jnp.stack([abr.reshape(1, -1), abi.reshape(1, -1)]))
        for t in seg_lens:
            tr, ti = _cpow(abr, abi, t)
            out["a_t"][(d, t)] = jnp.stack([tr.reshape(1, -1), ti.reshape(1, -1)])
    return out


def _seg_view(t):
    B, T, _ = t.shape
    return t.reshape(B, T, S5_SEGS, S5_WIDTH)


def _to_segments(t):
    B, L, W = t.shape
    return t.reshape(B, S5_SEGS, L // S5_SEGS, W).transpose(0, 2, 1, 3).reshape(B, L // S5_SEGS, S5_SEGS * W)


def kernel(x, c, ctx, c_ctx, w_mod, b_mod, g_mix_pre, g_mix_post, g_ffn_pre, g_ffn_post, w_in, g_q, g_k, g_cq, g_ckv, w_uq, w_ukv, s5_lambda_re, s5_lambda_im, s5_log_dt, s5_b_re, s5_b_im, s5_c_re, s5_c_im, s5_d, w_glu, w_branch, w_merge_gate, b_merge_gate, w_out, ffn_w1, ffn_w3, ffn_w2, router_w, router_b, moe_w1, moe_w3, moe_w2):
    p = dict(w_in=w_in, g_q=g_q, g_k=g_k, g_cq=g_cq, g_ckv=g_ckv, w_uq=w_uq, w_ukv=w_ukv,
             s5_lambda_re=s5_lambda_re, s5_lambda_im=s5_lambda_im, s5_log_dt=s5_log_dt,
             s5_b_re=s5_b_re, s5_b_im=s5_b_im, s5_c_re=s5_c_re, s5_c_im=s5_c_im, s5_d=s5_d, w_glu=w_glu,
             w_branch=w_branch, w_merge_gate=w_merge_gate, b_merge_gate=b_merge_gate, w_out=w_out,
             g_mix_pre=g_mix_pre, g_mix_post=g_mix_post, g_ffn_pre=g_ffn_pre, g_ffn_post=g_ffn_post)
    B, L, D = x.shape
    Lc = ctx.shape[1]
    depth = w_mod.shape[0]
    tabs = _rope_tables(L)
    t_lat, t_ctx = L // S5_SEGS, Lc // S5_SEGS

    cvec = jnp.concatenate([c, c_ctx[None, :], jnp.zeros((8 - B - 1, D), F32)], axis=0)
    lat, cx = x, ctx
    for layer in range(depth):
        last = layer == depth - 1
        lw = _pack_layer(p, layer)
        sw = _s5_weights(p, layer, (t_lat, t_ctx))
        mod = _modulation(cvec, w_mod[layer], b_mod[layer])
        mod_l = [mod[:B, i * D:(i + 1) * D].reshape(B, 1, D) for i in range(6)]
        mod_c = [jnp.broadcast_to(mod[B, i * D:(i + 1) * D], (B, 1, D)) for i in range(6)]

        h_l, qg_l, qm_l, kd_l, vd_l, km_l, vm_l, u_l = _in_proj(
            lat, mod_l[0], mod_l[1], lw["g_mix_pre"], lw, tabs, S5_SEGS)
        h_c, qg_c, qm_c, kd_c, vd_c, km_c, vm_c, u_c = _in_proj(
            cx, mod_c[0], mod_c[1], lw["g_mix_pre"], lw, None, 1)
        ya_l = _attention(qg_l, (kd_c, kd_l), (vd_c, vd_l), gqa=True)
        yb_l = _attention(qm_l, (km_c, km_l), (vm_c, vm_l), gqa=False)

        u_cs = _to_segments(u_c)
        zero = jnp.zeros((B, 2, 1, S5_N), F32)
        yf_c, fin_f = _s5_scan(_seg_view(u_cs), sw, zero, rev=False)
        yb_c, fin_b = _s5_scan(_seg_view(u_cs), sw, zero, rev=True)
        yf_l, _ = _s5_scan(_seg_view(u_l), sw, fin_f, rev=False)
        yr_l, _ = _s5_scan(_seg_view(u_l), sw, fin_b, rev=True)
        flat = lambda t: t.reshape(t.shape[0], t.shape[1], S5_SEGS * S5_WIDTH)
        yc_l = _s5_glu(u_l, flat(yf_l), flat(yr_l), lw["s5_d"], lw["w_glu"], S5_SEGS)

        lat = _merge(h_l, ya_l, yb_l, yc_l, lat, mod_l[2], lw)
        if not last:
            ya_c = _attention(qg_c, (kd_c,), (vd_c,), gqa=True)
            yb_c2 = _attention(qm_c, (km_c,), (vm_c,), gqa=False)
            yc_c = _s5_glu(u_cs, flat(yf_c), flat(yb_c), lw["s5_d"], lw["w_glu"], S5_SEGS)
            cx = _merge(h_c, ya_c, yb_c2, yc_c, cx, mod_c[2], lw)

        i = layer // 2
        if layer % 2 == 0:
            w1, w3, w2 = ffn_w1[i].astype(BF16), ffn_w3[i].astype(BF16), ffn_w2[i].astype(BF16)
            mix = lambda t, m: _ffn(t, m[3], m[4], m[5], lw["g_ffn_pre"], lw["g_ffn_post"], w1, w3, w2)
        else:
            w1, w3, w2 = moe_w1[i].astype(BF16), moe_w3[i].astype(BF16), moe_w2[i].astype(BF16)
            rw = jnp.pad(router_w[i], ((0, 0), (0, LANES - N_EXPERTS)))
            rb = jnp.concatenate([router_b[i], jnp.full((LANES - N_EXPERTS,), -1e30, F32)]).reshape(1, LANES)
            mix = lambda t, m: _moe(t, m[3], m[4], m[5], lw["g_ffn_pre"], lw["g_ffn_post"], rw, rb, w1, w3, w2)
        lat = mix(lat, mod_l)
        if not last:
            cx = mix(cx, mod_c)
    return lat
```

```python
import functools
import math

import jax
import jax.numpy as jnp
import numpy as np
from jax import lax
from jax.experimental import pallas as pl
from jax.experimental.pallas import tpu as pltpu

F32 = jnp.float32
BF16 = jnp.bfloat16

GRID_W = 64
ROPE_THETA = 10000.0
NORM_EPS = 1e-6

GQA_HEADS = 8
GQA_KV_HEADS = 2
GQA_HEAD_DIM = 64
GQA_SCALE = 1.0 / math.sqrt(GQA_HEAD_DIM)

MLA_HEADS = 8
MLA_NOPE_DIM = 64
MLA_ROPE_DIM = 32
MLA_V_DIM = 64
MLA_Q_RANK = 384
MLA_KV_RANK = 256
MLA_SCALE = 1.0 / math.sqrt(MLA_NOPE_DIM + MLA_ROPE_DIM)
LOG2E = math.log2(math.e)

S5_WIDTH = 512
S5_GROUP = 16
S5_GROUPS = S5_WIDTH // S5_GROUP
S5_STATE = 64
S5_N = S5_GROUPS * S5_STATE
S5_SEGS = 8

N_BRANCHES = 3
N_EXPERTS = 8

LANES = 128
HEAD_SLOT = 128

C_QG = 0
C_QM = C_QG + GQA_HEADS * GQA_HEAD_DIM
C_KD = C_QM + MLA_Q_RANK
C_VD = C_KD + 2 * GQA_KV_HEADS * GQA_HEAD_DIM
C_CKV = C_VD + 2 * GQA_KV_HEADS * GQA_HEAD_DIM
C_U = C_CKV + MLA_KV_RANK
C_KR = C_U + S5_WIDTH
N_IN_PACKED = C_KR + LANES

VMEM_LIMIT = 56 * 1024 * 1024


def _cparams(n_axes):
    return pltpu.CompilerParams(dimension_semantics=("arbitrary",) * n_axes,
                                vmem_limit_bytes=VMEM_LIMIT)


def _dot(a, b):
    return jnp.dot(a, b, preferred_element_type=F32)


def _rms(x, g):
    return x * lax.rsqrt(jnp.mean(x * x, axis=-1, keepdims=True) + NORM_EPS) * g


def _full(shape):
    n = len(shape)
    return pl.BlockSpec(shape, lambda *_: (0,) * n)


def _mod_kernel(c_ref, w_ref, b_ref, o_ref):
    c = c_ref[...]
    sc = c * jax.nn.sigmoid(c)
    o_ref[...] = jnp.dot(sc, w_ref[...], preferred_element_type=F32,
                         precision=lax.Precision.HIGHEST) + b_ref[...]


def _modulation(cvec, w, b):
    R, D = cvec.shape
    N = w.shape[1]
    tn = 1536
    return pl.pallas_call(
        _mod_kernel,
        grid=(N // tn,),
        in_specs=[_full((R, D)),
                  pl.BlockSpec((D, tn), lambda j: (0, j)),
                  pl.BlockSpec((1, tn), lambda j: (0, j))],
        out_specs=pl.BlockSpec((R, tn), lambda j: (0, j)),
        out_shape=jax.ShapeDtypeStruct((R, N), F32),
        compiler_params=_cparams(1),
        name="modulation",
    )(cvec, w, b.reshape(1, N))


def _seg_meansq(x, bd_ref, width, seg):
    x2 = x * x
    hi = x2.astype(BF16)
    lo = (x2 - hi.astype(F32)).astype(BF16)
    bd = bd_ref[0:width, 0:width]
    return (_dot(hi, bd) + _dot(lo, bd)) * (1.0 / seg)


def _rope(x, tab_ref, sh1, sh2):
    return (x * tab_ref[0]
            + pltpu.roll(x, sh1, axis=1) * tab_ref[1]
            + pltpu.roll(x, sh2, axis=1) * tab_ref[2])


def _in_kernel(*refs, rope):
    (x_ref, sh_ref, sc_ref, gpre_ref, win_ref, gq_ref, gk_ref, gcq_ref, gckv_ref,
     wuq_ref, wuk_ref, wuv_ref, ekr_ref, bd_ref) = refs[:14]
    if rope:
        rg_ref, rq_ref, rk_ref = refs[14:17]
        outs = refs[17:]
    else:
        outs = refs[14:]
    h_out, qg_out, qm_out, kd_out, vd_out, km_out, vm_out, u_out = outs

    x = x_ref[...]
    h = _rms(x, gpre_ref[...]) * (1.0 + sc_ref[...]) + sh_ref[...]
    hb = h.astype(BF16)
    h_out[...] = hb
    z = _dot(hb, win_ref[...])

    qg = z[:, C_QG:C_QM]
    qg = qg * lax.rsqrt(_seg_meansq(qg, bd_ref, C_QM - C_QG, GQA_HEAD_DIM) + NORM_EPS) * gq_ref[...]
    for c in range((C_QM - C_QG) // LANES):
        blk = qg[:, c * LANES:(c + 1) * LANES]
        if rope:
            blk = _rope(blk, rg_ref, 96, 32)
        qg_out[:, c * LANES:(c + 1) * LANES] = (blk * (GQA_SCALE * LOG2E)).astype(BF16)

    kd = z[:, C_KD:C_VD]
    kd = kd * lax.rsqrt(_seg_meansq(kd, bd_ref, C_VD - C_KD, GQA_HEAD_DIM) + NORM_EPS) * gk_ref[...]
    for c in range((C_VD - C_KD) // LANES):
        blk = kd[:, c * LANES:(c + 1) * LANES]
        if rope:
            blk = _rope(blk, rg_ref, 96, 32)
        kd_out[:, c * LANES:(c + 1) * LANES] = blk.astype(BF16)
    vd_out[...] = z[:, C_VD:C_CKV].astype(BF16)

    qm = _rms(z[:, C_QM:C_KD], gcq_ref[...]).astype(BF16)
    qm = _dot(qm, wuq_ref[...])
    for c in range(MLA_HEADS):
        blk = qm[:, c * HEAD_SLOT:(c + 1) * HEAD_SLOT]
        if rope:
            blk = _rope(blk, rq_ref, 112, 16)
        qm_out[:, c * HEAD_SLOT:(c + 1) * HEAD_SLOT] = (blk * (MLA_SCALE * LOG2E)).astype(BF16)

    ckv = _rms(z[:, C_CKV:C_U], gckv_ref[...]).astype(BF16)
    kr = z[:, C_KR:C_KR + LANES]
    if rope:
        kr = _rope(kr, rk_ref, 112, 16)
    km = _dot(ckv, wuk_ref[...]) + _dot(kr.astype(BF16), ekr_ref[...])
    km_out[...] = km.astype(BF16)
    vm_out[...] = _dot(ckv, wuv_ref[...]).astype(BF16)

    u_out[...] = z[:, C_U:C_KR]


def _in_proj(x, shift, scale, gpre, lw, tabs, nseg):
    B, L, D = x.shape
    tm = min(256, L // nseg)
    tseg = L // nseg
    nt = tseg // tm
    rope = tabs is not None

    def tok(b, j, i):
        return (b, j * nt + i, 0)

    def tokspec(w):
        return pl.BlockSpec((None, tm, w), tok)

    def tabspec():
        return pl.BlockSpec((3, tm, LANES), lambda b, j, i: (0, j * nt + i, 0))

    vec = lambda w: pl.BlockSpec((None, 1, w), lambda b, j, i: (b, 0, 0))
    in_specs = [tokspec(D), vec(D), vec(D), _full((1, D)), _full((D, N_IN_PACKED)),
                _full((1, 512)), _full((1, 256)), _full((1, MLA_Q_RANK)), _full((1, MLA_KV_RANK)),
                _full((MLA_Q_RANK, 1024)), _full((MLA_KV_RANK, 1024)), _full((MLA_KV_RANK, 512)),
                _full((LANES, 1024)), _full((512, 512))]
    args = [x, shift, scale, gpre, lw["w_in"], lw["g_q"], lw["g_k"], lw["g_cq"], lw["g_ckv"],
            lw["w_uq"], lw["w_uk"], lw["w_uv"], lw["e_kr"], lw["bd64"]]
    if rope:
        in_specs += [tabspec(), tabspec(), tabspec()]
        args += list(tabs)
    widths = [D, 512, 1024, 256, 256, 1024, 512]
    out_specs = [tokspec(w) for w in widths]
    out_shape = [jax.ShapeDtypeStruct((B, L, w), BF16) for w in widths]
    out_specs.append(pl.BlockSpec((None, tm, S5_WIDTH), lambda b, j, i: (b, i, j)))
    out_shape.append(jax.ShapeDtypeStruct((B, tseg, nseg * S5_WIDTH), F32))
    return pl.pallas_call(
        functools.partial(_in_kernel, rope=rope),
        grid=(B, nseg, nt),
        in_specs=in_specs,
        out_specs=out_specs,
        out_shape=out_shape,
        compiler_params=_cparams(3),
        name="in_proj_rope" if rope else "in_proj",
    )(*args)


ONES_ROWS = 16
ATTN_KEY_CHUNK = 512
ATTN_MIN_DENOM = 2.0 ** -60


def _attn_kernel(*refs, gqa, chunks, seg_starts):
    nseg = len(seg_starts)
    q_ref = refs[0]
    k_refs = refs[1:1 + nseg]
    v_refs = refs[1 + nseg:1 + 2 * nseg]
    o_ref, vt_ref, kn_ref, s0_ref, s1_ref, p0_ref, p1_ref = refs[1 + 2 * nseg:]
    s_refs, p_refs = (s0_ref, s1_ref), (p0_ref, p1_ref)
    hd = LANES // 2

    def keys(par, seg):
        return k_refs[seg][...] if gqa else k_refs[seg][:, par * HEAD_SLOT:(par + 1) * HEAD_SLOT]

    @pl.when(pl.program_id(2) == 0)
    def _():
        for v_ref, start in zip(v_refs, seg_starts):
            vt = v_ref[...].astype(F32).T
            ones = jnp.ones((ONES_ROWS, vt.shape[1]), BF16)
            for par in range(2):
                vt_ref[par, 0:hd, start:start + vt.shape[1]] = vt[par * hd:(par + 1) * hd, :].astype(BF16)
                vt_ref[par, hd:hd + ONES_ROWS, start:start + vt.shape[1]] = ones
        for par in range(2):
            big = None
            for seg in range(nseg):
                kf = keys(par, seg).astype(F32)
                n2 = jnp.max(jnp.sum(kf * kf, axis=1, keepdims=True), axis=0, keepdims=True)
                big = n2 if big is None else jnp.maximum(big, n2)
            kn_ref[par] = jnp.broadcast_to(jnp.sqrt(big), kn_ref.shape[1:])

    qt = q_ref[...].astype(F32).T
    qh = []
    for par in range(2):
        if gqa:
            row = lax.broadcasted_iota(jnp.int32, (LANES, 1), 0)
            keep = (row < hd) if par == 0 else (row >= hd)
            qh.append(jnp.where(keep, qt, 0.0).astype(BF16))
        else:
            qh.append(qt[par * HEAD_SLOT:(par + 1) * HEAD_SLOT, :].astype(BF16))

    outs, dens = [], []
    for par in range(2):
        qf = qh[par].astype(F32)
        bound = jnp.sqrt(jnp.sum(qf * qf, axis=0, keepdims=True)) * kn_ref[par, 0:1, 0:1]
        acc = None
        for seg, start in enumerate(seg_starts):
            kh = keys(par, seg)
            pt = jnp.exp2(_dot(kh, qh[par]) - bound).astype(BF16)
            t = _dot(vt_ref[par, :, start:start + kh.shape[0]], pt)
            acc = t if acc is None else acc + t
        outs.append(acc[0:hd, :] / acc[hd:hd + 1, :])
        dens.append(acc[hd:hd + 1, :])
    o_ref[...] = jnp.concatenate(outs, axis=0).T.astype(o_ref.dtype)
    underflow = jnp.min(jnp.minimum(dens[0], dens[1])) < ATTN_MIN_DENOM

    @pl.when(underflow)
    def _():
        _attn_exact(o_ref, k_refs, vt_ref, s_refs, p_refs, qh, gqa=gqa, chunks=chunks, seg_starts=seg_starts)


def _attn_exact(o_ref, k_refs, vt_ref, s_refs, p_refs, qh, *, gqa, chunks, seg_starts):
    hd = LANES // 2

    def scores(par, seg, r0, rows):
        k_ref = k_refs[seg]
        kh = k_ref[r0:r0 + rows, :] if gqa else k_ref[r0:r0 + rows, par * HEAD_SLOT:(par + 1) * HEAD_SLOT]
        st = _dot(kh, qh[par])
        c0 = seg_starts[seg] + r0
        s_refs[par][c0:c0 + rows, :] = st
        return jnp.max(st, axis=0, keepdims=True)

    def probs(par, seg, r0, rows, m):
        c0 = seg_starts[seg] + r0
        p_refs[par][c0:c0 + rows, :] = jnp.exp2(s_refs[par][c0:c0 + rows, :] - m).astype(BF16)

    def weighted(par, seg, r0, rows):
        c0 = seg_starts[seg] + r0
        return _dot(vt_ref[par, :, c0:c0 + rows], p_refs[par][c0:c0 + rows, :])

    def fold(a, b):
        return b if a is None else a + b

    m0 = m1 = acc0 = acc1 = None
    for ch in chunks:
        sm = scores(0, *ch)
        m0 = sm if m0 is None else jnp.maximum(m0, sm)
    for ch in chunks:
        sm = scores(1, *ch)
        m1 = sm if m1 is None else jnp.maximum(m1, sm)
        probs(0, *ch, m0)
    for ch in chunks:
        acc0 = fold(acc0, weighted(0, *ch))
        probs(1, *ch, m1)
    for ch in chunks:
        acc1 = fold(acc1, weighted(1, *ch))
    outs = [a[0:hd, :] / a[hd:hd + 1, :] for a in (acc0, acc1)]
    o_ref[...] = jnp.concatenate(outs, axis=0).T.astype(o_ref.dtype)


def _attention(q, ks, vs, gqa):
    B, Lq, _ = q.shape
    lens = [k.shape[1] for k in ks]
    seg_starts = tuple(int(s) for s in np.cumsum([0] + lens[:-1]))
    Lk = sum(lens)
    tq = min(512, Lq)
    n_pairs = 4
    if gqa:
        q_spec = pl.BlockSpec((None, tq, LANES), lambda b, p, i: (b, i, p))
        k_specs = [pl.BlockSpec((None, n, LANES), lambda b, p, i: (b, 0, p // 2)) for n in lens]
        v_specs = [pl.BlockSpec((None, n, LANES), lambda b, p, i: (b, 0, p // 2)) for n in lens]
    else:
        q_spec = pl.BlockSpec((None, tq, 2 * HEAD_SLOT), lambda b, p, i: (b, i, p))
        k_specs = [pl.BlockSpec((None, n, 2 * HEAD_SLOT), lambda b, p, i: (b, 0, p)) for n in lens]
        v_specs = [pl.BlockSpec((None, n, LANES), lambda b, p, i: (b, 0, p)) for n in lens]
    chunks = tuple((seg, r0, min(ATTN_KEY_CHUNK, n - r0))
                   for seg, n in enumerate(lens) for r0 in range(0, n, ATTN_KEY_CHUNK))
    return pl.pallas_call(
        functools.partial(_attn_kernel, gqa=gqa, chunks=chunks, seg_starts=seg_starts),
        grid=(B, n_pairs, Lq // tq),
        in_specs=[q_spec] + k_specs + v_specs,
        out_specs=pl.BlockSpec((None, tq, LANES), lambda b, p, i: (b, i, p)),
        out_shape=jax.ShapeDtypeStruct((B, Lq, n_pairs * LANES), BF16),
        scratch_shapes=[pltpu.VMEM((2, LANES // 2 + ONES_ROWS, Lk), BF16), pltpu.VMEM((2, 8, LANES), F32),
                        pltpu.VMEM((Lk, tq), F32), pltpu.VMEM((Lk, tq), F32),
                        pltpu.VMEM((Lk, tq), BF16), pltpu.VMEM((Lk, tq), BF16)],
        compiler_params=_cparams(3),
        name="attn_gqa" if gqa else "attn_mla",
    )(q, *ks, *vs)


S5_COLS = 512
assert S5_COLS // S5_STATE * S5_GROUP == LANES


def _s5_kernel(u_ref, bre_ref, bim_ref, cre_ref, cim_ref, a_ref, at_ref, init_ref,
               y_ref, fin_ref, vre, vim, st, *, rev, ti):
    ps = pl.program_id(1)
    ch = pl.program_id(2)
    n = S5_N

    @pl.when((ps == 0) & (ch == 0))
    def _():
        st[...] = jnp.zeros_like(st)

    @pl.when((ps == 1) & (ch == 0))
    def _():
        ends = [(st[0, j:j + 1, :], st[1, j:j + 1, :]) for j in range(S5_SEGS)]
        atr, ati = at_ref[0], at_ref[1]
        cr, ci = init_ref[0], init_ref[1]
        order = range(S5_SEGS - 1, -1, -1) if rev else range(S5_SEGS)
        for j in order:
            st[0, j:j + 1, :] = cr
            st[1, j:j + 1, :] = ci
            er, ei = ends[j]
            cr, ci = atr * cr - ati * ci + er, atr * ci + ati * cr + ei
        fin_ref[0] = cr
        fin_ref[1] = ci

    ub = u_ref[...].reshape(ti * S5_SEGS, S5_WIDTH).astype(BF16)
    for m in range(n // S5_COLS):
        ch = slice(m * LANES, (m + 1) * LANES)
        cols = slice(m * S5_COLS, (m + 1) * S5_COLS)
        vre[:, :, cols] = _dot(ub[:, ch], bre_ref[ch, cols]).reshape(ti, S5_SEGS, S5_COLS)
        vim[:, :, cols] = _dot(ub[:, ch], bim_ref[ch, cols]).reshape(ti, S5_SEGS, S5_COLS)

    def scan(store):
        for cb in range(n // S5_COLS):
            cols = slice(cb * S5_COLS, (cb + 1) * S5_COLS)
            ar = jnp.broadcast_to(a_ref[0, :, cols], (S5_SEGS, S5_COLS))
            ai = jnp.broadcast_to(a_ref[1, :, cols], (S5_SEGS, S5_COLS))

            def body(i, carry):
                xr, xi = carry
                idx = (ti - 1 - i) if rev else i
                nr = ar * xr - ai * xi + vre[idx, :, cols]
                ni = ar * xi + ai * xr + vim[idx, :, cols]
                if store:
                    vre[idx, :, cols] = nr
                    vim[idx, :, cols] = ni
                return nr, ni

            xr, xi = lax.fori_loop(0, ti, body, (st[0, :, cols], st[1, :, cols]), unroll=4)
            st[0, :, cols] = xr
            st[1, :, cols] = xi

    @pl.when(ps == 0)
    def _():
        scan(False)

    @pl.when(ps == 1)
    def _():
        scan(True)
        xr = vre[...].reshape(ti * S5_SEGS, n).astype(BF16)
        xi = vim[...].reshape(ti * S5_SEGS, n).astype(BF16)
        y = _dot(xr, cre_ref[...]) + _dot(xi, cim_ref[...])
        y_ref[...] = y.reshape(ti, S5_SEGS, S5_WIDTH)


def _s5_scan(u, sw, init, rev):
    B, T, _, _ = u.shape
    ti = min(64, T)
    nch = T // ti
    n = S5_N

    def chunk(c):
        return (nch - 1 - c) if rev else c

    d = 1 if rev else 0
    return pl.pallas_call(
        functools.partial(_s5_kernel, rev=rev, ti=ti),
        grid=(B, 2, nch),
        in_specs=[pl.BlockSpec((None, ti, S5_SEGS, S5_WIDTH), lambda b, p, c: (b, chunk(c), 0, 0)),
                  _full((S5_WIDTH, n)), _full((S5_WIDTH, n)), _full((n, S5_WIDTH)), _full((n, S5_WIDTH)),
                  _full((2, 1, n)), _full((2, 1, n)),
                  pl.BlockSpec((None, 2, 1, n), lambda b, p, c: (b, 0, 0, 0))],
        out_specs=[pl.BlockSpec((None, ti, S5_SEGS, S5_WIDTH),
                                lambda b, p, c: (b, chunk(c * p), 0, 0)),
                   pl.BlockSpec((None, 2, 1, n), lambda b, p, c: (b, 0, 0, 0))],
        out_shape=[jax.ShapeDtypeStruct(u.shape, F32), jax.ShapeDtypeStruct((B, 2, 1, n), F32)],
        scratch_shapes=[pltpu.VMEM((ti, S5_SEGS, n), F32), pltpu.VMEM((ti, S5_SEGS, n), F32),
                        pltpu.VMEM((2, S5_SEGS, n), F32)],
        compiler_params=_cparams(3),
        name="s5_bwd" if rev else "s5_fwd",
    )(u, sw["b_re"][d], sw["b_im"][d], sw["c_re"][d], sw["c_im_neg"][d],
      sw["a"][d], sw["a_t"][(d, T)], init)


def _glu_kernel(u_ref, yf_ref, yb_ref, d_ref, w_ref, o_ref):
    y = d_ref[...] * u_ref[...] + yf_ref[...] + yb_ref[...]
    g = 0.5 * y * (1.0 + jnp.tanh(math.sqrt(2.0 / math.pi) * (y + 0.044715 * (y * y * y))))
    t = _dot(g.astype(BF16), w_ref[...])
    o_ref[...] = (t[:, 0:S5_WIDTH] * jax.nn.sigmoid(t[:, S5_WIDTH:2 * S5_WIDTH])).astype(o_ref.dtype)


def _s5_glu(u, yf, yb, dskip, wglu, nseg):
    B, T, _ = u.shape
    tm = min(256, T)
    nt = T // tm
    seg = pl.BlockSpec((None, tm, S5_WIDTH), lambda b, j, i: (b, i, j))
    return pl.pallas_call(
        _glu_kernel,
        grid=(B, nseg, nt),
        in_specs=[seg, seg, seg, _full((1, S5_WIDTH)), _full((S5_WIDTH, 2 * S5_WIDTH))],
        out_specs=pl.BlockSpec((None, tm, S5_WIDTH), lambda b, j, i: (b, j * nt + i, 0)),
        out_shape=jax.ShapeDtypeStruct((B, T * nseg, S5_WIDTH), BF16),
        compiler_params=_cparams(3),
        name="s5_glu",
    )(u, yf, yb, dskip, wglu)


def _merge_kernel(h_ref, ya_ref, yb_ref, yc_ref, x_ref, gate_ref, wmg_ref, bmg_ref, wbr_ref, wout_ref,
                  gpost_ref, o_ref):
    h = h_ref[...]
    d = x_ref.shape[-1]
    acc = None
    for n, y_ref in enumerate((ya_ref, yb_ref, yc_ref)):
        g = jax.nn.sigmoid(_dot(h, wmg_ref[:, n * d:(n + 1) * d]) + bmg_ref[:, n * d:(n + 1) * d])
        t = g * _dot(y_ref[...], wbr_ref[n])
        acc = t if acc is None else acc + t
    y = _dot(acc.astype(BF16), wout_ref[...])
    o_ref[...] = x_ref[...] + gate_ref[...] * _rms(y, gpost_ref[...])


def _merge(h, ya, yb, yc, x, gate, lw):
    B, L, D = x.shape
    tm = min(256, L)
    tok = lambda w: pl.BlockSpec((None, tm, w), lambda b, i: (b, i, 0))
    return pl.pallas_call(
        _merge_kernel,
        grid=(B, L // tm),
        in_specs=[tok(D), tok(512), tok(512), tok(512), tok(D),
                  pl.BlockSpec((None, 1, D), lambda b, i: (b, 0, 0)),
                  _full((D, N_BRANCHES * D)), _full((1, N_BRANCHES * D)), _full((N_BRANCHES, 512, D)),
                  _full((D, D)), _full((1, D))],
        out_specs=tok(D),
        out_shape=jax.ShapeDtypeStruct((B, L, D), F32),
        compiler_params=_cparams(2),
        name="merge",
    )(h, ya, yb, yc, x, gate, lw["w_mg"], lw["b_mg"], lw["w_branch"], lw["w_out"], lw["g_mix_post"])


def _ffn_kernel(x_ref, sh_ref, sc_ref, gate_ref, gpre_ref, gpost_ref, w1_ref, w3_ref, w2_ref, o_ref,
                h_s, acc_s):
    f = pl.program_id(2)

    @pl.when(f == 0)
    def _():
        h = _rms(x_ref[...], gpre_ref[...]) * (1.0 + sc_ref[...]) + sh_ref[...]
        h_s[...] = h.astype(BF16)
        acc_s[...] = jnp.zeros_like(acc_s)

    h = h_s[...]
    a = _dot(h, w1_ref[...])
    g = (a * jax.nn.sigmoid(a)) * _dot(h, w3_ref[...])
    acc_s[...] += _dot(g.astype(BF16), w2_ref[...])

    @pl.when(f == pl.num_programs(2) - 1)
    def _():
        o_ref[...] = x_ref[...] + gate_ref[...] * _rms(acc_s[...], gpost_ref[...])


def _ffn(x, shift, scale, gate, gpre, gpost, w1, w3, w2):
    B, L, D = x.shape
    dff = w1.shape[1]
    tm = min(512, L)
    tf = dff // 2 if (dff // 2) % LANES == 0 else dff
    tok = pl.BlockSpec((None, tm, D), lambda b, i, f: (b, i, 0))
    vec = pl.BlockSpec((None, 1, D), lambda b, i, f: (b, 0, 0))
    return pl.pallas_call(
        _ffn_kernel,
        grid=(B, L // tm, dff // tf),
        in_specs=[tok, vec, vec, vec, _full((1, D)), _full((1, D)),
                  pl.BlockSpec((D, tf), lambda b, i, f: (0, f)),
                  pl.BlockSpec((D, tf), lambda b, i, f: (0, f)),
                  pl.BlockSpec((tf, D), lambda b, i, f: (f, 0))],
        out_specs=tok,
        out_shape=jax.ShapeDtypeStruct((B, L, D), F32),
        scratch_shapes=[pltpu.VMEM((tm, D), BF16), pltpu.VMEM((tm, D), F32)],
        compiler_params=_cparams(3),
        name="ffn",
    )(x, shift, scale, gate, gpre, gpost, w1, w3, w2)


MOE_BLOCK = 1024
MOE_CHUNK = 128
MOE_TILE = 512
MOE_SORT_ROWS = 512


def _lane_col(a, lane, k):
    return jnp.sum(jnp.where(lane == k, a, 0.0), axis=1, keepdims=True)


def _router_kernel(x_ref, sh_ref, sc_ref, gpre_ref, rw_ref, rb_ref, h_out, route_out, cnt_out):
    lane = lax.broadcasted_iota(jnp.int32, (1, LANES), 1)
    h = _rms(x_ref[...], gpre_ref[...]) * (1.0 + sc_ref[...]) + sh_ref[...]
    h_out[...] = h.astype(BF16)
    logits = jnp.dot(h, rw_ref[...], preferred_element_type=F32,
                     precision=lax.Precision.HIGHEST) + rb_ref[...]
    m1 = jnp.max(logits, axis=1, keepdims=True)
    i1 = jnp.min(jnp.where(logits == m1, lane, LANES), axis=1, keepdims=True)
    rest = jnp.where(lane == i1, -jnp.inf, logits)
    m2 = jnp.max(rest, axis=1, keepdims=True)
    i2 = jnp.min(jnp.where(rest == m2, lane, LANES), axis=1, keepdims=True)
    e2 = jnp.exp(m2 - m1)
    den = 1.0 + e2
    route_out[...] = (jnp.where(lane == 0, i1.astype(F32), 0.0) + jnp.where(lane == 1, i2.astype(F32), 0.0)
                      + jnp.where(lane == 2, 1.0 / den, 0.0) + jnp.where(lane == 3, e2 / den, 0.0))
    member = jnp.where((lane == i1) | (lane == i2), 1.0, 0.0)
    cnt_out[...] = jnp.sum(member, axis=0, keepdims=True)


def _bf16_pieces(w):
    hi = w.astype(BF16).astype(F32)
    r1 = w - hi
    lo = r1.astype(BF16).astype(F32)
    return hi, lo, r1 - lo


def _dispatch_kernel(h_ref, route_ref, ltri_ref, ustr_ref, xs_out, pos_out):
    d = h_ref.shape[1]
    lane = lax.broadcasted_iota(jnp.int32, (1, LANES), 1)
    lanef = lane.astype(F32)
    route = route_ref[...]
    i1, i2 = _lane_col(route, lane, 0), _lane_col(route, lane, 1)
    g1, g2 = _lane_col(route, lane, 2), _lane_col(route, lane, 3)
    m1, m2 = lanef == i1, lanef == i2
    member = jnp.where(m1 | m2, 1.0, 0.0)
    before = _dot(ltri_ref[...], member.astype(BF16))
    counts = jnp.sum(member, axis=0, keepdims=True)
    padded = jnp.floor((counts + (MOE_CHUNK - 1.0)) * (1.0 / MOE_CHUNK)) * MOE_CHUNK
    seg = _dot(jnp.broadcast_to(padded, (8, LANES)).astype(BF16), ustr_ref[...])[0:1, :]
    slot = before + seg
    pos1 = jnp.sum(jnp.where(m1, slot, 0.0), axis=1, keepdims=True)
    pos2 = jnp.sum(jnp.where(m2, slot, 0.0), axis=1, keepdims=True)
    pos = jnp.where(lane == 0, pos1, 0.0) + jnp.where(lane == 1, pos2, 0.0)
    pos_out[...] = pos
    post = pos.T
    row1, row2 = post[0:1, :], post[1:2, :]

    def gate_cols(g):
        a, b, c = _bf16_pieces(g)
        return (jnp.where(lane == 0, a, 0.0) + jnp.where(lane == 1, b, 0.0)
                + jnp.where(lane == 2, c, 0.0)).astype(BF16)

    gc1, gc2 = gate_cols(g1), gate_cols(g2)
    h = h_ref[...]
    for c in range(xs_out.shape[0] // MOE_SORT_ROWS):
        rows = (lax.broadcasted_iota(jnp.int32, (MOE_SORT_ROWS, 1), 0) + c * MOE_SORT_ROWS).astype(F32)
        q1, q2 = rows == row1, rows == row2
        sl = slice(c * MOE_SORT_ROWS, (c + 1) * MOE_SORT_ROWS)
        xs_out[sl, 0:d] = _dot(jnp.where(q1 | q2, 1.0, 0.0).astype(BF16), h).astype(BF16)
        gates = (_dot(jnp.where(q1, 1.0, 0.0).astype(BF16), gc1)
                 + _dot(jnp.where(q2, 1.0, 0.0).astype(BF16), gc2))
        xs_out[sl, d:d + LANES] = gates.astype(BF16)


def _expert_kernel(*refs, n_in):
    nu_ref = refs[1]
    x_refs = refs[3:3 + n_in]
    w1_ref, w3_ref, w2_ref, o_ref, x_s, acc_s = refs[3 + n_in:]
    i = pl.program_id(0)
    f = pl.program_id(1)
    d = o_ref.shape[1]

    @pl.when(i < nu_ref[0])
    def _():
        @pl.when(f == 0)
        def _():
            for k, x_ref in enumerate(x_refs):
                x_s[k * MOE_CHUNK:(k + 1) * MOE_CHUNK, :] = x_ref[...]
            acc_s[...] = jnp.zeros_like(acc_s)

        x = x_s[:, 0:d]
        a = _dot(x, w1_ref[...])
        g = (a * jax.nn.sigmoid(a)) * _dot(x, w3_ref[...])
        acc_s[...] += _dot(g.astype(BF16), w2_ref[...])

        @pl.when(f == pl.num_programs(1) - 1)
        def _():
            gate = jnp.sum(x_s[:, d:d + LANES].astype(F32), axis=1, keepdims=True)
            o_ref[...] = (acc_s[...] * gate).astype(o_ref.dtype)

    @pl.when(i >= nu_ref[0])
    def _():
        o_ref[...] = jnp.zeros_like(o_ref)


def _combine_kernel(*refs, rc, n_in):
    ys_refs = refs[1:1 + n_in]
    pos_ref, x_ref, gate_ref, gpost_ref, o_ref, ys_s = refs[1 + n_in:]
    g = pl.program_id(1)
    for k, ys_ref in enumerate(ys_refs):
        row = pl.multiple_of((g * n_in + k) * MOE_CHUNK, MOE_CHUNK)
        ys_s[pl.ds(row, MOE_CHUNK), :] = ys_ref[...]

    @pl.when(g == rc // n_in - 1)
    def _():
        lane = lax.broadcasted_iota(jnp.int32, (1, LANES), 1)
        pos = pos_ref[...]
        pos1, pos2 = _lane_col(pos, lane, 0), _lane_col(pos, lane, 1)
        acc = None
        for c in range(rc * MOE_CHUNK // MOE_SORT_ROWS):
            cols = (lax.broadcasted_iota(jnp.int32, (1, MOE_SORT_ROWS), 1) + c * MOE_SORT_ROWS).astype(F32)
            sel = jnp.where((cols == pos1) | (cols == pos2), 1.0, 0.0).astype(BF16)
            t = _dot(sel, ys_s[c * MOE_SORT_ROWS:(c + 1) * MOE_SORT_ROWS, :])
            acc = t if acc is None else acc + t
        o_ref[...] = x_ref[...] + gate_ref[...] * _rms(acc, gpost_ref[...])


def _moe_tables(counts, rc, nt):
    ne = counts.shape[1]
    per_tile = MOE_TILE // MOE_CHUNK
    cch = (counts + MOE_CHUNK - 1) // MOE_CHUNK
    tiles_e = (jnp.sum(cch, axis=0) + per_tile - 1) // per_tile
    tile_end = jnp.cumsum(tiles_e)
    n_used = tile_end[-1]
    chunk_start = (tile_end - tiles_e)[None, :] * per_tile + (jnp.cumsum(cch, axis=0) - cch)
    seg_end = jnp.cumsum(cch, axis=1)
    r = jnp.arange(rc)
    e_r = jnp.minimum(jnp.sum(r[None, :, None] >= seg_end[:, None, :], axis=-1), ne - 1)
    onehot = (e_r[:, :, None] == jnp.arange(ne)[None, None, :]).astype(jnp.int32)
    pick = lambda t: jnp.sum(onehot * t[:, None, :], axis=-1)
    dest = pick(chunk_start) + r[None, :] - pick(seg_end - cch)
    n_valid = seg_end[:, -1]
    last = jnp.sum(jnp.where(r[None, :] == n_valid[:, None] - 1, dest, 0), axis=1, keepdims=True)
    dest = jnp.where(r[None, :] < n_valid[:, None], dest, last)
    t = jnp.maximum(jnp.minimum(jnp.arange(nt), n_used - 1), 0)
    tile_e = jnp.minimum(jnp.sum(t[:, None] >= tile_end[None, :], axis=-1), ne - 1)
    e_hot = (tile_e[:, None] == jnp.arange(ne)[None, :]).astype(jnp.int32)
    of_e = lambda a: jnp.sum(e_hot * a[None, :], axis=-1)
    of_eb = lambda a: jnp.sum(e_hot[:, None, :] * a[None, :, :], axis=-1)
    q = (t - of_e(tile_end - tiles_e))[:, None] * per_tile + jnp.arange(per_tile)[None, :]
    q = jnp.where(q < of_e(jnp.sum(cch, axis=0))[:, None], q, q[:, :1])
    blk_end = of_eb(jnp.cumsum(cch, axis=0))
    b_q = jnp.minimum(jnp.sum(q[:, :, None] >= blk_end[:, None, :], axis=-1), cch.shape[0] - 1)
    b_hot = (b_q[:, :, None] == jnp.arange(cch.shape[0])[None, None, :]).astype(jnp.int32)
    of_b = lambda a: jnp.sum(b_hot * a[:, None, :], axis=-1)
    xsrc = b_q * rc + of_b(of_eb(seg_end - cch)) + q - of_b(blk_end - of_eb(cch))
    i32 = lambda a: a.reshape(-1).astype(jnp.int32)
    return i32(dest), i32(xsrc), i32(tile_e), i32(n_used)


def _moe(x, shift, scale, gate, gpre, gpost, rw, rb, w1, w3, w2):
    B, L, D = x.shape
    ne, _, dff = w1.shape
    n = B * L
    t = min(MOE_BLOCK, L)
    nblk, per_batch = n // t, L // t
    rc = (2 * t + ne * MOE_CHUNK) // MOE_CHUNK
    per_tile = MOE_TILE // MOE_CHUNK
    nt = (2 * n // MOE_CHUNK + nblk * ne + per_tile - 1) // per_tile + ne
    tf = dff // 2 if (dff // 2) % LANES == 0 else dff
    nf = dff // tf
    xf = x.reshape(n, D)
    wide = D + LANES

    blk = lambda w: pl.BlockSpec((t, w), lambda i: (i, 0))
    vec1 = pl.BlockSpec((None, 1, D), lambda i: (i // per_batch, 0, 0))
    h, route, cnt = pl.pallas_call(
        _router_kernel,
        grid=(nblk,),
        in_specs=[blk(D), vec1, vec1, _full((1, D)), _full((D, LANES)), _full((1, LANES))],
        out_specs=[blk(D), blk(LANES), pl.BlockSpec((None, 1, LANES), lambda i: (i, 0, 0))],
        out_shape=[jax.ShapeDtypeStruct((n, D), BF16), jax.ShapeDtypeStruct((n, LANES), F32),
                   jax.ShapeDtypeStruct((nblk, 1, LANES), F32)],
        compiler_params=_cparams(1),
        name="moe_router",
    )(xf, shift, scale, gpre, rw, rb)

    counts = cnt[:, 0, :ne].astype(jnp.int32)
    dest, xsrc, tile_e, n_used = _moe_tables(counts, rc, nt)

    ltri = jnp.asarray(np.tril(np.ones((t, t), np.float32), -1), BF16)
    ustr = jnp.asarray(np.triu(np.ones((LANES, LANES), np.float32), 1), BF16)
    xs, pos = pl.pallas_call(
        _dispatch_kernel,
        grid=(nblk,),
        in_specs=[blk(D), blk(LANES), _full((t, t)), _full((LANES, LANES))],
        out_specs=[pl.BlockSpec((rc * MOE_CHUNK, wide), lambda b: (b, 0)), blk(LANES)],
        out_shape=[jax.ShapeDtypeStruct((nblk * rc * MOE_CHUNK, wide), BF16),
                   jax.ShapeDtypeStruct((n, LANES), F32)],
        compiler_params=_cparams(1),
        name="moe_dispatch",
    )(h, route, ltri, ustr)

    def tile(i, nu):
        return jnp.maximum(jnp.minimum(i, nu[0] - 1), 0)

    def fcol(i, f, nu):
        return jnp.where(i < nu[0], f, nf - 1)

    def chunk_in(k):
        return pl.BlockSpec((MOE_CHUNK, wide), lambda i, f, te, nu, src: (src[tile(i, nu) * per_tile + k], 0))

    ys = pl.pallas_call(
        functools.partial(_expert_kernel, n_in=per_tile),
        grid_spec=pltpu.PrefetchScalarGridSpec(
            num_scalar_prefetch=3,
            grid=(nt, nf),
            in_specs=[chunk_in(k) for k in range(per_tile)] + [
                pl.BlockSpec((None, D, tf), lambda i, f, te, nu, src: (te[i], 0, fcol(i, f, nu))),
                pl.BlockSpec((None, D, tf), lambda i, f, te, nu, src: (te[i], 0, fcol(i, f, nu))),
                pl.BlockSpec((None, tf, D), lambda i, f, te, nu, src: (te[i], fcol(i, f, nu), 0))],
            out_specs=pl.BlockSpec((MOE_TILE, D), lambda i, f, te, nu, src: (i, 0)),
            scratch_shapes=[pltpu.VMEM((MOE_TILE, wide), BF16), pltpu.VMEM((MOE_TILE, D), F32)]),
        out_shape=jax.ShapeDtypeStruct((nt * MOE_TILE, D), BF16),
        compiler_params=_cparams(2),
        name="moe_experts",
    )(tile_e, n_used, xsrc, *([xs] * per_tile), w1, w3, w2)

    def chunk_out(k):
        return pl.BlockSpec((MOE_CHUNK, D), lambda b, g, src: (src[b * rc + g * per_tile + k], 0))

    out = pl.pallas_call(
        functools.partial(_combine_kernel, rc=rc, n_in=per_tile),
        grid_spec=pltpu.PrefetchScalarGridSpec(
            num_scalar_prefetch=1,
            grid=(nblk, rc // per_tile),
            in_specs=[chunk_out(k) for k in range(per_tile)] + [
                pl.BlockSpec((t, LANES), lambda b, g, src: (b, 0)),
                pl.BlockSpec((t, D), lambda b, g, src: (b, 0)),
                pl.BlockSpec((None, 1, D), lambda b, g, src: (b // per_batch, 0, 0)),
                pl.BlockSpec((1, D), lambda b, g, src: (0, 0))],
            out_specs=pl.BlockSpec((t, D), lambda b, g, src: (b, 0)),
            scratch_shapes=[pltpu.VMEM((rc * MOE_CHUNK, D), BF16)]),
        out_shape=jax.ShapeDtypeStruct((n, D), F32),
        compiler_params=_cparams(2),
        name="moe_combine",
    )(dest, *([ys] * per_tile), pos, xf, gate, gpost)
    return out.reshape(B, L, D)


def _rope_tables(n_lat):
    pos = jnp.arange(n_lat, dtype=jnp.int32)
    row = (pos // GRID_W).astype(F32)
    col = (pos % GRID_W).astype(F32)

    def cos_sin(rot_dim):
        axis_dim = rot_dim // 2
        inv_freq = ROPE_THETA ** (-jnp.arange(0, axis_dim, 2, dtype=F32) / axis_dim)
        ang = jnp.concatenate([row[:, None] * inv_freq, col[:, None] * inv_freq], axis=-1)
        return jnp.cos(ang), jnp.sin(ang)

    def table(cos, sin, start, stop):
        reps = (stop - start) // (2 * cos.shape[1])
        zero = jnp.zeros_like(sin)

        def lanes(first, second, fill):
            body = jnp.tile(jnp.concatenate([first, second], axis=1), (1, reps))
            return jnp.pad(body, ((0, 0), (start, LANES - stop)), constant_values=fill)

        return jnp.stack([lanes(cos, cos, 1.0), lanes(-sin, zero, 0.0), lanes(zero, sin, 0.0)]).astype(F32)

    gc, gs = cos_sin(GQA_HEAD_DIM)
    mc, ms = cos_sin(MLA_ROPE_DIM)
    t_gqa = table(gc, gs, 0, LANES)
    t_mq = table(mc, ms, MLA_NOPE_DIM, MLA_NOPE_DIM + MLA_ROPE_DIM)
    t_kr = table(mc, ms, 0, MLA_ROPE_DIM)
    return t_gqa, t_mq, t_kr


def _pack_layer(p, l):
    D = p["w_in"].shape[1]
    w = p["w_in"][l]
    offs = np.cumsum([0, 512, MLA_Q_RANK, 128, 128, MLA_KV_RANK, MLA_ROPE_DIM, S5_WIDTH])
    qg, qm, kg, vg, ckv, kr, u = [w[:, offs[i]:offs[i + 1]] for i in range(7)]
    dup = lambda t: jnp.concatenate([t[:, 0:64], t[:, 0:64], t[:, 64:128], t[:, 64:128]], axis=1)
    w_in = jnp.concatenate([qg, qm, dup(kg), dup(vg), ckv, u, kr,
                            jnp.zeros((D, LANES - MLA_ROPE_DIM), F32)], axis=1).astype(BF16)

    wq = p["w_uq"][l].reshape(MLA_Q_RANK, MLA_HEADS, MLA_NOPE_DIM + MLA_ROPE_DIM)
    w_uq = jnp.pad(wq, ((0, 0), (0, 0), (0, HEAD_SLOT - wq.shape[-1]))).reshape(MLA_Q_RANK, -1).astype(BF16)
    wkv = p["w_ukv"][l].reshape(MLA_KV_RANK, MLA_HEADS, MLA_NOPE_DIM + MLA_V_DIM)
    w_uk = jnp.pad(wkv[:, :, :MLA_NOPE_DIM], ((0, 0), (0, 0), (0, HEAD_SLOT - MLA_NOPE_DIM)))
    w_uk = w_uk.reshape(MLA_KV_RANK, -1).astype(BF16)
    w_uv = wkv[:, :, MLA_NOPE_DIM:].reshape(MLA_KV_RANK, -1).astype(BF16)
    e = np.zeros((LANES, MLA_HEADS, HEAD_SLOT), np.float32)
    for r in range(MLA_ROPE_DIM):
        e[r, :, MLA_NOPE_DIM + r] = 1.0
    bd = np.kron(np.eye(512 // GQA_HEAD_DIM, dtype=np.float32), np.ones((GQA_HEAD_DIM, GQA_HEAD_DIM), np.float32))
    return {
        "w_in": w_in,
        "g_q": jnp.tile(p["g_q"][l], GQA_HEADS).reshape(1, -1),
        "g_k": jnp.tile(p["g_k"][l], 2 * GQA_KV_HEADS).reshape(1, -1),
        "g_cq": p["g_cq"][l].reshape(1, -1),
        "g_ckv": p["g_ckv"][l].reshape(1, -1),
        "w_uq": w_uq, "w_uk": w_uk, "w_uv": w_uv,
        "e_kr": jnp.asarray(e.reshape(LANES, -1), BF16),
        "bd64": jnp.asarray(bd, BF16),
        "w_mg": p["w_merge_gate"][l].astype(BF16),
        "b_mg": p["b_merge_gate"][l].reshape(1, -1),
        "w_branch": p["w_branch"][l].astype(BF16),
        "w_out": p["w_out"][l].astype(BF16),
        "g_mix_pre": p["g_mix_pre"][l].reshape(1, -1),
        "g_mix_post": p["g_mix_post"][l].reshape(1, -1),
        "g_ffn_pre": p["g_ffn_pre"][l].reshape(1, -1),
        "g_ffn_post": p["g_ffn_post"][l].reshape(1, -1),
        "w_glu": p["w_glu"][l].astype(BF16),
        "s5_d": p["s5_d"][l].reshape(1, -1),
    }


def _cpow(ar, ai, n):
    rr, ri = None, None
    br, bi = ar, ai
    while n:
        if n & 1:
            rr, ri = (br, bi) if rr is None else (rr * br - ri * bi, rr * bi + ri * br)
        n >>= 1
        if n:
            br, bi = br * br - bi * bi, 2.0 * br * bi
    return rr, ri


def _s5_weights(p, l, seg_lens):
    G, P, H = S5_GROUPS, S5_STATE, S5_GROUP
    eye = jnp.eye(G, dtype=F32)
    out = {"b_re": [], "b_im": [], "c_re": [], "c_im_neg": [], "a": [], "a_t": {}}
    for d in range(2):
        lr = jnp.minimum(p["s5_lambda_re"][l, d], -1e-4)
        li = p["s5_lambda_im"][l, d]
        dt = jnp.exp(p["s5_log_dt"][l, d])[:, None]
        mag = jnp.exp(lr * dt)
        abr, abi = mag * jnp.cos(li * dt), mag * jnp.sin(li * dt)
        den = lr * lr + li * li
        fr = ((abr - 1.0) * lr + abi * li) / den
        fi = (abi * lr - (abr - 1.0) * li) / den
        br, bi = p["s5_b_re"][l, d], p["s5_b_im"][l, d]
        bbr = fr[..., None] * br - fi[..., None] * bi
        bbi = fr[..., None] * bi + fi[..., None] * br
        blk_b = lambda t: jnp.einsum("gph,gk->ghkp", t, eye).reshape(G * H, G * P).astype(BF16)
        blk_c = lambda t: jnp.einsum("ghp,gk->gpkh", t, eye).reshape(G * P, G * H).astype(BF16)
        out["b_re"].append(blk_b(bbr))
        out["b_im"].append(blk_b(bbi))
        out["c_re"].append(blk_c(p["s5_c_re"][l, d]))
        out["c_im_neg"].append(blk_c(-p["s5_c_im"][l, d]))
        out["a"].append(jnp.stack([abr.reshape(1, -1), abi.reshape(1, -1)]))
        for t in seg_lens:
            tr, ti = _cpow(abr, abi, t)
            out["a_t"][(d, t)] = jnp.stack([tr.reshape(1, -1), ti.reshape(1, -1)])
    return out


def _seg_view(t):
    B, T, _ = t.shape
    return t.reshape(B, T, S5_SEGS, S5_WIDTH)


def _to_segments(t):
    B, L, W = t.shape
    return t.reshape(B, S5_SEGS, L // S5_SEGS, W).transpose(0, 2, 1, 3).reshape(B, L // S5_SEGS, S5_SEGS * W)


def kernel(x, c, ctx, c_ctx, w_mod, b_mod, g_mix_pre, g_mix_post, g_ffn_pre, g_ffn_post, w_in, g_q, g_k, g_cq, g_ckv, w_uq, w_ukv, s5_lambda_re, s5_lambda_im, s5_log_dt, s5_b_re, s5_b_im, s5_c_re, s5_c_im, s5_d, w_glu, w_branch, w_merge_gate, b_merge_gate, w_out, ffn_w1, ffn_w3, ffn_w2, router_w, router_b, moe_w1, moe_w3, moe_w2):
    p = dict(w_in=w_in, g_q=g_q, g_k=g_k, g_cq=g_cq, g_ckv=g_ckv, w_uq=w_uq, w_ukv=w_ukv,
             s5_lambda_re=s5_lambda_re, s5_lambda_im=s5_lambda_im, s5_log_dt=s5_log_dt,
             s5_b_re=s5_b_re, s5_b_im=s5_b_im, s5_c_re=s5_c_re, s5_c_im=s5_c_im, s5_d=s5_d, w_glu=w_glu,
             w_branch=w_branch, w_merge_gate=w_merge_gate, b_merge_gate=b_merge_gate, w_out=w_out,
             g_mix_pre=g_mix_pre, g_mix_post=g_mix_post, g_ffn_pre=g_ffn_pre, g_ffn_post=g_ffn_post)
    B, L, D = x.shape
    Lc = ctx.shape[1]
    depth = w_mod.shape[0]
    tabs = _rope_tables(L)
    t_lat, t_ctx = L // S5_SEGS, Lc // S5_SEGS

    cvec = jnp.concatenate([c, c_ctx[None, :], jnp.zeros((8 - B - 1, D), F32)], axis=0)
    lat, cx = x, ctx
    for layer in range(depth):
        last = layer == depth - 1
        lw = _pack_layer(p, layer)
        sw = _s5_weights(p, layer, (t_lat, t_ctx))
        mod = _modulation(cvec, w_mod[layer], b_mod[layer])
        mod_l = [mod[:B, i * D:(i + 1) * D].reshape(B, 1, D) for i in range(6)]
        mod_c = [jnp.broadcast_to(mod[B, i * D:(i + 1) * D], (B, 1, D)) for i in range(6)]

        h_l, qg_l, qm_l, kd_l, vd_l, km_l, vm_l, u_l = _in_proj(
            lat, mod_l[0], mod_l[1], lw["g_mix_pre"], lw, tabs, S5_SEGS)
        h_c, qg_c, qm_c, kd_c, vd_c, km_c, vm_c, u_c = _in_proj(
            cx, mod_c[0], mod_c[1], lw["g_mix_pre"], lw, None, 1)
        ya_l = _attention(qg_l, (kd_c, kd_l), (vd_c, vd_l), gqa=True)
        yb_l = _attention(qm_l, (km_c, km_l), (vm_c, vm_l), gqa=False)

        u_cs = _to_segments(u_c)
        zero = jnp.zeros((B, 2, 1, S5_N), F32)
        yf_c, fin_f = _s5_scan(_seg_view(u_cs), sw, zero, rev=False)
        yb_c, fin_b = _s5_scan(_seg_view(u_cs), sw, zero, rev=True)
        yf_l, _ = _s5_scan(_seg_view(u_l), sw, fin_f, rev=False)
        yr_l, _ = _s5_scan(_seg_view(u_l), sw, fin_b, rev=True)
        flat = lambda t: t.reshape(t.shape[0], t.shape[1], S5_SEGS * S5_WIDTH)
        yc_l = _s5_glu(u_l, flat(yf_l), flat(yr_l), lw["s5_d"], lw["w_glu"], S5_SEGS)

        lat = _merge(h_l, ya_l, yb_l, yc_l, lat, mod_l[2], lw)
        if not last:
            ya_c = _attention(qg_c, (kd_c,), (vd_c,), gqa=True)
            yb_c2 = _attention(qm_c, (km_c,), (vm_c,), gqa=False)
            yc_c = _s5_glu(u_cs, flat(yf_c), flat(yb_c), lw["s5_d"], lw["w_glu"], S5_SEGS)
            cx = _merge(h_c, ya_c, yb_c2, yc_c, cx, mod_c[2], lw)

        i = layer // 2
        if layer % 2 == 0:
            w1, w3, w2 = ffn_w1[i].astype(BF16), ffn_w3[i].astype(BF16), ffn_w2[i].astype(BF16)
            mix = lambda t, m: _ffn(t, m[3], m[4], m[5], lw["g_ffn_pre"], lw["g_ffn_post"], w1, w3, w2)
        else:
            w1, w3, w2 = moe_w1[i].astype(BF16), moe_w3[i].astype(BF16), moe_w2[i].astype(BF16)
            rw = jnp.pad(router_w[i], ((0, 0), (0, LANES - N_EXPERTS)))
            rb = jnp.concatenate([router_b[i], jnp.full((LANES - N_EXPERTS,), -1e30, F32)]).reshape(1, LANES)
            mix = lambda t, m: _moe(t, m[3], m[4], m[5], lw["g_ffn_pre"], lw["g_ffn_post"], rw, rb, w1, w3, w2)
        lat = mix(lat, mod_l)
        if not last:
            cx = mix(cx, mod_c)
    return lat
```

```python
import functools
import math

import jax
import jax.numpy as jnp
import numpy as np
from jax import lax
from jax.experimental import pallas as pl
from jax.experimental.pallas import tpu as pltpu

F32 = jnp.float32
BF16 = jnp.bfloat16

GRID_W = 64
ROPE_THETA = 10000.0
NORM_EPS = 1e-6

GQA_HEADS = 8
GQA_KV_HEADS = 2
GQA_HEAD_DIM = 64
GQA_SCALE = 1.0 / math.sqrt(GQA_HEAD_DIM)

MLA_HEADS = 8
MLA_NOPE_DIM = 64
MLA_ROPE_DIM = 32
MLA_V_DIM = 64
MLA_Q_RANK = 384
MLA_KV_RANK = 256
MLA_SCALE = 1.0 / math.sqrt(MLA_NOPE_DIM + MLA_ROPE_DIM)
LOG2E = math.log2(math.e)

S5_WIDTH = 512
S5_GROUP = 16
S5_GROUPS = S5_WIDTH // S5_GROUP
S5_STATE = 64
S5_N = S5_GROUPS * S5_STATE
S5_SEGS = 8

N_BRANCHES = 3
N_EXPERTS = 8

LANES = 128
HEAD_SLOT = 128

C_QG = 0
C_QM = C_QG + GQA_HEADS * GQA_HEAD_DIM
C_KD = C_QM + MLA_Q_RANK
C_VD = C_KD + 2 * GQA_KV_HEADS * GQA_HEAD_DIM
C_CKV = C_VD + 2 * GQA_KV_HEADS * GQA_HEAD_DIM
C_U = C_CKV + MLA_KV_RANK
C_KR = C_U + S5_WIDTH
N_IN_PACKED = C_KR + LANES

VMEM_LIMIT = 56 * 1024 * 1024


def _cparams(n_axes):
    return pltpu.CompilerParams(dimension_semantics=("arbitrary",) * n_axes,
                                vmem_limit_bytes=VMEM_LIMIT)


def _dot(a, b):
    return jnp.dot(a, b, preferred_element_type=F32)


def _rms(x, g):
    return x * lax.rsqrt(jnp.mean(x * x, axis=-1, keepdims=True) + NORM_EPS) * g


def _full(shape):
    n = len(shape)
    return pl.BlockSpec(shape, lambda *_: (0,) * n)


def _mod_kernel(c_ref, w_ref, b_ref, o_ref):
    c = c_ref[...]
    sc = c * jax.nn.sigmoid(c)
    o_ref[...] = jnp.dot(sc, w_ref[...], preferred_element_type=F32,
                         precision=lax.Precision.HIGHEST) + b_ref[...]


def _modulation(cvec, w, b):
    R, D = cvec.shape
    N = w.shape[1]
    tn = 1536
    return pl.pallas_call(
        _mod_kernel,
        grid=(N // tn,),
        in_specs=[_full((R, D)),
                  pl.BlockSpec((D, tn), lambda j: (0, j)),
                  pl.BlockSpec((1, tn), lambda j: (0, j))],
        out_specs=pl.BlockSpec((R, tn), lambda j: (0, j)),
        out_shape=jax.ShapeDtypeStruct((R, N), F32),
        compiler_params=_cparams(1),
        name="modulation",
    )(cvec, w, b.reshape(1, N))


def _seg_meansq(x, bd_ref, width, seg):
    x2 = x * x
    hi = x2.astype(BF16)
    lo = (x2 - hi.astype(F32)).astype(BF16)
    bd = bd_ref[0:width, 0:width]
    return (_dot(hi, bd) + _dot(lo, bd)) * (1.0 / seg)


def _rope(x, tab_ref, sh1, sh2):
    return (x * tab_ref[0]
            + pltpu.roll(x, sh1, axis=1) * tab_ref[1]
            + pltpu.roll(x, sh2, axis=1) * tab_ref[2])


def _in_kernel(*refs, rope):
    (x_ref, sh_ref, sc_ref, gpre_ref, win_ref, gq_ref, gk_ref, gcq_ref, gckv_ref,
     wuq_ref, wuk_ref, wuv_ref, ekr_ref, bd_ref) = refs[:14]
    if rope:
        rg_ref, rq_ref, rk_ref = refs[14:17]
        outs = refs[17:]
    else:
        outs = refs[14:]
    h_out, qg_out, qm_out, kd_out, vd_out, km_out, vm_out, u_out = outs

    x = x_ref[...]
    h = _rms(x, gpre_ref[...]) * (1.0 + sc_ref[...]) + sh_ref[...]
    hb = h.astype(BF16)
    h_out[...] = hb
    z = _dot(hb, win_ref[...])

    qg = z[:, C_QG:C_QM]
    qg = qg * lax.rsqrt(_seg_meansq(qg, bd_ref, C_QM - C_QG, GQA_HEAD_DIM) + NORM_EPS) * gq_ref[...]
    for c in range((C_QM - C_QG) // LANES):
        blk = qg[:, c * LANES:(c + 1) * LANES]
        if rope:
            blk = _rope(blk, rg_ref, 96, 32)
        qg_out[:, c * LANES:(c + 1) * LANES] = (blk * (GQA_SCALE * LOG2E)).astype(BF16)

    kd = z[:, C_KD:C_VD]
    kd = kd * lax.rsqrt(_seg_meansq(kd, bd_ref, C_VD - C_KD, GQA_HEAD_DIM) + NORM_EPS) * gk_ref[...]
    for c in range((C_VD - C_KD) // LANES):
        blk = kd[:, c * LANES:(c + 1) * LANES]
        if rope:
            blk = _rope(blk, rg_ref, 96, 32)
        kd_out[:, c * LANES:(c + 1) * LANES] = blk.astype(BF16)
    vd_out[...] = z[:, C_VD:C_CKV].astype(BF16)

    qm = _rms(z[:, C_QM:C_KD], gcq_ref[...]).astype(BF16)
    qm = _dot(qm, wuq_ref[...])
    for c in range(MLA_HEADS):
        blk = qm[:, c * HEAD_SLOT:(c + 1) * HEAD_SLOT]
        if rope:
            blk = _rope(blk, rq_ref, 112, 16)
        qm_out[:, c * HEAD_SLOT:(c + 1) * HEAD_SLOT] = (blk * (MLA_SCALE * LOG2E)).astype(BF16)

    ckv = _rms(z[:, C_CKV:C_U], gckv_ref[...]).astype(BF16)
    kr = z[:, C_KR:C_KR + LANES]
    if rope:
        kr = _rope(kr, rk_ref, 112, 16)
    km = _dot(ckv, wuk_ref[...]) + _dot(kr.astype(BF16), ekr_ref[...])
    km_out[...] = km.astype(BF16)
    vm_out[...] = _dot(ckv, wuv_ref[...]).astype(BF16)

    u_out[...] = z[:, C_U:C_KR]


def _in_proj(x, shift, scale, gpre, lw, tabs, nseg):
    B, L, D = x.shape
    tm = min(512, L // nseg)
    tseg = L // nseg
    nt = tseg // tm
    rope = tabs is not None

    def tok(b, j, i):
        return (b, j * nt + i, 0)

    def tokspec(w):
        return pl.BlockSpec((None, tm, w), tok)

    def tabspec():
        return pl.BlockSpec((3, tm, LANES), lambda b, j, i: (0, j * nt + i, 0))

    vec = lambda w: pl.BlockSpec((None, 1, w), lambda b, j, i: (b, 0, 0))
    in_specs = [tokspec(D), vec(D), vec(D), _full((1, D)), _full((D, N_IN_PACKED)),
                _full((1, 512)), _full((1, 256)), _full((1, MLA_Q_RANK)), _full((1, MLA_KV_RANK)),
                _full((MLA_Q_RANK, 1024)), _full((MLA_KV_RANK, 1024)), _full((MLA_KV_RANK, 512)),
                _full((LANES, 1024)), _full((512, 512))]
    args = [x, shift, scale, gpre, lw["w_in"], lw["g_q"], lw["g_k"], lw["g_cq"], lw["g_ckv"],
            lw["w_uq"], lw["w_uk"], lw["w_uv"], lw["e_kr"], lw["bd64"]]
    if rope:
        in_specs += [tabspec(), tabspec(), tabspec()]
        args += list(tabs)
    widths = [D, 512, 1024, 256, 256, 1024, 512]
    out_specs = [tokspec(w) for w in widths]
    out_shape = [jax.ShapeDtypeStruct((B, L, w), BF16) for w in widths]
    out_specs.append(pl.BlockSpec((None, tm, S5_WIDTH), lambda b, j, i: (b, i, j)))
    out_shape.append(jax.ShapeDtypeStruct((B, tseg, nseg * S5_WIDTH), F32))
    return pl.pallas_call(
        functools.partial(_in_kernel, rope=rope),
        grid=(B, nseg, nt),
        in_specs=in_specs,
        out_specs=out_specs,
        out_shape=out_shape,
        compiler_params=_cparams(3),
        name="in_proj_rope" if rope else "in_proj",
    )(*args)


ONES_ROWS = 16
ATTN_KEY_CHUNK = 512
ATTN_MIN_DENOM = 2.0 ** -60


def _attn_kernel(*refs, gqa, chunks, seg_starts):
    nseg = len(seg_starts)
    q_ref = refs[0]
    k_refs = refs[1:1 + nseg]
    v_refs = refs[1 + nseg:1 + 2 * nseg]
    o_ref, vt_ref, kn_ref, s0_ref, s1_ref, p0_ref, p1_ref = refs[1 + 2 * nseg:]
    s_refs, p_refs = (s0_ref, s1_ref), (p0_ref, p1_ref)
    hd = LANES // 2

    def keys(par, seg):
        return k_refs[seg][...] if gqa else k_refs[seg][:, par * HEAD_SLOT:(par + 1) * HEAD_SLOT]

    @pl.when(pl.program_id(2) == 0)
    def _():
        for v_ref, start in zip(v_refs, seg_starts):
            vt = v_ref[...].astype(F32).T
            ones = jnp.ones((ONES_ROWS, vt.shape[1]), BF16)
            for par in range(2):
                vt_ref[par, 0:hd, start:start + vt.shape[1]] = vt[par * hd:(par + 1) * hd, :].astype(BF16)
                vt_ref[par, hd:hd + ONES_ROWS, start:start + vt.shape[1]] = ones
        for par in range(2):
            big = None
            for seg in range(nseg):
                kf = keys(par, seg).astype(F32)
                n2 = jnp.max(jnp.sum(kf * kf, axis=1, keepdims=True), axis=0, keepdims=True)
                big = n2 if big is None else jnp.maximum(big, n2)
            kn_ref[par] = jnp.broadcast_to(jnp.sqrt(big), kn_ref.shape[1:])

    qt = q_ref[...].astype(F32).T
    qh = []
    for par in range(2):
        if gqa:
            row = lax.broadcasted_iota(jnp.int32, (LANES, 1), 0)
            keep = (row < hd) if par == 0 else (row >= hd)
            qh.append(jnp.where(keep, qt, 0.0).astype(BF16))
        else:
            qh.append(qt[par * HEAD_SLOT:(par + 1) * HEAD_SLOT, :].astype(BF16))

    outs, dens = [], []
    for par in range(2):
        qf = qh[par].astype(F32)
        bound = jnp.sqrt(jnp.sum(qf * qf, axis=0, keepdims=True)) * kn_ref[par, 0:1, 0:1]
        acc = None
        for seg, start in enumerate(seg_starts):
            kh = keys(par, seg)
            pt = jnp.exp2(_dot(kh, qh[par]) - bound).astype(BF16)
            t = _dot(vt_ref[par, :, start:start + kh.shape[0]], pt)
            acc = t if acc is None else acc + t
        outs.append(acc[0:hd, :] / acc[hd:hd + 1, :])
        dens.append(acc[hd:hd + 1, :])
    o_ref[...] = jnp.concatenate(outs, axis=0).T.astype(o_ref.dtype)
    underflow = jnp.min(jnp.minimum(dens[0], dens[1])) < ATTN_MIN_DENOM

    @pl.when(underflow)
    def _():
        _attn_exact(o_ref, k_refs, vt_ref, s_refs, p_refs, qh, gqa=gqa, chunks=chunks, seg_starts=seg_starts)


def _attn_exact(o_ref, k_refs, vt_ref, s_refs, p_refs, qh, *, gqa, chunks, seg_starts):
    hd = LANES // 2

    def scores(par, seg, r0, rows):
        k_ref = k_refs[seg]
        kh = k_ref[r0:r0 + rows, :] if gqa else k_ref[r0:r0 + rows, par * HEAD_SLOT:(par + 1) * HEAD_SLOT]
        st = _dot(kh, qh[par])
        c0 = seg_starts[seg] + r0
        s_refs[par][c0:c0 + rows, :] = st
        return jnp.max(st, axis=0, keepdims=True)

    def probs(par, seg, r0, rows, m):
        c0 = seg_starts[seg] + r0
        p_refs[par][c0:c0 + rows, :] = jnp.exp2(s_refs[par][c0:c0 + rows, :] - m).astype(BF16)

    def weighted(par, seg, r0, rows):
        c0 = seg_starts[seg] + r0
        return _dot(vt_ref[par, :, c0:c0 + rows], p_refs[par][c0:c0 + rows, :])

    def fold(a, b):
        return b if a is None else a + b

    m0 = m1 = acc0 = acc1 = None
    for ch in chunks:
        sm = scores(0, *ch)
        m0 = sm if m0 is None else jnp.maximum(m0, sm)
    for ch in chunks:
        sm = scores(1, *ch)
        m1 = sm if m1 is None else jnp.maximum(m1, sm)
        probs(0, *ch, m0)
    for ch in chunks:
        acc0 = fold(acc0, weighted(0, *ch))
        probs(1, *ch, m1)
    for ch in chunks:
        acc1 = fold(acc1, weighted(1, *ch))
    outs = [a[0:hd, :] / a[hd:hd + 1, :] for a in (acc0, acc1)]
    o_ref[...] = jnp.concatenate(outs, axis=0).T.astype(o_ref.dtype)


def _attention(q, ks, vs, gqa):
    B, Lq, _ = q.shape
    lens = [k.shape[1] for k in ks]
    seg_starts = tuple(int(s) for s in np.cumsum([0] + lens[:-1]))
    Lk = sum(lens)
    tq = min(512, Lq)
    n_pairs = 4
    if gqa:
        q_spec = pl.BlockSpec((None, tq, LANES), lambda b, p, i: (b, i, p))
        k_specs = [pl.BlockSpec((None, n, LANES), lambda b, p, i: (b, 0, p // 2)) for n in lens]
        v_specs = [pl.BlockSpec((None, n, LANES), lambda b, p, i: (b, 0, p // 2)) for n in lens]
    else:
        q_spec = pl.BlockSpec((None, tq, 2 * HEAD_SLOT), lambda b, p, i: (b, i, p))
        k_specs = [pl.BlockSpec((None, n, 2 * HEAD_SLOT), lambda b, p, i: (b, 0, p)) for n in lens]
        v_specs = [pl.BlockSpec((None, n, LANES), lambda b, p, i: (b, 0, p)) for n in lens]
    chunks = tuple((seg, r0, min(ATTN_KEY_CHUNK, n - r0))
                   for seg, n in enumerate(lens) for r0 in range(0, n, ATTN_KEY_CHUNK))
    return pl.pallas_call(
        functools.partial(_attn_kernel, gqa=gqa, chunks=chunks, seg_starts=seg_starts),
        grid=(B, n_pairs, Lq // tq),
        in_specs=[q_spec] + k_specs + v_specs,
        out_specs=pl.BlockSpec((None, tq, LANES), lambda b, p, i: (b, i, p)),
        out_shape=jax.ShapeDtypeStruct((B, Lq, n_pairs * LANES), BF16),
        scratch_shapes=[pltpu.VMEM((2, LANES // 2 + ONES_ROWS, Lk), BF16), pltpu.VMEM((2, 8, LANES), F32),
                        pltpu.VMEM((Lk, tq), F32), pltpu.VMEM((Lk, tq), F32),
                        pltpu.VMEM((Lk, tq), BF16), pltpu.VMEM((Lk, tq), BF16)],
        compiler_params=_cparams(3),
        name="attn_gqa" if gqa else "attn_mla",
    )(q, *ks, *vs)


S5_COLS = 512
assert S5_COLS // S5_STATE * S5_GROUP == LANES


def _s5_kernel(u_ref, bre_ref, bim_ref, cre_ref, cim_ref, a_ref, at_ref, init_ref,
               y_ref, fin_ref, vre, vim, st, *, rev, ti):
    ps = pl.program_id(1)
    ch = pl.program_id(2)
    n = S5_N

    @pl.when((ps == 0) & (ch == 0))
    def _():
        st[...] = jnp.zeros_like(st)

    @pl.when((ps == 1) & (ch == 0))
    def _():
        ends = [(st[0, j:j + 1, :], st[1, j:j + 1, :]) for j in range(S5_SEGS)]
        atr, ati = at_ref[0], at_ref[1]
        cr, ci = init_ref[0], init_ref[1]
        order = range(S5_SEGS - 1, -1, -1) if rev else range(S5_SEGS)
        for j in order:
            st[0, j:j + 1, :] = cr
            st[1, j:j + 1, :] = ci
            er, ei = ends[j]
            cr, ci = atr * cr - ati * ci + er, atr * ci + ati * cr + ei
        fin_ref[0] = cr
        fin_ref[1] = ci

    ub = u_ref[...].reshape(ti * S5_SEGS, S5_WIDTH).astype(BF16)
    for m in range(n // S5_COLS):
        ch = slice(m * LANES, (m + 1) * LANES)
        cols = slice(m * S5_COLS, (m + 1) * S5_COLS)
        vre[:, :, cols] = _dot(ub[:, ch], bre_ref[ch, cols]).reshape(ti, S5_SEGS, S5_COLS)
        vim[:, :, cols] = _dot(ub[:, ch], bim_ref[ch, cols]).reshape(ti, S5_SEGS, S5_COLS)

    def scan(store):
        for cb in range(n // S5_COLS):
            cols = slice(cb * S5_COLS, (cb + 1) * S5_COLS)
            ar = jnp.broadcast_to(a_ref[0, :, cols], (S5_SEGS, S5_COLS))
            ai = jnp.broadcast_to(a_ref[1, :, cols], (S5_SEGS, S5_COLS))

            def body(i, carry):
                xr, xi = carry
                idx = (ti - 1 - i) if rev else i
                nr = ar * xr - ai * xi + vre[idx, :, cols]
                ni = ar * xi + ai * xr + vim[idx, :, cols]
                if store:
                    vre[idx, :, cols] = nr
                    vim[idx, :, cols] = ni
                return nr, ni

            xr, xi = lax.fori_loop(0, ti, body, (st[0, :, cols], st[1, :, cols]), unroll=4)
            st[0, :, cols] = xr
            st[1, :, cols] = xi

    @pl.when(ps == 0)
    def _():
        scan(False)

    @pl.when(ps == 1)
    def _():
        scan(True)
        xr = vre[...].reshape(ti * S5_SEGS, n).astype(BF16)
        xi = vim[...].reshape(ti * S5_SEGS, n).astype(BF16)
        y = _dot(xr, cre_ref[...]) + _dot(xi, cim_ref[...])
        y_ref[...] = y.reshape(ti, S5_SEGS, S5_WIDTH)


def _s5_scan(u, sw, init, rev):
    B, T, _, _ = u.shape
    ti = min(64, T)
    nch = T // ti
    n = S5_N

    def chunk(c):
        return (nch - 1 - c) if rev else c

    d = 1 if rev else 0
    return pl.pallas_call(
        functools.partial(_s5_kernel, rev=rev, ti=ti),
        grid=(B, 2, nch),
        in_specs=[pl.BlockSpec((None, ti, S5_SEGS, S5_WIDTH), lambda b, p, c: (b, chunk(c), 0, 0)),
                  _full((S5_WIDTH, n)), _full((S5_WIDTH, n)), _full((n, S5_WIDTH)), _full((n, S5_WIDTH)),
                  _full((2, 1, n)), _full((2, 1, n)),
                  pl.BlockSpec((None, 2, 1, n), lambda b, p, c: (b, 0, 0, 0))],
        out_specs=[pl.BlockSpec((None, ti, S5_SEGS, S5_WIDTH),
                                lambda b, p, c: (b, chunk(c * p), 0, 0)),
                   pl.BlockSpec((None, 2, 1, n), lambda b, p, c: (b, 0, 0, 0))],
        out_shape=[jax.ShapeDtypeStruct(u.shape, F32), jax.ShapeDtypeStruct((B, 2, 1, n), F32)],
        scratch_shapes=[pltpu.VMEM((ti, S5_SEGS, n), F32), pltpu.VMEM((ti, S5_SEGS, n), F32),
                        pltpu.VMEM((2, S5_SEGS, n), F32)],
        compiler_params=_cparams(3),
        name="s5_bwd" if rev else "s5_fwd",
    )(u, sw["b_re"][d], sw["b_im"][d], sw["c_re"][d], sw["c_im_neg"][d],
      sw["a"][d], sw["a_t"][(d, T)], init)


def _glu_kernel(u_ref, yf_ref, yb_ref, d_ref, w_ref, o_ref):
    y = d_ref[...] * u_ref[...] + yf_ref[...] + yb_ref[...]
    g = 0.5 * y * (1.0 + jnp.tanh(math.sqrt(2.0 / math.pi) * (y + 0.044715 * (y * y * y))))
    t = _dot(g.astype(BF16), w_ref[...])
    o_ref[...] = (t[:, 0:S5_WIDTH] * jax.nn.sigmoid(t[:, S5_WIDTH:2 * S5_WIDTH])).astype(o_ref.dtype)


def _s5_glu(u, yf, yb, dskip, wglu, nseg):
    B, T, _ = u.shape
    tm = min(256, T)
    nt = T // tm
    seg = pl.BlockSpec((None, tm, S5_WIDTH), lambda b, j, i: (b, i, j))
    return pl.pallas_call(
        _glu_kernel,
        grid=(B, nseg, nt),
        in_specs=[seg, seg, seg, _full((1, S5_WIDTH)), _full((S5_WIDTH, 2 * S5_WIDTH))],
        out_specs=pl.BlockSpec((None, tm, S5_WIDTH), lambda b, j, i: (b, j * nt + i, 0)),
        out_shape=jax.ShapeDtypeStruct((B, T * nseg, S5_WIDTH), BF16),
        compiler_params=_cparams(3),
        name="s5_glu",
    )(u, yf, yb, dskip, wglu)


def _merge_kernel(h_ref, ya_ref, yb_ref, yc_ref, x_ref, gate_ref, wmg_ref, bmg_ref, wbr_ref, wout_ref,
                  gpost_ref, o_ref):
    h = h_ref[...]
    d = x_ref.shape[-1]
    acc = None
    for n, y_ref in enumerate((ya_ref, yb_ref, yc_ref)):
        g = jax.nn.sigmoid(_dot(h, wmg_ref[:, n * d:(n + 1) * d]) + bmg_ref[:, n * d:(n + 1) * d])
        t = g * _dot(y_ref[...], wbr_ref[n])
        acc = t if acc is None else acc + t
    y = _dot(acc.astype(BF16), wout_ref[...])
    o_ref[...] = x_ref[...] + gate_ref[...] * _rms(y, gpost_ref[...])


def _merge(h, ya, yb, yc, x, gate, lw):
    B, L, D = x.shape
    tm = min(512, L)
    tok = lambda w: pl.BlockSpec((None, tm, w), lambda b, i: (b, i, 0))
    return pl.pallas_call(
        _merge_kernel,
        grid=(B, L // tm),
        in_specs=[tok(D), tok(512), tok(512), tok(512), tok(D),
                  pl.BlockSpec((None, 1, D), lambda b, i: (b, 0, 0)),
                  _full((D, N_BRANCHES * D)), _full((1, N_BRANCHES * D)), _full((N_BRANCHES, 512, D)),
                  _full((D, D)), _full((1, D))],
        out_specs=tok(D),
        out_shape=jax.ShapeDtypeStruct((B, L, D), F32),
        compiler_params=_cparams(2),
        name="merge",
    )(h, ya, yb, yc, x, gate, lw["w_mg"], lw["b_mg"], lw["w_branch"], lw["w_out"], lw["g_mix_post"])


def _ffn_kernel(x_ref, sh_ref, sc_ref, gate_ref, gpre_ref, gpost_ref, w1_ref, w3_ref, w2_ref, o_ref,
                h_s, acc_s):
    f = pl.program_id(2)

    @pl.when(f == 0)
    def _():
        h = _rms(x_ref[...], gpre_ref[...]) * (1.0 + sc_ref[...]) + sh_ref[...]
        h_s[...] = h.astype(BF16)
        acc_s[...] = jnp.zeros_like(acc_s)

    h = h_s[...]
    a = _dot(h, w1_ref[...])
    g = (a * jax.nn.sigmoid(a)) * _dot(h, w3_ref[...])
    acc_s[...] += _dot(g.astype(BF16), w2_ref[...])

    @pl.when(f == pl.num_programs(2) - 1)
    def _():
        o_ref[...] = x_ref[...] + gate_ref[...] * _rms(acc_s[...], gpost_ref[...])


def _ffn(x, shift, scale, gate, gpre, gpost, w1, w3, w2):
    B, L, D = x.shape
    dff = w1.shape[1]
    tm = min(512, L)
    tf = dff // 2 if (dff // 2) % LANES == 0 else dff
    tok = pl.BlockSpec((None, tm, D), lambda b, i, f: (b, i, 0))
    vec = pl.BlockSpec((None, 1, D), lambda b, i, f: (b, 0, 0))
    return pl.pallas_call(
        _ffn_kernel,
        grid=(B, L // tm, dff // tf),
        in_specs=[tok, vec, vec, vec, _full((1, D)), _full((1, D)),
                  pl.BlockSpec((D, tf), lambda b, i, f: (0, f)),
                  pl.BlockSpec((D, tf), lambda b, i, f: (0, f)),
                  pl.BlockSpec((tf, D), lambda b, i, f: (f, 0))],
        out_specs=tok,
        out_shape=jax.ShapeDtypeStruct((B, L, D), F32),
        scratch_shapes=[pltpu.VMEM((tm, D), BF16), pltpu.VMEM((tm, D), F32)],
        compiler_params=_cparams(3),
        name="ffn",
    )(x, shift, scale, gate, gpre, gpost, w1, w3, w2)


MOE_BLOCK = 1024
MOE_CHUNK = 64
MOE_TILE = 512
MOE_SORT_ROWS = 512


def _lane_col(a, lane, k):
    return jnp.sum(jnp.where(lane == k, a, 0.0), axis=1, keepdims=True)


def _router_kernel(x_ref, sh_ref, sc_ref, gpre_ref, rw_ref, rb_ref, h_out, route_out, cnt_out):
    lane = lax.broadcasted_iota(jnp.int32, (1, LANES), 1)
    h = _rms(x_ref[...], gpre_ref[...]) * (1.0 + sc_ref[...]) + sh_ref[...]
    h_out[...] = h.astype(BF16)
    logits = jnp.dot(h, rw_ref[...], preferred_element_type=F32,
                     precision=lax.Precision.HIGHEST) + rb_ref[...]
    m1 = jnp.max(logits, axis=1, keepdims=True)
    i1 = jnp.min(jnp.where(logits == m1, lane, LANES), axis=1, keepdims=True)
    rest = jnp.where(lane == i1, -jnp.inf, logits)
    m2 = jnp.max(rest, axis=1, keepdims=True)
    i2 = jnp.min(jnp.where(rest == m2, lane, LANES), axis=1, keepdims=True)
    e2 = jnp.exp(m2 - m1)
    den = 1.0 + e2
    route_out[...] = (jnp.where(lane == 0, i1.astype(F32), 0.0) + jnp.where(lane == 1, i2.astype(F32), 0.0)
                      + jnp.where(lane == 2, 1.0 / den, 0.0) + jnp.where(lane == 3, e2 / den, 0.0))
    member = jnp.where((lane == i1) | (lane == i2), 1.0, 0.0)
    cnt_out[...] = jnp.sum(member, axis=0, keepdims=True)


def _bf16_pieces(w):
    hi = w.astype(BF16).astype(F32)
    r1 = w - hi
    lo = r1.astype(BF16).astype(F32)
    return hi, lo, r1 - lo


def _dispatch_kernel(h_ref, route_ref, ltri_ref, ustr_ref, xs_out, pos_out):
    d = h_ref.shape[1]
    lane = lax.broadcasted_iota(jnp.int32, (1, LANES), 1)
    lanef = lane.astype(F32)
    route = route_ref[...]
    i1, i2 = _lane_col(route, lane, 0), _lane_col(route, lane, 1)
    g1, g2 = _lane_col(route, lane, 2), _lane_col(route, lane, 3)
    m1, m2 = lanef == i1, lanef == i2
    member = jnp.where(m1 | m2, 1.0, 0.0)
    before = _dot(ltri_ref[...], member.astype(BF16))
    counts = jnp.sum(member, axis=0, keepdims=True)
    padded = jnp.floor((counts + (MOE_CHUNK - 1.0)) * (1.0 / MOE_CHUNK)) * MOE_CHUNK
    seg = _dot(jnp.broadcast_to(padded, (8, LANES)).astype(BF16), ustr_ref[...])[0:1, :]
    slot = before + seg
    pos1 = jnp.sum(jnp.where(m1, slot, 0.0), axis=1, keepdims=True)
    pos2 = jnp.sum(jnp.where(m2, slot, 0.0), axis=1, keepdims=True)
    pos = jnp.where(lane == 0, pos1, 0.0) + jnp.where(lane == 1, pos2, 0.0)
    pos_out[...] = pos
    post = pos.T
    row1, row2 = post[0:1, :], post[1:2, :]

    def gate_cols(g):
        a, b, c = _bf16_pieces(g)
        return (jnp.where(lane == 0, a, 0.0) + jnp.where(lane == 1, b, 0.0)
                + jnp.where(lane == 2, c, 0.0)).astype(BF16)

    gc1, gc2 = gate_cols(g1), gate_cols(g2)
    h = h_ref[...]
    for c in range(xs_out.shape[0] // MOE_SORT_ROWS):
        rows = (lax.broadcasted_iota(jnp.int32, (MOE_SORT_ROWS, 1), 0) + c * MOE_SORT_ROWS).astype(F32)
        q1, q2 = rows == row1, rows == row2
        sl = slice(c * MOE_SORT_ROWS, (c + 1) * MOE_SORT_ROWS)
        xs_out[sl, 0:d] = _dot(jnp.where(q1 | q2, 1.0, 0.0).astype(BF16), h).astype(BF16)
        gates = (_dot(jnp.where(q1, 1.0, 0.0).astype(BF16), gc1)
                 + _dot(jnp.where(q2, 1.0, 0.0).astype(BF16), gc2))
        xs_out[sl, d:d + LANES] = gates.astype(BF16)


def _expert_kernel(*refs, n_in):
    nu_ref = refs[1]
    x_refs = refs[3:3 + n_in]
    w1_ref, w3_ref, w2_ref, o_ref, x_s, acc_s = refs[3 + n_in:]
    i = pl.program_id(0)
    f = pl.program_id(1)
    d = o_ref.shape[1]

    @pl.when(i < nu_ref[0])
    def _():
        @pl.when(f == 0)
        def _():
            for k, x_ref in enumerate(x_refs):
                x_s[k * MOE_CHUNK:(k + 1) * MOE_CHUNK, :] = x_ref[...]
            acc_s[...] = jnp.zeros_like(acc_s)

        x = x_s[:, 0:d]
        a = _dot(x, w1_ref[...])
        g = (a * jax.nn.sigmoid(a)) * _dot(x, w3_ref[...])
        acc_s[...] += _dot(g.astype(BF16), w2_ref[...])

        @pl.when(f == pl.num_programs(1) - 1)
        def _():
            gate = jnp.sum(x_s[:, d:d + LANES].astype(F32), axis=1, keepdims=True)
            o_ref[...] = (acc_s[...] * gate).astype(o_ref.dtype)

    @pl.when(i >= nu_ref[0])
    def _():
        o_ref[...] = jnp.zeros_like(o_ref)


def _combine_kernel(*refs, rc, n_in):
    ys_refs = refs[1:1 + n_in]
    pos_ref, x_ref, gate_ref, gpost_ref, o_ref, ys_s = refs[1 + n_in:]
    g = pl.program_id(1)
    for k, ys_ref in enumerate(ys_refs):
        row = pl.multiple_of((g * n_in + k) * MOE_CHUNK, MOE_CHUNK)
        ys_s[pl.ds(row, MOE_CHUNK), :] = ys_ref[...]

    @pl.when(g == rc // n_in - 1)
    def _():
        lane = lax.broadcasted_iota(jnp.int32, (1, LANES), 1)
        pos = pos_ref[...]
        pos1, pos2 = _lane_col(pos, lane, 0), _lane_col(pos, lane, 1)
        acc = None
        for c in range(rc * MOE_CHUNK // MOE_SORT_ROWS):
            cols = (lax.broadcasted_iota(jnp.int32, (1, MOE_SORT_ROWS), 1) + c * MOE_SORT_ROWS).astype(F32)
            sel = jnp.where((cols == pos1) | (cols == pos2), 1.0, 0.0).astype(BF16)
            t = _dot(sel, ys_s[c * MOE_SORT_ROWS:(c + 1) * MOE_SORT_ROWS, :])
            acc = t if acc is None else acc + t
        o_ref[...] = x_ref[...] + gate_ref[...] * _rms(acc, gpost_ref[...])


def _moe_tables(counts, rc, nt):
    ne = counts.shape[1]
    per_tile = MOE_TILE // MOE_CHUNK
    cch = (counts + MOE_CHUNK - 1) // MOE_CHUNK
    tiles_e = (jnp.sum(cch, axis=0) + per_tile - 1) // per_tile
    tile_end = jnp.cumsum(tiles_e)
    n_used = tile_end[-1]
    chunk_start = (tile_end - tiles_e)[None, :] * per_tile + (jnp.cumsum(cch, axis=0) - cch)
    seg_end = jnp.cumsum(cch, axis=1)
    r = jnp.arange(rc)
    e_r = jnp.minimum(jnp.sum(r[None, :, None] >= seg_end[:, None, :], axis=-1), ne - 1)
    onehot = (e_r[:, :, None] == jnp.arange(ne)[None, None, :]).astype(jnp.int32)
    pick = lambda t: jnp.sum(onehot * t[:, None, :], axis=-1)
    dest = pick(chunk_start) + r[None, :] - pick(seg_end - cch)
    n_valid = seg_end[:, -1]
    last = jnp.sum(jnp.where(r[None, :] == n_valid[:, None] - 1, dest, 0), axis=1, keepdims=True)
    dest = jnp.where(r[None, :] < n_valid[:, None], dest, last)
    t = jnp.maximum(jnp.minimum(jnp.arange(nt), n_used - 1), 0)
    tile_e = jnp.minimum(jnp.sum(t[:, None] >= tile_end[None, :], axis=-1), ne - 1)
    e_hot = (tile_e[:, None] == jnp.arange(ne)[None, :]).astype(jnp.int32)
    of_e = lambda a: jnp.sum(e_hot * a[None, :], axis=-1)
    of_eb = lambda a: jnp.sum(e_hot[:, None, :] * a[None, :, :], axis=-1)
    q = (t - of_e(tile_end - tiles_e))[:, None] * per_tile + jnp.arange(per_tile)[None, :]
    q = jnp.where(q < of_e(jnp.sum(cch, axis=0))[:, None], q, q[:, :1])
    blk_end = of_eb(jnp.cumsum(cch, axis=0))
    b_q = jnp.minimum(jnp.sum(q[:, :, None] >= blk_end[:, None, :], axis=-1), cch.shape[0] - 1)
    b_hot = (b_q[:, :, None] == jnp.arange(cch.shape[0])[None, None, :]).astype(jnp.int32)
    of_b = lambda a: jnp.sum(b_hot * a[:, None, :], axis=-1)
    xsrc = b_q * rc + of_b(of_eb(seg_end - cch)) + q - of_b(blk_end - of_eb(cch))
    i32 = lambda a: a.reshape(-1).astype(jnp.int32)
    return i32(dest), i32(xsrc), i32(tile_e), i32(n_used)


def _moe(x, shift, scale, gate, gpre, gpost, rw, rb, w1, w3, w2):
    B, L, D = x.shape
    ne, _, dff = w1.shape
    n = B * L
    t = min(MOE_BLOCK, L)
    nblk, per_batch = n // t, L // t
    rc = (2 * t + ne * MOE_CHUNK) // MOE_CHUNK
    per_tile = MOE_TILE // MOE_CHUNK
    nt = (2 * n // MOE_CHUNK + nblk * ne + per_tile - 1) // per_tile + ne
    tf = dff // 2 if (dff // 2) % LANES == 0 else dff
    nf = dff // tf
    xf = x.reshape(n, D)
    wide = D + LANES

    blk = lambda w: pl.BlockSpec((t, w), lambda i: (i, 0))
    vec1 = pl.BlockSpec((None, 1, D), lambda i: (i // per_batch, 0, 0))
    h, route, cnt = pl.pallas_call(
        _router_kernel,
        grid=(nblk,),
        in_specs=[blk(D), vec1, vec1, _full((1, D)), _full((D, LANES)), _full((1, LANES))],
        out_specs=[blk(D), blk(LANES), pl.BlockSpec((None, 1, LANES), lambda i: (i, 0, 0))],
        out_shape=[jax.ShapeDtypeStruct((n, D), BF16), jax.ShapeDtypeStruct((n, LANES), F32),
                   jax.ShapeDtypeStruct((nblk, 1, LANES), F32)],
        compiler_params=_cparams(1),
        name="moe_router",
    )(xf, shift, scale, gpre, rw, rb)

    counts = cnt[:, 0, :ne].astype(jnp.int32)
    dest, xsrc, tile_e, n_used = _moe_tables(counts, rc, nt)

    ltri = jnp.asarray(np.tril(np.ones((t, t), np.float32), -1), BF16)
    ustr = jnp.asarray(np.triu(np.ones((LANES, LANES), np.float32), 1), BF16)
    xs, pos = pl.pallas_call(
        _dispatch_kernel,
        grid=(nblk,),
        in_specs=[blk(D), blk(LANES), _full((t, t)), _full((LANES, LANES))],
        out_specs=[pl.BlockSpec((rc * MOE_CHUNK, wide), lambda b: (b, 0)), blk(LANES)],
        out_shape=[jax.ShapeDtypeStruct((nblk * rc * MOE_CHUNK, wide), BF16),
                   jax.ShapeDtypeStruct((n, LANES), F32)],
        compiler_params=_cparams(1),
        name="moe_dispatch",
    )(h, route, ltri, ustr)

    def tile(i, nu):
        return jnp.maximum(jnp.minimum(i, nu[0] - 1), 0)

    def fcol(i, f, nu):
        return jnp.where(i < nu[0], f, nf - 1)

    def chunk_in(k):
        return pl.BlockSpec((MOE_CHUNK, wide), lambda i, f, te, nu, src: (src[tile(i, nu) * per_tile + k], 0))

    ys = pl.pallas_call(
        functools.partial(_expert_kernel, n_in=per_tile),
        grid_spec=pltpu.PrefetchScalarGridSpec(
            num_scalar_prefetch=3,
            grid=(nt, nf),
            in_specs=[chunk_in(k) for k in range(per_tile)] + [
                pl.BlockSpec((None, D, tf), lambda i, f, te, nu, src: (te[i], 0, fcol(i, f, nu))),
                pl.BlockSpec((None, D, tf), lambda i, f, te, nu, src: (te[i], 0, fcol(i, f, nu))),
                pl.BlockSpec((None, tf, D), lambda i, f, te, nu, src: (te[i], fcol(i, f, nu), 0))],
            out_specs=pl.BlockSpec((MOE_TILE, D), lambda i, f, te, nu, src: (i, 0)),
            scratch_shapes=[pltpu.VMEM((MOE_TILE, wide), BF16), pltpu.VMEM((MOE_TILE, D), F32)]),
        out_shape=jax.ShapeDtypeStruct((nt * MOE_TILE, D), BF16),
        compiler_params=_cparams(2),
        name="moe_experts",
    )(tile_e, n_used, xsrc, *([xs] * per_tile), w1, w3, w2)

    def chunk_out(k):
        return pl.BlockSpec((MOE_CHUNK, D), lambda b, g, src: (src[b * rc + g * per_tile + k], 0))

    out = pl.pallas_call(
        functools.partial(_combine_kernel, rc=rc, n_in=per_tile),
        grid_spec=pltpu.PrefetchScalarGridSpec(
            num_scalar_prefetch=1,
            grid=(nblk, rc // per_tile),
            in_specs=[chunk_out(k) for k in range(per_tile)] + [
                pl.BlockSpec((t, LANES), lambda b, g, src: (b, 0)),
                pl.BlockSpec((t, D), lambda b, g, src: (b, 0)),
                pl.BlockSpec((None, 1, D), lambda b, g, src: (b // per_batch, 0, 0)),
                pl.BlockSpec((1, D), lambda b, g, src: (0, 0))],
            out_specs=pl.BlockSpec((t, D), lambda b, g, src: (b, 0)),
            scratch_shapes=[pltpu.VMEM((rc * MOE_CHUNK, D), BF16)]),
        out_shape=jax.ShapeDtypeStruct((n, D), F32),
        compiler_params=_cparams(2),
        name="moe_combine",
    )(dest, *([ys] * per_tile), pos, xf, gate, gpost)
    return out.reshape(B, L, D)


def _rope_tables(n_lat):
    pos = jnp.arange(n_lat, dtype=jnp.int32)
    row = (pos // GRID_W).astype(F32)
    col = (pos % GRID_W).astype(F32)

    def cos_sin(rot_dim):
        axis_dim = rot_dim // 2
        inv_freq = ROPE_THETA ** (-jnp.arange(0, axis_dim, 2, dtype=F32) / axis_dim)
        ang = jnp.concatenate([row[:, None] * inv_freq, col[:, None] * inv_freq], axis=-1)
        return jnp.cos(ang), jnp.sin(ang)

    def table(cos, sin, start, stop):
        reps = (stop - start) // (2 * cos.shape[1])
        zero = jnp.zeros_like(sin)

        def lanes(first, second, fill):
            body = jnp.tile(jnp.concatenate([first, second], axis=1), (1, reps))
            return jnp.pad(body, ((0, 0), (start, LANES - stop)), constant_values=fill)

        return jnp.stack([lanes(cos, cos, 1.0), lanes(-sin, zero, 0.0), lanes(zero, sin, 0.0)]).astype(F32)

    gc, gs = cos_sin(GQA_HEAD_DIM)
    mc, ms = cos_sin(MLA_ROPE_DIM)
    t_gqa = table(gc, gs, 0, LANES)
    t_mq = table(mc, ms, MLA_NOPE_DIM, MLA_NOPE_DIM + MLA_ROPE_DIM)
    t_kr = table(mc, ms, 0, MLA_ROPE_DIM)
    return t_gqa, t_mq, t_kr


def _pack_layer(p, l):
    D = p["w_in"].shape[1]
    w = p["w_in"][l]
    offs = np.cumsum([0, 512, MLA_Q_RANK, 128, 128, MLA_KV_RANK, MLA_ROPE_DIM, S5_WIDTH])
    qg, qm, kg, vg, ckv, kr, u = [w[:, offs[i]:offs[i + 1]] for i in range(7)]
    dup = lambda t: jnp.concatenate([t[:, 0:64], t[:, 0:64], t[:, 64:128], t[:, 64:128]], axis=1)
    w_in = jnp.concatenate([qg, qm, dup(kg), dup(vg), ckv, u, kr,
                            jnp.zeros((D, LANES - MLA_ROPE_DIM), F32)], axis=1).astype(BF16)

    wq = p["w_uq"][l].reshape(MLA_Q_RANK, MLA_HEADS, MLA_NOPE_DIM + MLA_ROPE_DIM)
    w_uq = jnp.pad(wq, ((0, 0), (0, 0), (0, HEAD_SLOT - wq.shape[-1]))).reshape(MLA_Q_RANK, -1).astype(BF16)
    wkv = p["w_ukv"][l].reshape(MLA_KV_RANK, MLA_HEADS, MLA_NOPE_DIM + MLA_V_DIM)
    w_uk = jnp.pad(wkv[:, :, :MLA_NOPE_DIM], ((0, 0), (0, 0), (0, HEAD_SLOT - MLA_NOPE_DIM)))
    w_uk = w_uk.reshape(MLA_KV_RANK, -1).astype(BF16)
    w_uv = wkv[:, :, MLA_NOPE_DIM:].reshape(MLA_KV_RANK, -1).astype(BF16)
    e = np.zeros((LANES, MLA_HEADS, HEAD_SLOT), np.float32)
    for r in range(MLA_ROPE_DIM):
        e[r, :, MLA_NOPE_DIM + r] = 1.0
    bd = np.kron(np.eye(512 // GQA_HEAD_DIM, dtype=np.float32), np.ones((GQA_HEAD_DIM, GQA_HEAD_DIM), np.float32))
    return {
        "w_in": w_in,
        "g_q": jnp.tile(p["g_q"][l], GQA_HEADS).reshape(1, -1),
        "g_k": jnp.tile(p["g_k"][l], 2 * GQA_KV_HEADS).reshape(1, -1),
        "g_cq": p["g_cq"][l].reshape(1, -1),
        "g_ckv": p["g_ckv"][l].reshape(1, -1),
        "w_uq": w_uq, "w_uk": w_uk, "w_uv": w_uv,
        "e_kr": jnp.asarray(e.reshape(LANES, -1), BF16),
        "bd64": jnp.asarray(bd, BF16),
        "w_mg": p["w_merge_gate"][l].astype(BF16),
        "b_mg": p["b_merge_gate"][l].reshape(1, -1),
        "w_branch": p["w_branch"][l].astype(BF16),
        "w_out": p["w_out"][l].astype(BF16),
        "g_mix_pre": p["g_mix_pre"][l].reshape(1, -1),
        "g_mix_post": p["g_mix_post"][l].reshape(1, -1),
        "g_ffn_pre": p["g_ffn_pre"][l].reshape(1, -1),
        "g_ffn_post": p["g_ffn_post"][l].reshape(1, -1),
        "w_glu": p["w_glu"][l].astype(BF16),
        "s5_d": p["s5_d"][l].reshape(1, -1),
    }


def _cpow(ar, ai, n):
    rr, ri = None, None
    br, bi = ar, ai
    while n:
        if n & 1:
            rr, ri = (br, bi) if rr is None else (rr * br - ri * bi, rr * bi + ri * br)
        n >>= 1
        if n:
            br, bi = br * br - bi * bi, 2.0 * br * bi
    return rr, ri


def _s5_weights(p, l, seg_lens):
    G, P, H = S5_GROUPS, S5_STATE, S5_GROUP
    eye = jnp.eye(G, dtype=F32)
    out = {"b_re": [], "b_im": [], "c_re": [], "c_im_neg": [], "a": [], "a_t": {}}
    for d in range(2):
        lr = jnp.minimum(p["s5_lambda_re"][l, d], -1e-4)
        li = p["s5_lambda_im"][l, d]
        dt = jnp.exp(p["s5_log_dt"][l, d])[:, None]
        mag = jnp.exp(lr * dt)
        abr, abi = mag * jnp.cos(li * dt), mag * jnp.sin(li * dt)
        den = lr * lr + li * li
        fr = ((abr - 1.0) * lr + abi * li) / den
        fi = (abi * lr - (abr - 1.0) * li) / den
        br, bi = p["s5_b_re"][l, d], p["s5_b_im"][l, d]
        bbr = fr[..., None] * br - fi[..., None] * bi
        bbi = fr[..., None] * bi + fi[..., None] * br
        blk_b = lambda t: jnp.einsum("gph,gk->ghkp", t, eye).reshape(G * H, G * P).astype(BF16)
        blk_c = lambda t: jnp.einsum("ghp,gk->gpkh", t, eye).reshape(G * P, G * H).astype(BF16)
        out["b_re"].append(blk_b(bbr))
        out["b_im"].append(blk_b(bbi))
        out["c_re"].append(blk_c(p["s5_c_re"][l, d]))
        out["c_im_neg"].append(blk_c(-p["s5_c_im"][l, d]))
        out["a"].append(jnp.stack([abr.reshape(1, -1), abi.reshape(1, -1)]))
        for t in seg_lens:
            tr, ti = _cpow(abr, abi, t)
            out["a_t"][(d, t)] = jnp.stack([tr.reshape(1, -1), ti.reshape(1, -1)])
    return out


def _seg_view(t):
    B, T, _ = t.shape
    return t.reshape(B, T, S5_SEGS, S5_WIDTH)


def _to_segments(t):
    B, L, W = t.shape
    return t.reshape(B, S5_SEGS, L // S5_SEGS, W).transpose(0, 2, 1, 3).reshape(B, L // S5_SEGS, S5_SEGS * W)


def kernel(x, c, ctx, c_ctx, w_mod, b_mod, g_mix_pre, g_mix_post, g_ffn_pre, g_ffn_post, w_in, g_q, g_k, g_cq, g_ckv, w_uq, w_ukv, s5_lambda_re, s5_lambda_im, s5_log_dt, s5_b_re, s5_b_im, s5_c_re, s5_c_im, s5_d, w_glu, w_branch, w_merge_gate, b_merge_gate, w_out, ffn_w1, ffn_w3, ffn_w2, router_w, router_b, moe_w1, moe_w3, moe_w2):
    p = dict(w_in=w_in, g_q=g_q, g_k=g_k, g_cq=g_cq, g_ckv=g_ckv, w_uq=w_uq, w_ukv=w_ukv,
             s5_lambda_re=s5_lambda_re, s5_lambda_im=s5_lambda_im, s5_log_dt=s5_log_dt,
             s5_b_re=s5_b_re, s5_b_im=s5_b_im, s5_c_re=s5_c_re, s5_c_im=s5_c_im, s5_d=s5_d, w_glu=w_glu,
             w_branch=w_branch, w_merge_gate=w_merge_gate, b_merge_gate=b_merge_gate, w_out=w_out,
             g_mix_pre=g_mix_pre, g_mix_post=g_mix_post, g_ffn_pre=g_ffn_pre, g_ffn_post=g_ffn_post)
    B, L, D = x.shape
    Lc = ctx.shape[1]
    depth = w_mod.shape[0]
    tabs = _rope_tables(L)
    t_lat, t_ctx = L // S5_SEGS, Lc // S5_SEGS

    cvec = jnp.concatenate([c, c_ctx[None, :], jnp.zeros((8 - B - 1, D), F32)], axis=0)
    lat, cx = x, ctx
    for layer in range(depth):
        last = layer == depth - 1
        lw = _pack_layer(p, layer)
        sw = _s5_weights(p, layer, (t_lat, t_ctx))
        mod = _modulation(cvec, w_mod[layer], b_mod[layer])
        mod_l = [mod[:B, i * D:(i + 1) * D].reshape(B, 1, D) for i in range(6)]
        mod_c = [jnp.broadcast_to(mod[B, i * D:(i + 1) * D], (B, 1, D)) for i in range(6)]

        h_l, qg_l, qm_l, kd_l, vd_l, km_l, vm_l, u_l = _in_proj(
            lat, mod_l[0], mod_l[1], lw["g_mix_pre"], lw, tabs, S5_SEGS)
        h_c, qg_c, qm_c, kd_c, vd_c, km_c, vm_c, u_c = _in_proj(
            cx, mod_c[0], mod_c[1], lw["g_mix_pre"], lw, None, 1)
        ya_l = _attention(qg_l, (kd_c, kd_l), (vd_c, vd_l), gqa=True)
        yb_l = _attention(qm_l, (km_c, km_l), (vm_c, vm_l), gqa=False)

        u_cs = _to_segments(u_c)
        zero = jnp.zeros((B, 2, 1, S5_N), F32)
        yf_c, fin_f = _s5_scan(_seg_view(u_cs), sw, zero, rev=False)
        yb_c, fin_b = _s5_scan(_seg_view(u_cs), sw, zero, rev=True)
        yf_l, _ = _s5_scan(_seg_view(u_l), sw, fin_f, rev=False)
        yr_l, _ = _s5_scan(_seg_view(u_l), sw, fin_b, rev=True)
        flat = lambda t: t.reshape(t.shape[0], t.shape[1], S5_SEGS * S5_WIDTH)
        yc_l = _s5_glu(u_l, flat(yf_l), flat(yr_l), lw["s5_d"], lw["w_glu"], S5_SEGS)

        lat = _merge(h_l, ya_l, yb_l, yc_l, lat, mod_l[2], lw)
        if not last:
            ya_c = _attention(qg_c, (kd_c,), (vd_c,), gqa=True)
            yb_c2 = _attention(qm_c, (km_c,), (vm_c,), gqa=False)
            yc_c = _s5_glu(u_cs, flat(yf_c), flat(yb_c), lw["s5_d"], lw["w_glu"], S5_SEGS)
            cx = _merge(h_c, ya_c, yb_c2, yc_c, cx, mod_c[2], lw)

        i = layer // 2
        if layer % 2 == 0:
            w1, w3, w2 = ffn_w1[i].astype(BF16), ffn_w3[i].astype(BF16), ffn_w2[i].astype(BF16)
            mix = lambda t, m: _ffn(t, m[3], m[4], m[5], lw["g_ffn_pre"], lw["g_ffn_post"], w1, w3, w2)
        else:
            w1, w3, w2 = moe_w1[i].astype(BF16), moe_w3[i].astype(BF16), moe_w2[i].astype(BF16)
            rw = jnp.pad(router_w[i], ((0, 0), (0, LANES - N_EXPERTS)))
            rb = jnp.concatenate([router_b[i], jnp.full((LANES - N_EXPERTS,), -1e30, F32)]).reshape(1, LANES)
            mix = lambda t, m: _moe(t, m[3], m[4], m[5], lw["g_ffn_pre"], lw["g_ffn_post"], rw, rb, w1, w3, w2)
        lat = mix(lat, mod_l)
        if not last:
            cx = mix(cx, mod_c)
    return lat
```

```python
import functools
import math

import jax
import jax.numpy as jnp
import numpy as np
from jax import lax
from jax.experimental import pallas as pl
from jax.experimental.pallas import tpu as pltpu

F32 = jnp.float32
BF16 = jnp.bfloat16

GRID_W = 64
ROPE_THETA = 10000.0
NORM_EPS = 1e-6

GQA_HEADS = 8
GQA_KV_HEADS = 2
GQA_HEAD_DIM = 64
GQA_SCALE = 1.0 / math.sqrt(GQA_HEAD_DIM)

MLA_HEADS = 8
MLA_NOPE_DIM = 64
MLA_ROPE_DIM = 32
MLA_V_DIM = 64
MLA_Q_RANK = 384
MLA_KV_RANK = 256
MLA_SCALE = 1.0 / math.sqrt(MLA_NOPE_DIM + MLA_ROPE_DIM)
LOG2E = math.log2(math.e)

S5_WIDTH = 512
S5_GROUP = 16
S5_GROUPS = S5_WIDTH // S5_GROUP
S5_STATE = 64
S5_N = S5_GROUPS * S5_STATE
S5_SEGS = 8

N_BRANCHES = 3
N_EXPERTS = 8

LANES = 128
HEAD_SLOT = 128

C_QG = 0
C_QM = C_QG + GQA_HEADS * GQA_HEAD_DIM
C_KD = C_QM + MLA_Q_RANK
C_VD = C_KD + 2 * GQA_KV_HEADS * GQA_HEAD_DIM
C_CKV = C_VD + 2 * GQA_KV_HEADS * GQA_HEAD_DIM
C_U = C_CKV + MLA_KV_RANK
C_KR = C_U + S5_WIDTH
N_IN_PACKED = C_KR + LANES

VMEM_LIMIT = 56 * 1024 * 1024


def _cparams(n_axes):
    return pltpu.CompilerParams(dimension_semantics=("arbitrary",) * n_axes,
                                vmem_limit_bytes=VMEM_LIMIT)


def _dot(a, b):
    return jnp.dot(a, b, preferred_element_type=F32)


def _rms(x, g):
    return x * lax.rsqrt(jnp.mean(x * x, axis=-1, keepdims=True) + NORM_EPS) * g


def _full(shape):
    n = len(shape)
    return pl.BlockSpec(shape, lambda *_: (0,) * n)


def _mod_kernel(c_ref, w_ref, b_ref, o_ref):
    c = c_ref[...]
    sc = c * jax.nn.sigmoid(c)
    o_ref[...] = jnp.dot(sc, w_ref[...], preferred_element_type=F32,
                         precision=lax.Precision.HIGHEST) + b_ref[...]


def _modulation(cvec, w, b):
    R, D = cvec.shape
    N = w.shape[1]
    tn = 1536
    return pl.pallas_call(
        _mod_kernel,
        grid=(N // tn,),
        in_specs=[_full((R, D)),
                  pl.BlockSpec((D, tn), lambda j: (0, j)),
                  pl.BlockSpec((1, tn), lambda j: (0, j))],
        out_specs=pl.BlockSpec((R, tn), lambda j: (0, j)),
        out_shape=jax.ShapeDtypeStruct((R, N), F32),
        compiler_params=_cparams(1),
        name="modulation",
    )(cvec, w, b.reshape(1, N))


def _seg_meansq(x, bd_ref, width, seg):
    x2 = x * x
    hi = x2.astype(BF16)
    lo = (x2 - hi.astype(F32)).astype(BF16)
    bd = bd_ref[0:width, 0:width]
    return (_dot(hi, bd) + _dot(lo, bd)) * (1.0 / seg)


def _rope(x, tab_ref, sh1, sh2):
    return (x * tab_ref[0]
            + pltpu.roll(x, sh1, axis=1) * tab_ref[1]
            + pltpu.roll(x, sh2, axis=1) * tab_ref[2])


def _in_kernel(*refs, rope):
    (x_ref, sh_ref, sc_ref, gpre_ref, win_ref, gq_ref, gk_ref, gcq_ref, gckv_ref,
     wuq_ref, wuk_ref, wuv_ref, ekr_ref, bd_ref) = refs[:14]
    if rope:
        rg_ref, rq_ref, rk_ref = refs[14:17]
        outs = refs[17:]
    else:
        outs = refs[14:]
    h_out, qg_out, qm_out, kd_out, vd_out, km_out, vm_out, u_out = outs

    x = x_ref[...]
    h = _rms(x, gpre_ref[...]) * (1.0 + sc_ref[...]) + sh_ref[...]
    hb = h.astype(BF16)
    h_out[...] = hb
    z = _dot(hb, win_ref[...])

    qg = z[:, C_QG:C_QM]
    qg = qg * lax.rsqrt(_seg_meansq(qg, bd_ref, C_QM - C_QG, GQA_HEAD_DIM) + NORM_EPS) * gq_ref[...]
    for c in range((C_QM - C_QG) // LANES):
        blk = qg[:, c * LANES:(c + 1) * LANES]
        if rope:
            blk = _rope(blk, rg_ref, 96, 32)
        qg_out[:, c * LANES:(c + 1) * LANES] = (blk * (GQA_SCALE * LOG2E)).astype(BF16)

    kd = z[:, C_KD:C_VD]
    kd = kd * lax.rsqrt(_seg_meansq(kd, bd_ref, C_VD - C_KD, GQA_HEAD_DIM) + NORM_EPS) * gk_ref[...]
    for c in range((C_VD - C_KD) // LANES):
        blk = kd[:, c * LANES:(c + 1) * LANES]
        if rope:
            blk = _rope(blk, rg_ref, 96, 32)
        kd_out[:, c * LANES:(c + 1) * LANES] = blk.astype(BF16)
    vd_out[...] = z[:, C_VD:C_CKV].astype(BF16)

    qm = _rms(z[:, C_QM:C_KD], gcq_ref[...]).astype(BF16)
    qm = _dot(qm, wuq_ref[...])
    for c in range(MLA_HEADS):
        blk = qm[:, c * HEAD_SLOT:(c + 1) * HEAD_SLOT]
        if rope:
            blk = _rope(blk, rq_ref, 112, 16)
        qm_out[:, c * HEAD_SLOT:(c + 1) * HEAD_SLOT] = (blk * (MLA_SCALE * LOG2E)).astype(BF16)

    ckv = _rms(z[:, C_CKV:C_U], gckv_ref[...]).astype(BF16)
    kr = z[:, C_KR:C_KR + LANES]
    if rope:
        kr = _rope(kr, rk_ref, 112, 16)
    km = _dot(ckv, wuk_ref[...]) + _dot(kr.astype(BF16), ekr_ref[...])
    km_out[...] = km.astype(BF16)
    vm_out[...] = _dot(ckv, wuv_ref[...]).astype(BF16)

    u_out[...] = z[:, C_U:C_KR]


def _in_proj(x, shift, scale, gpre, lw, tabs, nseg):
    B, L, D = x.shape
    tm = min(512, L // nseg)
    tseg = L // nseg
    nt = tseg // tm
    rope = tabs is not None

    def tok(b, j, i):
        return (b, j * nt + i, 0)

    def tokspec(w):
        return pl.BlockSpec((None, tm, w), tok)

    def tabspec():
        return pl.BlockSpec((3, tm, LANES), lambda b, j, i: (0, j * nt + i, 0))

    vec = lambda w: pl.BlockSpec((None, 1, w), lambda b, j, i: (b, 0, 0))
    in_specs = [tokspec(D), vec(D), vec(D), _full((1, D)), _full((D, N_IN_PACKED)),
                _full((1, 512)), _full((1, 256)), _full((1, MLA_Q_RANK)), _full((1, MLA_KV_RANK)),
                _full((MLA_Q_RANK, 1024)), _full((MLA_KV_RANK, 1024)), _full((MLA_KV_RANK, 512)),
                _full((LANES, 1024)), _full((512, 512))]
    args = [x, shift, scale, gpre, lw["w_in"], lw["g_q"], lw["g_k"], lw["g_cq"], lw["g_ckv"],
            lw["w_uq"], lw["w_uk"], lw["w_uv"], lw["e_kr"], lw["bd64"]]
    if rope:
        in_specs += [tabspec(), tabspec(), tabspec()]
        args += list(tabs)
    widths = [D, 512, 1024, 256, 256, 1024, 512]
    out_specs = [tokspec(w) for w in widths]
    out_shape = [jax.ShapeDtypeStruct((B, L, w), BF16) for w in widths]
    out_specs.append(pl.BlockSpec((None, tm, S5_WIDTH), lambda b, j, i: (b, i, j)))
    out_shape.append(jax.ShapeDtypeStruct((B, tseg, nseg * S5_WIDTH), F32))
    return pl.pallas_call(
        functools.partial(_in_kernel, rope=rope),
        grid=(B, nseg, nt),
        in_specs=in_specs,
        out_specs=out_specs,
        out_shape=out_shape,
        compiler_params=_cparams(3),
        name="in_proj_rope" if rope else "in_proj",
    )(*args)


ONES_ROWS = 16
ATTN_KEY_CHUNK = 512
ATTN_MIN_DENOM = 2.0 ** -60
KEY_NORM_PAD = 1.0 + 2.0 ** -6


def _attn_kernel(*refs, gqa, chunks, seg_starts):
    nseg = len(seg_starts)
    q_ref = refs[0]
    k_refs = refs[1:1 + nseg]
    v_refs = refs[1 + nseg:1 + 2 * nseg]
    o_ref, vt_ref, kn_ref, s0_ref, s1_ref, p0_ref, p1_ref = refs[1 + 2 * nseg:]
    s_refs, p_refs = (s0_ref, s1_ref), (p0_ref, p1_ref)
    hd = LANES // 2

    def keys(par, seg):
        return k_refs[seg][...] if gqa else k_refs[seg][:, par * HEAD_SLOT:(par + 1) * HEAD_SLOT]

    @pl.when(pl.program_id(2) == 0)
    def _():
        for v_ref, start in zip(v_refs, seg_starts):
            vt = v_ref[...].astype(F32).T
            ones = jnp.ones((ONES_ROWS, vt.shape[1]), BF16)
            for par in range(2):
                vt_ref[par, 0:hd, start:start + vt.shape[1]] = vt[par * hd:(par + 1) * hd, :].astype(BF16)
                vt_ref[par, hd:hd + ONES_ROWS, start:start + vt.shape[1]] = ones
        all_ones = jnp.ones((LANES, LANES), BF16)
        for par in range(2):
            big = None
            for seg in range(nseg):
                kf = keys(par, seg).astype(F32)
                n2 = jnp.max(_dot((kf * kf).astype(BF16), all_ones), axis=0, keepdims=True)
                big = n2 if big is None else jnp.maximum(big, n2)
            kn_ref[par] = jnp.broadcast_to(jnp.sqrt(big * KEY_NORM_PAD), kn_ref.shape[1:])

    qt = q_ref[...].astype(F32).T
    qh = []
    for par in range(2):
        if gqa:
            row = lax.broadcasted_iota(jnp.int32, (LANES, 1), 0)
            keep = (row < hd) if par == 0 else (row >= hd)
            qh.append(jnp.where(keep, qt, 0.0).astype(BF16))
        else:
            qh.append(qt[par * HEAD_SLOT:(par + 1) * HEAD_SLOT, :].astype(BF16))

    outs, dens = [], []
    for par in range(2):
        qf = qh[par].astype(F32)
        bound = jnp.sqrt(jnp.sum(qf * qf, axis=0, keepdims=True)) * kn_ref[par, 0:1, 0:1]
        acc = None
        for seg, start in enumerate(seg_starts):
            kh = keys(par, seg)
            pt = jnp.exp2(_dot(kh, qh[par]) - bound).astype(BF16)
            t = _dot(vt_ref[par, :, start:start + kh.shape[0]], pt)
            acc = t if acc is None else acc + t
        outs.append(acc[0:hd, :] / acc[hd:hd + 1, :])
        dens.append(acc[hd:hd + 1, :])
    o_ref[...] = jnp.concatenate(outs, axis=0).T.astype(o_ref.dtype)
    underflow = jnp.min(jnp.minimum(dens[0], dens[1])) < ATTN_MIN_DENOM

    @pl.when(underflow)
    def _():
        _attn_exact(o_ref, k_refs, vt_ref, s_refs, p_refs, qh, gqa=gqa, chunks=chunks, seg_starts=seg_starts)


def _attn_exact(o_ref, k_refs, vt_ref, s_refs, p_refs, qh, *, gqa, chunks, seg_starts):
    hd = LANES // 2

    def scores(par, seg, r0, rows):
        k_ref = k_refs[seg]
        kh = k_ref[r0:r0 + rows, :] if gqa else k_ref[r0:r0 + rows, par * HEAD_SLOT:(par + 1) * HEAD_SLOT]
        st = _dot(kh, qh[par])
        c0 = seg_starts[seg] + r0
        s_refs[par][c0:c0 + rows, :] = st
        return jnp.max(st, axis=0, keepdims=True)

    def probs(par, seg, r0, rows, m):
        c0 = seg_starts[seg] + r0
        p_refs[par][c0:c0 + rows, :] = jnp.exp2(s_refs[par][c0:c0 + rows, :] - m).astype(BF16)

    def weighted(par, seg, r0, rows):
        c0 = seg_starts[seg] + r0
        return _dot(vt_ref[par, :, c0:c0 + rows], p_refs[par][c0:c0 + rows, :])

    def fold(a, b):
        return b if a is None else a + b

    m0 = m1 = acc0 = acc1 = None
    for ch in chunks:
        sm = scores(0, *ch)
        m0 = sm if m0 is None else jnp.maximum(m0, sm)
    for ch in chunks:
        sm = scores(1, *ch)
        m1 = sm if m1 is None else jnp.maximum(m1, sm)
        probs(0, *ch, m0)
    for ch in chunks:
        acc0 = fold(acc0, weighted(0, *ch))
        probs(1, *ch, m1)
    for ch in chunks:
        acc1 = fold(acc1, weighted(1, *ch))
    outs = [a[0:hd, :] / a[hd:hd + 1, :] for a in (acc0, acc1)]
    o_ref[...] = jnp.concatenate(outs, axis=0).T.astype(o_ref.dtype)


def _attention(q, ks, vs, gqa):
    B, Lq, _ = q.shape
    lens = [k.shape[1] for k in ks]
    seg_starts = tuple(int(s) for s in np.cumsum([0] + lens[:-1]))
    Lk = sum(lens)
    tq = min(512, Lq)
    n_pairs = 4
    if gqa:
        q_spec = pl.BlockSpec((None, tq, LANES), lambda b, p, i: (b, i, p))
        k_specs = [pl.BlockSpec((None, n, LANES), lambda b, p, i: (b, 0, p // 2)) for n in lens]
        v_specs = [pl.BlockSpec((None, n, LANES), lambda b, p, i: (b, 0, p // 2)) for n in lens]
    else:
        q_spec = pl.BlockSpec((None, tq, 2 * HEAD_SLOT), lambda b, p, i: (b, i, p))
        k_specs = [pl.BlockSpec((None, n, 2 * HEAD_SLOT), lambda b, p, i: (b, 0, p)) for n in lens]
        v_specs = [pl.BlockSpec((None, n, LANES), lambda b, p, i: (b, 0, p)) for n in lens]
    chunks = tuple((seg, r0, min(ATTN_KEY_CHUNK, n - r0))
                   for seg, n in enumerate(lens) for r0 in range(0, n, ATTN_KEY_CHUNK))
    return pl.pallas_call(
        functools.partial(_attn_kernel, gqa=gqa, chunks=chunks, seg_starts=seg_starts),
        grid=(B, n_pairs, Lq // tq),
        in_specs=[q_spec] + k_specs + v_specs,
        out_specs=pl.BlockSpec((None, tq, LANES), lambda b, p, i: (b, i, p)),
        out_shape=jax.ShapeDtypeStruct((B, Lq, n_pairs * LANES), BF16),
        scratch_shapes=[pltpu.VMEM((2, LANES // 2 + ONES_ROWS, Lk), BF16), pltpu.VMEM((2, 8, LANES), F32),
                        pltpu.VMEM((Lk, tq), F32), pltpu.VMEM((Lk, tq), F32),
                        pltpu.VMEM((Lk, tq), BF16), pltpu.VMEM((Lk, tq), BF16)],
        compiler_params=_cparams(3),
        name="attn_gqa" if gqa else "attn_mla",
    )(q, *ks, *vs)


S5_COLS = 512
assert S5_COLS // S5_STATE * S5_GROUP == LANES


def _s5_kernel(u_ref, bre_ref, bim_ref, cre_ref, cim_ref, a_ref, at_ref, init_ref,
               y_ref, fin_ref, vre, vim, st, *, rev, ti):
    ps = pl.program_id(1)
    ch = pl.program_id(2)
    n = S5_N

    @pl.when((ps == 0) & (ch == 0))
    def _():
        st[...] = jnp.zeros_like(st)

    @pl.when((ps == 1) & (ch == 0))
    def _():
        ends = [(st[0, j:j + 1, :], st[1, j:j + 1, :]) for j in range(S5_SEGS)]
        atr, ati = at_ref[0], at_ref[1]
        cr, ci = init_ref[0], init_ref[1]
        order = range(S5_SEGS - 1, -1, -1) if rev else range(S5_SEGS)
        for j in order:
            st[0, j:j + 1, :] = cr
            st[1, j:j + 1, :] = ci
            er, ei = ends[j]
            cr, ci = atr * cr - ati * ci + er, atr * ci + ati * cr + ei
        fin_ref[0] = cr
        fin_ref[1] = ci

    ub = u_ref[...].reshape(ti * S5_SEGS, S5_WIDTH).astype(BF16)
    for m in range(n // S5_COLS):
        ch = slice(m * LANES, (m + 1) * LANES)
        cols = slice(m * S5_COLS, (m + 1) * S5_COLS)
        vre[:, :, cols] = _dot(ub[:, ch], bre_ref[ch, cols]).reshape(ti, S5_SEGS, S5_COLS)
        vim[:, :, cols] = _dot(ub[:, ch], bim_ref[ch, cols]).reshape(ti, S5_SEGS, S5_COLS)

    def scan(store):
        for cb in range(n // S5_COLS):
            cols = slice(cb * S5_COLS, (cb + 1) * S5_COLS)
            ar = jnp.broadcast_to(a_ref[0, :, cols], (S5_SEGS, S5_COLS))
            ai = jnp.broadcast_to(a_ref[1, :, cols], (S5_SEGS, S5_COLS))

            def body(i, carry):
                xr, xi = carry
                idx = (ti - 1 - i) if rev else i
                nr = ar * xr - ai * xi + vre[idx, :, cols]
                ni = ar * xi + ai * xr + vim[idx, :, cols]
                if store:
                    vre[idx, :, cols] = nr
                    vim[idx, :, cols] = ni
                return nr, ni

            xr, xi = lax.fori_loop(0, ti, body, (st[0, :, cols], st[1, :, cols]), unroll=4)
            st[0, :, cols] = xr
            st[1, :, cols] = xi

    @pl.when(ps == 0)
    def _():
        scan(False)

    @pl.when(ps == 1)
    def _():
        scan(True)
        xr = vre[...].reshape(ti * S5_SEGS, n).astype(BF16)
        xi = vim[...].reshape(ti * S5_SEGS, n).astype(BF16)
        y = _dot(xr, cre_ref[...]) + _dot(xi, cim_ref[...])
        y_ref[...] = y.reshape(ti, S5_SEGS, S5_WIDTH)


def _s5_scan(u, sw, init, rev):
    B, T, _, _ = u.shape
    ti = min(128, T)
    nch = T // ti
    n = S5_N

    def chunk(c):
        return (nch - 1 - c) if rev else c

    d = 1 if rev else 0
    return pl.pallas_call(
        functools.partial(_s5_kernel, rev=rev, ti=ti),
        grid=(B, 2, nch),
        in_specs=[pl.BlockSpec((None, ti, S5_SEGS, S5_WIDTH), lambda b, p, c: (b, chunk(c), 0, 0)),
                  _full((S5_WIDTH, n)), _full((S5_WIDTH, n)), _full((n, S5_WIDTH)), _full((n, S5_WIDTH)),
                  _full((2, 1, n)), _full((2, 1, n)),
                  pl.BlockSpec((None, 2, 1, n), lambda b, p, c: (b, 0, 0, 0))],
        out_specs=[pl.BlockSpec((None, ti, S5_SEGS, S5_WIDTH),
                                lambda b, p, c: (b, chunk(c * p), 0, 0)),
                   pl.BlockSpec((None, 2, 1, n), lambda b, p, c: (b, 0, 0, 0))],
        out_shape=[jax.ShapeDtypeStruct(u.shape, F32), jax.ShapeDtypeStruct((B, 2, 1, n), F32)],
        scratch_shapes=[pltpu.VMEM((ti, S5_SEGS, n), F32), pltpu.VMEM((ti, S5_SEGS, n), F32),
                        pltpu.VMEM((2, S5_SEGS, n), F32)],
        compiler_params=_cparams(3),
        name="s5_bwd" if rev else "s5_fwd",
    )(u, sw["b_re"][d], sw["b_im"][d], sw["c_re"][d], sw["c_im_neg"][d],
      sw["a"][d], sw["a_t"][(d, T)], init)


def _glu_kernel(u_ref, yf_ref, yb_ref, d_ref, w_ref, o_ref):
    y = d_ref[...] * u_ref[...] + yf_ref[...] + yb_ref[...]
    g = 0.5 * y * (1.0 + jnp.tanh(math.sqrt(2.0 / math.pi) * (y + 0.044715 * (y * y * y))))
    t = _dot(g.astype(BF16), w_ref[...])
    o_ref[...] = (t[:, 0:S5_WIDTH] * jax.nn.sigmoid(t[:, S5_WIDTH:2 * S5_WIDTH])).astype(o_ref.dtype)


def _s5_glu(u, yf, yb, dskip, wglu, nseg):
    B, T, _ = u.shape
    tm = min(256, T)
    nt = T // tm
    seg = pl.BlockSpec((None, tm, S5_WIDTH), lambda b, j, i: (b, i, j))
    return pl.pallas_call(
        _glu_kernel,
        grid=(B, nseg, nt),
        in_specs=[seg, seg, seg, _full((1, S5_WIDTH)), _full((S5_WIDTH, 2 * S5_WIDTH))],
        out_specs=pl.BlockSpec((None, tm, S5_WIDTH), lambda b, j, i: (b, j * nt + i, 0)),
        out_shape=jax.ShapeDtypeStruct((B, T * nseg, S5_WIDTH), BF16),
        compiler_params=_cparams(3),
        name="s5_glu",
    )(u, yf, yb, dskip, wglu)


def _merge_kernel(h_ref, ya_ref, yb_ref, yc_ref, x_ref, gate_ref, wmg_ref, bmg_ref, wbr_ref, wout_ref,
                  gpost_ref, o_ref):
    h = h_ref[...]
    d = x_ref.shape[-1]
    acc = None
    for n, y_ref in enumerate((ya_ref, yb_ref, yc_ref)):
        g = jax.nn.sigmoid(_dot(h, wmg_ref[:, n * d:(n + 1) * d]) + bmg_ref[:, n * d:(n + 1) * d])
        t = g * _dot(y_ref[...], wbr_ref[n])
        acc = t if acc is None else acc + t
    y = _dot(acc.astype(BF16), wout_ref[...])
    o_ref[...] = x_ref[...] + gate_ref[...] * _rms(y, gpost_ref[...])


def _merge(h, ya, yb, yc, x, gate, lw):
    B, L, D = x.shape
    tm = min(512, L)
    tok = lambda w: pl.BlockSpec((None, tm, w), lambda b, i: (b, i, 0))
    return pl.pallas_call(
        _merge_kernel,
        grid=(B, L // tm),
        in_specs=[tok(D), tok(512), tok(512), tok(512), tok(D),
                  pl.BlockSpec((None, 1, D), lambda b, i: (b, 0, 0)),
                  _full((D, N_BRANCHES * D)), _full((1, N_BRANCHES * D)), _full((N_BRANCHES, 512, D)),
                  _full((D, D)), _full((1, D))],
        out_specs=tok(D),
        out_shape=jax.ShapeDtypeStruct((B, L, D), F32),
        compiler_params=_cparams(2),
        name="merge",
    )(h, ya, yb, yc, x, gate, lw["w_mg"], lw["b_mg"], lw["w_branch"], lw["w_out"], lw["g_mix_post"])


def _ffn_kernel(x_ref, sh_ref, sc_ref, gate_ref, gpre_ref, gpost_ref, w1_ref, w3_ref, w2_ref, o_ref,
                h_s, acc_s):
    f = pl.program_id(2)

    @pl.when(f == 0)
    def _():
        h = _rms(x_ref[...], gpre_ref[...]) * (1.0 + sc_ref[...]) + sh_ref[...]
        h_s[...] = h.astype(BF16)
        acc_s[...] = jnp.zeros_like(acc_s)

    h = h_s[...]
    a = _dot(h, w1_ref[...])
    g = (a * jax.nn.sigmoid(a)) * _dot(h, w3_ref[...])
    acc_s[...] += _dot(g.astype(BF16), w2_ref[...])

    @pl.when(f == pl.num_programs(2) - 1)
    def _():
        o_ref[...] = x_ref[...] + gate_ref[...] * _rms(acc_s[...], gpost_ref[...])


def _ffn(x, shift, scale, gate, gpre, gpost, w1, w3, w2):
    B, L, D = x.shape
    dff = w1.shape[1]
    tm = min(512, L)
    tf = dff // 2 if (dff // 2) % LANES == 0 else dff
    tok = pl.BlockSpec((None, tm, D), lambda b, i, f: (b, i, 0))
    vec = pl.BlockSpec((None, 1, D), lambda b, i, f: (b, 0, 0))
    return pl.pallas_call(
        _ffn_kernel,
        grid=(B, L // tm, dff // tf),
        in_specs=[tok, vec, vec, vec, _full((1, D)), _full((1, D)),
                  pl.BlockSpec((D, tf), lambda b, i, f: (0, f)),
                  pl.BlockSpec((D, tf), lambda b, i, f: (0, f)),
                  pl.BlockSpec((tf, D), lambda b, i, f: (f, 0))],
        out_specs=tok,
        out_shape=jax.ShapeDtypeStruct((B, L, D), F32),
        scratch_shapes=[pltpu.VMEM((tm, D), BF16), pltpu.VMEM((tm, D), F32)],
        compiler_params=_cparams(3),
        name="ffn",
    )(x, shift, scale, gate, gpre, gpost, w1, w3, w2)


MOE_BLOCK = 1024
MOE_CHUNK = 64
MOE_TILE = 512
MOE_SORT_ROWS = 512


def _lane_col(a, lane, k):
    return jnp.sum(jnp.where(lane == k, a, 0.0), axis=1, keepdims=True)


def _router_kernel(x_ref, sh_ref, sc_ref, gpre_ref, rw_ref, rb_ref, h_out, route_out, cnt_out):
    lane = lax.broadcasted_iota(jnp.int32, (1, LANES), 1)
    h = _rms(x_ref[...], gpre_ref[...]) * (1.0 + sc_ref[...]) + sh_ref[...]
    h_out[...] = h.astype(BF16)
    logits = jnp.dot(h, rw_ref[...], preferred_element_type=F32,
                     precision=lax.Precision.HIGHEST) + rb_ref[...]
    m1 = jnp.max(logits, axis=1, keepdims=True)
    i1 = jnp.min(jnp.where(logits == m1, lane, LANES), axis=1, keepdims=True)
    rest = jnp.where(lane == i1, -jnp.inf, logits)
    m2 = jnp.max(rest, axis=1, keepdims=True)
    i2 = jnp.min(jnp.where(rest == m2, lane, LANES), axis=1, keepdims=True)
    e2 = jnp.exp(m2 - m1)
    den = 1.0 + e2
    route_out[...] = (jnp.where(lane == 0, i1.astype(F32), 0.0) + jnp.where(lane == 1, i2.astype(F32), 0.0)
                      + jnp.where(lane == 2, 1.0 / den, 0.0) + jnp.where(lane == 3, e2 / den, 0.0))
    member = jnp.where((lane == i1) | (lane == i2), 1.0, 0.0)
    cnt_out[...] = jnp.sum(member, axis=0, keepdims=True)


def _bf16_pieces(w):
    hi = w.astype(BF16).astype(F32)
    r1 = w - hi
    lo = r1.astype(BF16).astype(F32)
    return hi, lo, r1 - lo


def _dispatch_kernel(h_ref, route_ref, ltri_ref, ustr_ref, xs_out, pos_out):
    d = h_ref.shape[1]
    lane = lax.broadcasted_iota(jnp.int32, (1, LANES), 1)
    lanef = lane.astype(F32)
    route = route_ref[...]
    i1, i2 = _lane_col(route, lane, 0), _lane_col(route, lane, 1)
    g1, g2 = _lane_col(route, lane, 2), _lane_col(route, lane, 3)
    m1, m2 = lanef == i1, lanef == i2
    member = jnp.where(m1 | m2, 1.0, 0.0)
    before = _dot(ltri_ref[...], member.astype(BF16))
    counts = jnp.sum(member, axis=0, keepdims=True)
    padded = jnp.floor((counts + (MOE_CHUNK - 1.0)) * (1.0 / MOE_CHUNK)) * MOE_CHUNK
    seg = _dot(jnp.broadcast_to(padded, (8, LANES)).astype(BF16), ustr_ref[...])[0:1, :]
    slot = before + seg
    pos1 = jnp.sum(jnp.where(m1, slot, 0.0), axis=1, keepdims=True)
    pos2 = jnp.sum(jnp.where(m2, slot, 0.0), axis=1, keepdims=True)
    pos = jnp.where(lane == 0, pos1, 0.0) + jnp.where(lane == 1, pos2, 0.0)
    pos_out[...] = pos
    post = pos.T
    row1, row2 = post[0:1, :], post[1:2, :]

    def gate_cols(g):
        a, b, c = _bf16_pieces(g)
        return (jnp.where(lane == 0, a, 0.0) + jnp.where(lane == 1, b, 0.0)
                + jnp.where(lane == 2, c, 0.0)).astype(BF16)

    gc1, gc2 = gate_cols(g1), gate_cols(g2)
    h = h_ref[...]
    for c in range(xs_out.shape[0] // MOE_SORT_ROWS):
        rows = (lax.broadcasted_iota(jnp.int32, (MOE_SORT_ROWS, 1), 0) + c * MOE_SORT_ROWS).astype(F32)
        q1, q2 = rows == row1, rows == row2
        sl = slice(c * MOE_SORT_ROWS, (c + 1) * MOE_SORT_ROWS)
        xs_out[sl, 0:d] = _dot(jnp.where(q1 | q2, 1.0, 0.0).astype(BF16), h).astype(BF16)
        gates = (_dot(jnp.where(q1, 1.0, 0.0).astype(BF16), gc1)
                 + _dot(jnp.where(q2, 1.0, 0.0).astype(BF16), gc2))
        xs_out[sl, d:d + LANES] = gates.astype(BF16)


def _expert_kernel(*refs, n_in):
    nu_ref = refs[1]
    x_refs = refs[3:3 + n_in]
    w1_ref, w3_ref, w2_ref, o_ref, x_s, acc_s = refs[3 + n_in:]
    i = pl.program_id(0)
    f = pl.program_id(1)
    d = o_ref.shape[1]

    @pl.when(i < nu_ref[0])
    def _():
        @pl.when(f == 0)
        def _():
            for k, x_ref in enumerate(x_refs):
                x_s[k * MOE_CHUNK:(k + 1) * MOE_CHUNK, :] = x_ref[...]
            acc_s[...] = jnp.zeros_like(acc_s)

        x = x_s[:, 0:d]
        a = _dot(x, w1_ref[...])
        g = (a * jax.nn.sigmoid(a)) * _dot(x, w3_ref[...])
        acc_s[...] += _dot(g.astype(BF16), w2_ref[...])

        @pl.when(f == pl.num_programs(1) - 1)
        def _():
            gate = jnp.sum(x_s[:, d:d + LANES].astype(F32), axis=1, keepdims=True)
            o_ref[...] = (acc_s[...] * gate).astype(o_ref.dtype)

    @pl.when(i >= nu_ref[0])
    def _():
        o_ref[...] = jnp.zeros_like(o_ref)


def _combine_kernel(*refs, rc, n_in):
    ys_refs = refs[1:1 + n_in]
    pos_ref, x_ref, gate_ref, gpost_ref, o_ref, ys_s = refs[1 + n_in:]
    g = pl.program_id(1)
    for k, ys_ref in enumerate(ys_refs):
        row = pl.multiple_of((g * n_in + k) * MOE_CHUNK, MOE_CHUNK)
        ys_s[pl.ds(row, MOE_CHUNK), :] = ys_ref[...]

    @pl.when(g == rc // n_in - 1)
    def _():
        lane = lax.broadcasted_iota(jnp.int32, (1, LANES), 1)
        pos = pos_ref[...]
        pos1, pos2 = _lane_col(pos, lane, 0), _lane_col(pos, lane, 1)
        acc = None
        for c in range(rc * MOE_CHUNK // MOE_SORT_ROWS):
            cols = (lax.broadcasted_iota(jnp.int32, (1, MOE_SORT_ROWS), 1) + c * MOE_SORT_ROWS).astype(F32)
            sel = jnp.where((cols == pos1) | (cols == pos2), 1.0, 0.0).astype(BF16)
            t = _dot(sel, ys_s[c * MOE_SORT_ROWS:(c + 1) * MOE_SORT_ROWS, :])
            acc = t if acc is None else acc + t
        o_ref[...] = x_ref[...] + gate_ref[...] * _rms(acc, gpost_ref[...])


def _moe_tables(counts, rc, nt):
    ne = counts.shape[1]
    per_tile = MOE_TILE // MOE_CHUNK
    cch = (counts + MOE_CHUNK - 1) // MOE_CHUNK
    tiles_e = (jnp.sum(cch, axis=0) + per_tile - 1) // per_tile
    tile_end = jnp.cumsum(tiles_e)
    n_used = tile_end[-1]
    chunk_start = (tile_end - tiles_e)[None, :] * per_tile + (jnp.cumsum(cch, axis=0) - cch)
    seg_end = jnp.cumsum(cch, axis=1)
    r = jnp.arange(rc)
    e_r = jnp.minimum(jnp.sum(r[None, :, None] >= seg_end[:, None, :], axis=-1), ne - 1)
    onehot = (e_r[:, :, None] == jnp.arange(ne)[None, None, :]).astype(jnp.int32)
    pick = lambda t: jnp.sum(onehot * t[:, None, :], axis=-1)
    dest = pick(chunk_start) + r[None, :] - pick(seg_end - cch)
    n_valid = seg_end[:, -1]
    last = jnp.sum(jnp.where(r[None, :] == n_valid[:, None] - 1, dest, 0), axis=1, keepdims=True)
    dest = jnp.where(r[None, :] < n_valid[:, None], dest, last)
    t = jnp.maximum(jnp.minimum(jnp.arange(nt), n_used - 1), 0)
    tile_e = jnp.minimum(jnp.sum(t[:, None] >= tile_end[None, :], axis=-1), ne - 1)
    e_hot = (tile_e[:, None] == jnp.arange(ne)[None, :]).astype(jnp.int32)
    of_e = lambda a: jnp.sum(e_hot * a[None, :], axis=-1)
    of_eb = lambda a: jnp.sum(e_hot[:, None, :] * a[None, :, :], axis=-1)
    q = (t - of_e(tile_end - tiles_e))[:, None] * per_tile + jnp.arange(per_tile)[None, :]
    q = jnp.where(q < of_e(jnp.sum(cch, axis=0))[:, None], q, q[:, :1])
    blk_end = of_eb(jnp.cumsum(cch, axis=0))
    b_q = jnp.minimum(jnp.sum(q[:, :, None] >= blk_end[:, None, :], axis=-1), cch.shape[0] - 1)
    b_hot = (b_q[:, :, None] == jnp.arange(cch.shape[0])[None, None, :]).astype(jnp.int32)
    of_b = lambda a: jnp.sum(b_hot * a[:, None, :], axis=-1)
    xsrc = b_q * rc + of_b(of_eb(seg_end - cch)) + q - of_b(blk_end - of_eb(cch))
    i32 = lambda a: a.reshape(-1).astype(jnp.int32)
    return i32(dest), i32(xsrc), i32(tile_e), i32(n_used)


def _moe(x, shift, scale, gate, gpre, gpost, rw, rb, w1, w3, w2):
    B, L, D = x.shape
    ne, _, dff = w1.shape
    n = B * L
    t = min(MOE_BLOCK, L)
    nblk, per_batch = n // t, L // t
    per_tile = MOE_TILE // MOE_CHUNK
    rc = pl.cdiv(2 * t + ne * MOE_CHUNK, MOE_TILE) * per_tile
    nt = (2 * n // MOE_CHUNK + nblk * ne + per_tile - 1) // per_tile + ne
    tf = dff // 2 if (dff // 2) % LANES == 0 else dff
    nf = dff // tf
    xf = x.reshape(n, D)
    wide = D + LANES

    blk = lambda w: pl.BlockSpec((t, w), lambda i: (i, 0))
    vec1 = pl.BlockSpec((None, 1, D), lambda i: (i // per_batch, 0, 0))
    h, route, cnt = pl.pallas_call(
        _router_kernel,
        grid=(nblk,),
        in_specs=[blk(D), vec1, vec1, _full((1, D)), _full((D, LANES)), _full((1, LANES))],
        out_specs=[blk(D), blk(LANES), pl.BlockSpec((None, 1, LANES), lambda i: (i, 0, 0))],
        out_shape=[jax.ShapeDtypeStruct((n, D), BF16), jax.ShapeDtypeStruct((n, LANES), F32),
                   jax.ShapeDtypeStruct((nblk, 1, LANES), F32)],
        compiler_params=_cparams(1),
        name="moe_router",
    )(xf, shift, scale, gpre, rw, rb)

    counts = cnt[:, 0, :ne].astype(jnp.int32)
    dest, xsrc, tile_e, n_used = _moe_tables(counts, rc, nt)

    ltri = jnp.asarray(np.tril(np.ones((t, t), np.float32), -1), BF16)
    ustr = jnp.asarray(np.triu(np.ones((LANES, LANES), np.float32), 1), BF16)
    xs, pos = pl.pallas_call(
        _dispatch_kernel,
        grid=(nblk,),
        in_specs=[blk(D), blk(LANES), _full((t, t)), _full((LANES, LANES))],
        out_specs=[pl.BlockSpec((rc * MOE_CHUNK, wide), lambda b: (b, 0)), blk(LANES)],
        out_shape=[jax.ShapeDtypeStruct((nblk * rc * MOE_CHUNK, wide), BF16),
                   jax.ShapeDtypeStruct((n, LANES), F32)],
        compiler_params=_cparams(1),
        name="moe_dispatch",
    )(h, route, ltri, ustr)

    def tile(i, nu):
        return jnp.maximum(jnp.minimum(i, nu[0] - 1), 0)

    def fcol(i, f, nu):
        return jnp.where(i < nu[0], f, nf - 1)

    def chunk_in(k):
        return pl.BlockSpec((MOE_CHUNK, wide), lambda i, f, te, nu, src: (src[tile(i, nu) * per_tile + k], 0))

    ys = pl.pallas_call(
        functools.partial(_expert_kernel, n_in=per_tile),
        grid_spec=pltpu.PrefetchScalarGridSpec(
            num_scalar_prefetch=3,
            grid=(nt, nf),
            in_specs=[chunk_in(k) for k in range(per_tile)] + [
                pl.BlockSpec((None, D, tf), lambda i, f, te, nu, src: (te[i], 0, fcol(i, f, nu))),
                pl.BlockSpec((None, D, tf), lambda i, f, te, nu, src: (te[i], 0, fcol(i, f, nu))),
                pl.BlockSpec((None, tf, D), lambda i, f, te, nu, src: (te[i], fcol(i, f, nu), 0))],
            out_specs=pl.BlockSpec((MOE_TILE, D), lambda i, f, te, nu, src: (i, 0)),
            scratch_shapes=[pltpu.VMEM((MOE_TILE, wide), BF16), pltpu.VMEM((MOE_TILE, D), F32)]),
        out_shape=jax.ShapeDtypeStruct((nt * MOE_TILE, D), BF16),
        compiler_params=_cparams(2),
        name="moe_experts",
    )(tile_e, n_used, xsrc, *([xs] * per_tile), w1, w3, w2)

    def chunk_out(k):
        return pl.BlockSpec((MOE_CHUNK, D), lambda b, g, src: (src[b * rc + g * per_tile + k], 0))

    out = pl.pallas_call(
        functools.partial(_combine_kernel, rc=rc, n_in=per_tile),
        grid_spec=pltpu.PrefetchScalarGridSpec(
            num_scalar_prefetch=1,
            grid=(nblk, rc // per_tile),
            in_specs=[chunk_out(k) for k in range(per_tile)] + [
                pl.BlockSpec((t, LANES), lambda b, g, src: (b, 0)),
                pl.BlockSpec((t, D), lambda b, g, src: (b, 0)),
                pl.BlockSpec((None, 1, D), lambda b, g, src: (b // per_batch, 0, 0)),
                pl.BlockSpec((1, D), lambda b, g, src: (0, 0))],
            out_specs=pl.BlockSpec((t, D), lambda b, g, src: (b, 0)),
            scratch_shapes=[pltpu.VMEM((rc * MOE_CHUNK, D), BF16)]),
        out_shape=jax.ShapeDtypeStruct((n, D), F32),
        compiler_params=_cparams(2),
        name="moe_combine",
    )(dest, *([ys] * per_tile), pos, xf, gate, gpost)
    return out.reshape(B, L, D)


def _rope_tables(n_lat):
    pos = jnp.arange(n_lat, dtype=jnp.int32)
    row = (pos // GRID_W).astype(F32)
    col = (pos % GRID_W).astype(F32)

    def cos_sin(rot_dim):
        axis_dim = rot_dim // 2
        inv_freq = ROPE_THETA ** (-jnp.arange(0, axis_dim, 2, dtype=F32) / axis_dim)
        ang = jnp.concatenate([row[:, None] * inv_freq, col[:, None] * inv_freq], axis=-1)
        return jnp.cos(ang), jnp.sin(ang)

    def table(cos, sin, start, stop):
        reps = (stop - start) // (2 * cos.shape[1])
        zero = jnp.zeros_like(sin)

        def lanes(first, second, fill):
            body = jnp.tile(jnp.concatenate([first, second], axis=1), (1, reps))
            return jnp.pad(body, ((0, 0), (start, LANES - stop)), constant_values=fill)

        return jnp.stack([lanes(cos, cos, 1.0), lanes(-sin, zero, 0.0), lanes(zero, sin, 0.0)]).astype(F32)

    gc, gs = cos_sin(GQA_HEAD_DIM)
    mc, ms = cos_sin(MLA_ROPE_DIM)
    t_gqa = table(gc, gs, 0, LANES)
    t_mq = table(mc, ms, MLA_NOPE_DIM, MLA_NOPE_DIM + MLA_ROPE_DIM)
    t_kr = table(mc, ms, 0, MLA_ROPE_DIM)
    return t_gqa, t_mq, t_kr


def _pack_layer(p, l):
    D = p["w_in"].shape[1]
    w = p["w_in"][l]
    offs = np.cumsum([0, 512, MLA_Q_RANK, 128, 128, MLA_KV_RANK, MLA_ROPE_DIM, S5_WIDTH])
    qg, qm, kg, vg, ckv, kr, u = [w[:, offs[i]:offs[i + 1]] for i in range(7)]
    dup = lambda t: jnp.concatenate([t[:, 0:64], t[:, 0:64], t[:, 64:128], t[:, 64:128]], axis=1)
    w_in = jnp.concatenate([qg, qm, dup(kg), dup(vg), ckv, u, kr,
                            jnp.zeros((D, LANES - MLA_ROPE_DIM), F32)], axis=1).astype(BF16)

    wq = p["w_uq"][l].reshape(MLA_Q_RANK, MLA_HEADS, MLA_NOPE_DIM + MLA_ROPE_DIM)
    w_uq = jnp.pad(wq, ((0, 0), (0, 0), (0, HEAD_SLOT - wq.shape[-1]))).reshape(MLA_Q_RANK, -1).astype(BF16)
    wkv = p["w_ukv"][l].reshape(MLA_KV_RANK, MLA_HEADS, MLA_NOPE_DIM + MLA_V_DIM)
    w_uk = jnp.pad(wkv[:, :, :MLA_NOPE_DIM], ((0, 0), (0, 0), (0, HEAD_SLOT - MLA_NOPE_DIM)))
    w_uk = w_uk.reshape(MLA_KV_RANK, -1).astype(BF16)
    w_uv = wkv[:, :, MLA_NOPE_DIM:].reshape(MLA_KV_RANK, -1).astype(BF16)
    e = np.zeros((LANES, MLA_HEADS, HEAD_SLOT), np.float32)
    for r in range(MLA_ROPE_DIM):
        e[r, :, MLA_NOPE_DIM + r] = 1.0
    bd = np.kron(np.eye(512 // GQA_HEAD_DIM, dtype=np.float32), np.ones((GQA_HEAD_DIM, GQA_HEAD_DIM), np.float32))
    return {
        "w_in": w_in,
        "g_q": jnp.tile(p["g_q"][l], GQA_HEADS).reshape(1, -1),
        "g_k": jnp.tile(p["g_k"][l], 2 * GQA_KV_HEADS).reshape(1, -1),
        "g_cq": p["g_cq"][l].reshape(1, -1),
        "g_ckv": p["g_ckv"][l].reshape(1, -1),
        "w_uq": w_uq, "w_uk": w_uk, "w_uv": w_uv,
        "e_kr": jnp.asarray(e.reshape(LANES, -1), BF16),
        "bd64": jnp.asarray(bd, BF16),
        "w_mg": p["w_merge_gate"][l].astype(BF16),
        "b_mg": p["b_merge_gate"][l].reshape(1, -1),
        "w_branch": p["w_branch"][l].astype(BF16),
        "w_out": p["w_out"][l].astype(BF16),
        "g_mix_pre": p["g_mix_pre"][l].reshape(1, -1),
        "g_mix_post": p["g_mix_post"][l].reshape(1, -1),
        "g_ffn_pre": p["g_ffn_pre"][l].reshape(1, -1),
        "g_ffn_post": p["g_ffn_post"][l].reshape(1, -1),
        "w_glu": p["w_glu"][l].astype(BF16),
        "s5_d": p["s5_d"][l].reshape(1, -1),
    }


def _cpow(ar, ai, n):
    rr, ri = None, None
    br, bi = ar, ai
    while n:
        if n & 1:
            rr, ri = (br, bi) if rr is None else (rr * br - ri * bi, rr * bi + ri * br)
        n >>= 1
        if n:
            br, bi = br * br - bi * bi, 2.0 * br * bi
    return rr, ri


def _s5_weights(p, l, seg_lens):
    G, P, H = S5_GROUPS, S5_STATE, S5_GROUP
    eye = jnp.eye(G, dtype=F32)
    out = {"b_re": [], "b_im": [], "c_re": [], "c_im_neg": [], "a": [], "a_t": {}}
    for d in range(2):
        lr = jnp.minimum(p["s5_lambda_re"][l, d], -1e-4)
        li = p["s5_lambda_im"][l, d]
        dt = jnp.exp(p["s5_log_dt"][l, d])[:, None]
        mag = jnp.exp(lr * dt)
        abr, abi = mag * jnp.cos(li * dt), mag * jnp.sin(li * dt)
        den = lr * lr + li * li
        fr = ((abr - 1.0) * lr + abi * li) / den
        fi = (abi * lr - (abr - 1.0) * li) / den
        br, bi = p["s5_b_re"][l, d], p["s5_b_im"][l, d]
        bbr = fr[..., None] * br - fi[..., None] * bi
        bbi = fr[..., None] * bi + fi[..., None] * br
        blk_b = lambda t: jnp.einsum("gph,gk->ghkp", t, eye).reshape(G * H, G * P).astype(BF16)
        blk_c = lambda t: jnp.einsum("ghp,gk->gpkh", t, eye).reshape(G * P, G * H).astype(BF16)
        out["b_re"].append(blk_b(bbr))
        out["b_im"].append(blk_b(bbi))
        out["c_re"].append(blk_c(p["s5_c_re"][l, d]))
        out["c_im_neg"].append(blk_c(-p["s5_c_im"][l, d]))
        out["a"].append(jnp.stack([abr.reshape(1, -1), abi.reshape(1, -1)]))
        for t in seg_lens:
            tr, ti = _cpow(abr, abi, t)
            out["a_t"][(d, t)] = jnp.stack([tr.reshape(1, -1), ti.reshape(1, -1)])
    return out


def _seg_view(t):
    B, T, _ = t.shape
    return t.reshape(B, T, S5_SEGS, S5_WIDTH)


def _to_segments(t):
    B, L, W = t.shape
    return t.reshape(B, S5_SEGS, L // S5_SEGS, W).transpose(0, 2, 1, 3).reshape(B, L // S5_SEGS, S5_SEGS * W)


def kernel(x, c, ctx, c_ctx, w_mod, b_mod, g_mix_pre, g_mix_post, g_ffn_pre, g_ffn_post, w_in, g_q, g_k, g_cq, g_ckv, w_uq, w_ukv, s5_lambda_re, s5_lambda_im, s5_log_dt, s5_b_re, s5_b_im, s5_c_re, s5_c_im, s5_d, w_glu, w_branch, w_merge_gate, b_merge_gate, w_out, ffn_w1, ffn_w3, ffn_w2, router_w, router_b, moe_w1, moe_w3, moe_w2):
    p = dict(w_in=w_in, g_q=g_q, g_k=g_k, g_cq=g_cq, g_ckv=g_ckv, w_uq=w_uq, w_ukv=w_ukv,
             s5_lambda_re=s5_lambda_re, s5_lambda_im=s5_lambda_im, s5_log_dt=s5_log_dt,
             s5_b_re=s5_b_re, s5_b_im=s5_b_im, s5_c_re=s5_c_re, s5_c_im=s5_c_im, s5_d=s5_d, w_glu=w_glu,
             w_branch=w_branch, w_merge_gate=w_merge_gate, b_merge_gate=b_merge_gate, w_out=w_out,
             g_mix_pre=g_mix_pre, g_mix_post=g_mix_post, g_ffn_pre=g_ffn_pre, g_ffn_post=g_ffn_post)
    B, L, D = x.shape
    Lc = ctx.shape[1]
    depth = w_mod.shape[0]
    tabs = _rope_tables(L)
    t_lat, t_ctx = L // S5_SEGS, Lc // S5_SEGS

    cvec = jnp.concatenate([c, c_ctx[None, :], jnp.zeros((8 - B - 1, D), F32)], axis=0)
    lat, cx = x, ctx
    for layer in range(depth):
        last = layer == depth - 1
        lw = _pack_layer(p, layer)
        sw = _s5_weights(p, layer, (t_lat, t_ctx))
        mod = _modulation(cvec, w_mod[layer], b_mod[layer])
        mod_l = [mod[:B, i * D:(i + 1) * D].reshape(B, 1, D) for i in range(6)]
        mod_c = [jnp.broadcast_to(mod[B, i * D:(i + 1) * D], (B, 1, D)) for i in range(6)]

        h_l, qg_l, qm_l, kd_l, vd_l, km_l, vm_l, u_l = _in_proj(
            lat, mod_l[0], mod_l[1], lw["g_mix_pre"], lw, tabs, S5_SEGS)
        h_c, qg_c, qm_c, kd_c, vd_c, km_c, vm_c, u_c = _in_proj(
            cx, mod_c[0], mod_c[1], lw["g_mix_pre"], lw, None, 1)
        ya_l = _attention(qg_l, (kd_c, kd_l), (vd_c, vd_l), gqa=True)
        yb_l = _attention(qm_l, (km_c, km_l), (vm_c, vm_l), gqa=False)

        u_cs = _to_segments(u_c)
        zero = jnp.zeros((B, 2, 1, S5_N), F32)
        yf_c, fin_f = _s5_scan(_seg_view(u_cs), sw, zero, rev=False)
        yb_c, fin_b = _s5_scan(_seg_view(u_cs), sw, zero, rev=True)
        yf_l, _ = _s5_scan(_seg_view(u_l), sw, fin_f, rev=False)
        yr_l, _ = _s5_scan(_seg_view(u_l), sw, fin_b, rev=True)
        flat = lambda t: t.reshape(t.shape[0], t.shape[1], S5_SEGS * S5_WIDTH)
        yc_l = _s5_glu(u_l, flat(yf_l), flat(yr_l), lw["s5_d"], lw["w_glu"], S5_SEGS)

        lat = _merge(h_l, ya_l, yb_l, yc_l, lat, mod_l[2], lw)
        if not last:
            ya_c = _attention(qg_c, (kd_c,), (vd_c,), gqa=True)
            yb_c2 = _attention(qm_c, (km_c,), (vm_c,), gqa=False)
            yc_c = _s5_glu(u_cs, flat(yf_c), flat(yb_c), lw["s5_d"], lw["w_glu"], S5_SEGS)
            cx = _merge(h_c, ya_c, yb_c2, yc_c, cx, mod_c[2], lw)

        i = layer // 2
        if layer % 2 == 0:
            w1, w3, w2 = ffn_w1[i].astype(BF16), ffn_w3[i].astype(BF16), ffn_w2[i].astype(BF16)
            mix = lambda t, m: _ffn(t, m[3], m[4], m[5], lw["g_ffn_pre"], lw["g_ffn_post"], w1, w3, w2)
        else:
            w1, w3, w2 = moe_w1[i].astype(BF16), moe_w3[i].astype(BF16), moe_w2[i].astype(BF16)
            rw = jnp.pad(router_w[i], ((0, 0), (0, LANES - N_EXPERTS)))
            rb = jnp.concatenate([router_b[i], jnp.full((LANES - N_EXPERTS,), -1e30, F32)]).reshape(1, LANES)
            mix = lambda t, m: _moe(t, m[3], m[4], m[5], lw["g_ffn_pre"], lw["g_ffn_post"], rw, rb, w1, w3, w2)
        lat = mix(lat, mod_l)
        if not last:
            cx = mix(cx, mod_c)
    return lat
```

```python
import functools
import math

import jax
import jax.numpy as jnp
import numpy as np
from jax import lax
from jax.experimental import pallas as pl
from jax.experimental.pallas import tpu as pltpu

F32 = jnp.float32
BF16 = jnp.bfloat16

GRID_W = 64
ROPE_THETA = 10000.0
NORM_EPS = 1e-6

GQA_HEADS = 8
GQA_KV_HEADS = 2
GQA_HEAD_DIM = 64
GQA_SCALE = 1.0 / math.sqrt(GQA_HEAD_DIM)

MLA_HEADS = 8
MLA_NOPE_DIM = 64
MLA_ROPE_DIM = 32
MLA_V_DIM = 64
MLA_Q_RANK = 384
MLA_KV_RANK = 256
MLA_SCALE = 1.0 / math.sqrt(MLA_NOPE_DIM + MLA_ROPE_DIM)
LOG2E = math.log2(math.e)

S5_WIDTH = 512
S5_GROUP = 16
S5_GROUPS = S5_WIDTH // S5_GROUP
S5_STATE = 64
S5_N = S5_GROUPS * S5_STATE
S5_SEGS = 8

N_BRANCHES = 3
N_EXPERTS = 8

LANES = 128
HEAD_SLOT = 128

C_QG = 0
C_QM = C_QG + GQA_HEADS * GQA_HEAD_DIM
C_KD = C_QM + MLA_Q_RANK
C_VD = C_KD + 2 * GQA_KV_HEADS * GQA_HEAD_DIM
C_CKV = C_VD + 2 * GQA_KV_HEADS * GQA_HEAD_DIM
C_U = C_CKV + MLA_KV_RANK
C_KR = C_U + S5_WIDTH
N_IN_PACKED = C_KR + LANES

VMEM_LIMIT = 56 * 1024 * 1024


def _cparams(n_axes):
    return pltpu.CompilerParams(dimension_semantics=("arbitrary",) * n_axes,
                                vmem_limit_bytes=VMEM_LIMIT)


def _dot(a, b):
    return jnp.dot(a, b, preferred_element_type=F32)


def _rms(x, g):
    return x * lax.rsqrt(jnp.mean(x * x, axis=-1, keepdims=True) + NORM_EPS) * g


def _full(shape):
    n = len(shape)
    return pl.BlockSpec(shape, lambda *_: (0,) * n)


def _mod_kernel(c_ref, w_ref, b_ref, o_ref):
    c = c_ref[...]
    sc = c * jax.nn.sigmoid(c)
    o_ref[...] = jnp.dot(sc, w_ref[...], preferred_element_type=F32,
                         precision=lax.Precision.HIGHEST) + b_ref[...]


def _modulation(cvec, w, b):
    R, D = cvec.shape
    N = w.shape[1]
    tn = 1536
    return pl.pallas_call(
        _mod_kernel,
        grid=(N // tn,),
        in_specs=[_full((R, D)),
                  pl.BlockSpec((D, tn), lambda j: (0, j)),
                  pl.BlockSpec((1, tn), lambda j: (0, j))],
        out_specs=pl.BlockSpec((R, tn), lambda j: (0, j)),
        out_shape=jax.ShapeDtypeStruct((R, N), F32),
        compiler_params=_cparams(1),
        name="modulation",
    )(cvec, w, b.reshape(1, N))


def _seg_meansq(x, bd_ref, width, seg):
    x2 = x * x
    hi = x2.astype(BF16)
    lo = (x2 - hi.astype(F32)).astype(BF16)
    bd = bd_ref[0:width, 0:width]
    return (_dot(hi, bd) + _dot(lo, bd)) * (1.0 / seg)


def _rope(x, tab_ref, sh1, sh2):
    return (x * tab_ref[0]
            + pltpu.roll(x, sh1, axis=1) * tab_ref[1]
            + pltpu.roll(x, sh2, axis=1) * tab_ref[2])


def _in_kernel(*refs, rope):
    (x_ref, sh_ref, sc_ref, gpre_ref, win_ref, gq_ref, gk_ref, gcq_ref, gckv_ref,
     wuq_ref, wuk_ref, wuv_ref, ekr_ref, bd_ref) = refs[:14]
    if rope:
        rg_ref, rq_ref, rk_ref = refs[14:17]
        outs = refs[17:]
    else:
        outs = refs[14:]
    h_out, qg_out, qm_out, kd_out, vd_out, km_out, vm_out, u_out = outs

    x = x_ref[...]
    h = _rms(x, gpre_ref[...]) * (1.0 + sc_ref[...]) + sh_ref[...]
    hb = h.astype(BF16)
    h_out[...] = hb
    z = _dot(hb, win_ref[...])

    qg = z[:, C_QG:C_QM]
    qg = qg * lax.rsqrt(_seg_meansq(qg, bd_ref, C_QM - C_QG, GQA_HEAD_DIM) + NORM_EPS) * gq_ref[...]
    for c in range((C_QM - C_QG) // LANES):
        blk = qg[:, c * LANES:(c + 1) * LANES]
        if rope:
            blk = _rope(blk, rg_ref, 96, 32)
        qg_out[:, c * LANES:(c + 1) * LANES] = (blk * (GQA_SCALE * LOG2E)).astype(BF16)

    kd = z[:, C_KD:C_VD]
    kd = kd * lax.rsqrt(_seg_meansq(kd, bd_ref, C_VD - C_KD, GQA_HEAD_DIM) + NORM_EPS) * gk_ref[...]
    for c in range((C_VD - C_KD) // LANES):
        blk = kd[:, c * LANES:(c + 1) * LANES]
        if rope:
            blk = _rope(blk, rg_ref, 96, 32)
        kd_out[:, c * LANES:(c + 1) * LANES] = blk.astype(BF16)
    vd_out[...] = z[:, C_VD:C_CKV].astype(BF16)

    qm = _rms(z[:, C_QM:C_KD], gcq_ref[...]).astype(BF16)
    qm = _dot(qm, wuq_ref[...])
    for c in range(MLA_HEADS):
        blk = qm[:, c * HEAD_SLOT:(c + 1) * HEAD_SLOT]
        if rope:
            blk = _rope(blk, rq_ref, 112, 16)
        qm_out[:, c * HEAD_SLOT:(c + 1) * HEAD_SLOT] = (blk * (MLA_SCALE * LOG2E)).astype(BF16)

    ckv = _rms(z[:, C_CKV:C_U], gckv_ref[...]).astype(BF16)
    kr = z[:, C_KR:C_KR + LANES]
    if rope:
        kr = _rope(kr, rk_ref, 112, 16)
    km = _dot(ckv, wuk_ref[...]) + _dot(kr.astype(BF16), ekr_ref[...])
    km_out[...] = km.astype(BF16)
    vm_out[...] = _dot(ckv, wuv_ref[...]).astype(BF16)

    u_out[...] = z[:, C_U:C_KR]


def _in_proj(x, shift, scale, gpre, lw, tabs, nseg):
    B, L, D = x.shape
    tm = min(512, L // nseg)
    tseg = L // nseg
    nt = tseg // tm
    rope = tabs is not None

    def tok(b, j, i):
        return (b, j * nt + i, 0)

    def tokspec(w):
        return pl.BlockSpec((None, tm, w), tok)

    def tabspec():
        return pl.BlockSpec((3, tm, LANES), lambda b, j, i: (0, j * nt + i, 0))

    vec = lambda w: pl.BlockSpec((None, 1, w), lambda b, j, i: (b, 0, 0))
    in_specs = [tokspec(D), vec(D), vec(D), _full((1, D)), _full((D, N_IN_PACKED)),
                _full((1, 512)), _full((1, 256)), _full((1, MLA_Q_RANK)), _full((1, MLA_KV_RANK)),
                _full((MLA_Q_RANK, 1024)), _full((MLA_KV_RANK, 1024)), _full((MLA_KV_RANK, 512)),
                _full((LANES, 1024)), _full((512, 512))]
    args = [x, shift, scale, gpre, lw["w_in"], lw["g_q"], lw["g_k"], lw["g_cq"], lw["g_ckv"],
            lw["w_uq"], lw["w_uk"], lw["w_uv"], lw["e_kr"], lw["bd64"]]
    if rope:
        in_specs += [tabspec(), tabspec(), tabspec()]
        args += list(tabs)
    widths = [D, 512, 1024, 256, 256, 1024, 512]
    out_specs = [tokspec(w) for w in widths]
    out_shape = [jax.ShapeDtypeStruct((B, L, w), BF16) for w in widths]
    out_specs.append(pl.BlockSpec((None, tm, S5_WIDTH), lambda b, j, i: (b, i, j)))
    out_shape.append(jax.ShapeDtypeStruct((B, tseg, nseg * S5_WIDTH), F32))
    return pl.pallas_call(
        functools.partial(_in_kernel, rope=rope),
        grid=(B, nseg, nt),
        in_specs=in_specs,
        out_specs=out_specs,
        out_shape=out_shape,
        compiler_params=_cparams(3),
        name="in_proj_rope" if rope else "in_proj",
    )(*args)


ONES_ROWS = 16
ATTN_KEY_CHUNK = 512
ATTN_MIN_DENOM = 2.0 ** -60
KEY_NORM_PAD = 1.0 + 2.0 ** -6


def _attn_kernel(*refs, gqa, chunks, seg_starts):
    nseg = len(seg_starts)
    q_ref = refs[0]
    k_refs = refs[1:1 + nseg]
    v_refs = refs[1 + nseg:1 + 2 * nseg]
    o_ref, vt_ref, kn_ref, s0_ref, s1_ref, p0_ref, p1_ref = refs[1 + 2 * nseg:]
    s_refs, p_refs = (s0_ref, s1_ref), (p0_ref, p1_ref)
    hd = LANES // 2

    def keys(par, seg):
        return k_refs[seg][...] if gqa else k_refs[seg][:, par * HEAD_SLOT:(par + 1) * HEAD_SLOT]

    @pl.when(pl.program_id(2) == 0)
    def _():
        for v_ref, start in zip(v_refs, seg_starts):
            vt = v_ref[...].astype(F32).T
            ones = jnp.ones((ONES_ROWS, vt.shape[1]), BF16)
            for par in range(2):
                vt_ref[par, 0:hd, start:start + vt.shape[1]] = vt[par * hd:(par + 1) * hd, :].astype(BF16)
                vt_ref[par, hd:hd + ONES_ROWS, start:start + vt.shape[1]] = ones
        all_ones = jnp.ones((LANES, LANES), BF16)
        for par in range(2):
            big = None
            for seg in range(nseg):
                kf = keys(par, seg).astype(F32)
                n2 = jnp.max(_dot((kf * kf).astype(BF16), all_ones), axis=0, keepdims=True)
                big = n2 if big is None else jnp.maximum(big, n2)
            kn_ref[par] = jnp.broadcast_to(jnp.sqrt(big * KEY_NORM_PAD), kn_ref.shape[1:])

    qt = q_ref[...].astype(F32).T
    qh = []
    for par in range(2):
        if gqa:
            row = lax.broadcasted_iota(jnp.int32, (LANES, 1), 0)
            keep = (row < hd) if par == 0 else (row >= hd)
            qh.append(jnp.where(keep, qt, 0.0).astype(BF16))
        else:
            qh.append(qt[par * HEAD_SLOT:(par + 1) * HEAD_SLOT, :].astype(BF16))

    outs, dens = [], []
    for par in range(2):
        qf = qh[par].astype(F32)
        bound = jnp.sqrt(jnp.sum(qf * qf, axis=0, keepdims=True)) * kn_ref[par, 0:1, 0:1]
        acc = None
        for seg, start in enumerate(seg_starts):
            kh = keys(par, seg)
            pt = jnp.exp2(_dot(kh, qh[par]) - bound).astype(BF16)
            t = _dot(vt_ref[par, :, start:start + kh.shape[0]], pt)
            acc = t if acc is None else acc + t
        outs.append(acc[0:hd, :] / acc[hd:hd + 1, :])
        dens.append(acc[hd:hd + 1, :])
    o_ref[...] = jnp.concatenate(outs, axis=0).T.astype(o_ref.dtype)
    underflow = jnp.min(jnp.minimum(dens[0], dens[1])) < ATTN_MIN_DENOM

    @pl.when(underflow)
    def _():
        _attn_exact(o_ref, k_refs, vt_ref, s_refs, p_refs, qh, gqa=gqa, chunks=chunks, seg_starts=seg_starts)


def _attn_exact(o_ref, k_refs, vt_ref, s_refs, p_refs, qh, *, gqa, chunks, seg_starts):
    hd = LANES // 2

    def scores(par, seg, r0, rows):
        k_ref = k_refs[seg]
        kh = k_ref[r0:r0 + rows, :] if gqa else k_ref[r0:r0 + rows, par * HEAD_SLOT:(par + 1) * HEAD_SLOT]
        st = _dot(kh, qh[par])
        c0 = seg_starts[seg] + r0
        s_refs[par][c0:c0 + rows, :] = st
        return jnp.max(st, axis=0, keepdims=True)

    def probs(par, seg, r0, rows, m):
        c0 = seg_starts[seg] + r0
        p_refs[par][c0:c0 + rows, :] = jnp.exp2(s_refs[par][c0:c0 + rows, :] - m).astype(BF16)

    def weighted(par, seg, r0, rows):
        c0 = seg_starts[seg] + r0
        return _dot(vt_ref[par, :, c0:c0 + rows], p_refs[par][c0:c0 + rows, :])

    def fold(a, b):
        return b if a is None else a + b

    m0 = m1 = acc0 = acc1 = None
    for ch in chunks:
        sm = scores(0, *ch)
        m0 = sm if m0 is None else jnp.maximum(m0, sm)
    for ch in chunks:
        sm = scores(1, *ch)
        m1 = sm if m1 is None else jnp.maximum(m1, sm)
        probs(0, *ch, m0)
    for ch in chunks:
        acc0 = fold(acc0, weighted(0, *ch))
        probs(1, *ch, m1)
    for ch in chunks:
        acc1 = fold(acc1, weighted(1, *ch))
    outs = [a[0:hd, :] / a[hd:hd + 1, :] for a in (acc0, acc1)]
    o_ref[...] = jnp.concatenate(outs, axis=0).T.astype(o_ref.dtype)


def _attention(q, ks, vs, gqa):
    B, Lq, _ = q.shape
    lens = [k.shape[1] for k in ks]
    seg_starts = tuple(int(s) for s in np.cumsum([0] + lens[:-1]))
    Lk = sum(lens)
    tq = min(512, Lq)
    n_pairs = 4
    if gqa:
        q_spec = pl.BlockSpec((None, tq, LANES), lambda b, p, i: (b, i, p))
        k_specs = [pl.BlockSpec((None, n, LANES), lambda b, p, i: (b, 0, p // 2)) for n in lens]
        v_specs = [pl.BlockSpec((None, n, LANES), lambda b, p, i: (b, 0, p // 2)) for n in lens]
    else:
        q_spec = pl.BlockSpec((None, tq, 2 * HEAD_SLOT), lambda b, p, i: (b, i, p))
        k_specs = [pl.BlockSpec((None, n, 2 * HEAD_SLOT), lambda b, p, i: (b, 0, p)) for n in lens]
        v_specs = [pl.BlockSpec((None, n, LANES), lambda b, p, i: (b, 0, p)) for n in lens]
    chunks = tuple((seg, r0, min(ATTN_KEY_CHUNK, n - r0))
                   for seg, n in enumerate(lens) for r0 in range(0, n, ATTN_KEY_CHUNK))
    return pl.pallas_call(
        functools.partial(_attn_kernel, gqa=gqa, chunks=chunks, seg_starts=seg_starts),
        grid=(B, n_pairs, Lq // tq),
        in_specs=[q_spec] + k_specs + v_specs,
        out_specs=pl.BlockSpec((None, tq, LANES), lambda b, p, i: (b, i, p)),
        out_shape=jax.ShapeDtypeStruct((B, Lq, n_pairs * LANES), BF16),
        scratch_shapes=[pltpu.VMEM((2, LANES // 2 + ONES_ROWS, Lk), BF16), pltpu.VMEM((2, 8, LANES), F32),
                        pltpu.VMEM((Lk, tq), F32), pltpu.VMEM((Lk, tq), F32),
                        pltpu.VMEM((Lk, tq), BF16), pltpu.VMEM((Lk, tq), BF16)],
        compiler_params=_cparams(3),
        name="attn_gqa" if gqa else "attn_mla",
    )(q, *ks, *vs)


S5_COLS = 512
assert S5_COLS // S5_STATE * S5_GROUP == LANES


def _s5_kernel(u_ref, bre_ref, bim_ref, cre_ref, cim_ref, a_ref, at_ref, init_ref,
               y_ref, fin_ref, vre, vim, st, *, rev, ti):
    ps = pl.program_id(1)
    ch = pl.program_id(2)
    n = S5_N

    @pl.when((ps == 0) & (ch == 0))
    def _():
        st[...] = jnp.zeros_like(st)

    @pl.when((ps == 1) & (ch == 0))
    def _():
        ends = [(st[0, j:j + 1, :], st[1, j:j + 1, :]) for j in range(S5_SEGS)]
        atr, ati = at_ref[0], at_ref[1]
        cr, ci = init_ref[0], init_ref[1]
        order = range(S5_SEGS - 1, -1, -1) if rev else range(S5_SEGS)
        for j in order:
            st[0, j:j + 1, :] = cr
            st[1, j:j + 1, :] = ci
            er, ei = ends[j]
            cr, ci = atr * cr - ati * ci + er, atr * ci + ati * cr + ei
        fin_ref[0] = cr
        fin_ref[1] = ci

    ub = u_ref[...].reshape(ti * S5_SEGS, S5_WIDTH).astype(BF16)
    for m in range(n // S5_COLS):
        ch = slice(m * LANES, (m + 1) * LANES)
        cols = slice(m * S5_COLS, (m + 1) * S5_COLS)
        vre[:, :, cols] = _dot(ub[:, ch], bre_ref[ch, cols]).reshape(ti, S5_SEGS, S5_COLS)
        vim[:, :, cols] = _dot(ub[:, ch], bim_ref[ch, cols]).reshape(ti, S5_SEGS, S5_COLS)

    def scan(store):
        for cb in range(n // S5_COLS):
            cols = slice(cb * S5_COLS, (cb + 1) * S5_COLS)
            ar = jnp.broadcast_to(a_ref[0, :, cols], (S5_SEGS, S5_COLS))
            ai = jnp.broadcast_to(a_ref[1, :, cols], (S5_SEGS, S5_COLS))

            def body(i, carry):
                xr, xi = carry
                idx = (ti - 1 - i) if rev else i
                nr = ar * xr - ai * xi + vre[idx, :, cols]
                ni = ar * xi + ai * xr + vim[idx, :, cols]
                if store:
                    vre[idx, :, cols] = nr
                    vim[idx, :, cols] = ni
                return nr, ni

            xr, xi = lax.fori_loop(0, ti, body, (st[0, :, cols], st[1, :, cols]), unroll=4)
            st[0, :, cols] = xr
            st[1, :, cols] = xi

    @pl.when(ps == 0)
    def _():
        scan(False)

    @pl.when(ps == 1)
    def _():
        scan(True)
        xr = vre[...].reshape(ti * S5_SEGS, n).astype(BF16)
        xi = vim[...].reshape(ti * S5_SEGS, n).astype(BF16)
        y = _dot(xr, cre_ref[...]) + _dot(xi, cim_ref[...])
        y_ref[...] = y.reshape(ti, S5_SEGS, S5_WIDTH)


def _s5_scan(u, sw, init, rev):
    B, T, _, _ = u.shape
    ti = min(128, T)
    nch = T // ti
    n = S5_N

    def chunk(c):
        return (nch - 1 - c) if rev else c

    d = 1 if rev else 0
    return pl.pallas_call(
        functools.partial(_s5_kernel, rev=rev, ti=ti),
        grid=(B, 2, nch),
        in_specs=[pl.BlockSpec((None, ti, S5_SEGS, S5_WIDTH), lambda b, p, c: (b, chunk(c), 0, 0)),
                  _full((S5_WIDTH, n)), _full((S5_WIDTH, n)), _full((n, S5_WIDTH)), _full((n, S5_WIDTH)),
                  _full((2, 1, n)), _full((2, 1, n)),
                  pl.BlockSpec((None, 2, 1, n), lambda b, p, c: (b, 0, 0, 0))],
        out_specs=[pl.BlockSpec((None, ti, S5_SEGS, S5_WIDTH),
                                lambda b, p, c: (b, chunk(c * p), 0, 0)),
                   pl.BlockSpec((None, 2, 1, n), lambda b, p, c: (b, 0, 0, 0))],
        out_shape=[jax.ShapeDtypeStruct(u.shape, F32), jax.ShapeDtypeStruct((B, 2, 1, n), F32)],
        scratch_shapes=[pltpu.VMEM((ti, S5_SEGS, n), F32), pltpu.VMEM((ti, S5_SEGS, n), F32),
                        pltpu.VMEM((2, S5_SEGS, n), F32)],
        compiler_params=_cparams(3),
        name="s5_bwd" if rev else "s5_fwd",
    )(u, sw["b_re"][d], sw["b_im"][d], sw["c_re"][d], sw["c_im_neg"][d],
      sw["a"][d], sw["a_t"][(d, T)], init)


def _glu_kernel(u_ref, yf_ref, yb_ref, d_ref, w_ref, o_ref):
    y = d_ref[...] * u_ref[...] + yf_ref[...] + yb_ref[...]
    g = 0.5 * y * (1.0 + jnp.tanh(math.sqrt(2.0 / math.pi) * (y + 0.044715 * (y * y * y))))
    t = _dot(g.astype(BF16), w_ref[...])
    o_ref[...] = (t[:, 0:S5_WIDTH] * jax.nn.sigmoid(t[:, S5_WIDTH:2 * S5_WIDTH])).astype(o_ref.dtype)


def _s5_glu(u, yf, yb, dskip, wglu, nseg):
    B, T, _ = u.shape
    tm = min(256, T)
    nt = T // tm
    seg = pl.BlockSpec((None, tm, S5_WIDTH), lambda b, j, i: (b, i, j))
    return pl.pallas_call(
        _glu_kernel,
        grid=(B, nseg, nt),
        in_specs=[seg, seg, seg, _full((1, S5_WIDTH)), _full((S5_WIDTH, 2 * S5_WIDTH))],
        out_specs=pl.BlockSpec((None, tm, S5_WIDTH), lambda b, j, i: (b, j * nt + i, 0)),
        out_shape=jax.ShapeDtypeStruct((B, T * nseg, S5_WIDTH), BF16),
        compiler_params=_cparams(3),
        name="s5_glu",
    )(u, yf, yb, dskip, wglu)


def _merge_kernel(*refs, glu):
    h_ref, ya_ref, yb_ref = refs[:3]
    if glu:
        u_ref, yf_ref, yr_ref, dsk_ref, wglu_ref = refs[3:8]
        rest = refs[8:]
        y = dsk_ref[...] * u_ref[...] + yf_ref[...] + yr_ref[...]
        y = 0.5 * y * (1.0 + jnp.tanh(math.sqrt(2.0 / math.pi) * (y + 0.044715 * (y * y * y))))
        t = _dot(y.astype(BF16), wglu_ref[...])
        yc = (t[:, 0:S5_WIDTH] * jax.nn.sigmoid(t[:, S5_WIDTH:2 * S5_WIDTH])).astype(BF16)
    else:
        yc = refs[3][...]
        rest = refs[4:]
    x_ref, gate_ref, wmg_ref, bmg_ref, wbr_ref, wout_ref, gpost_ref, o_ref = rest
    h = h_ref[...]
    d = x_ref.shape[-1]
    acc = None
    for n, yn in enumerate((ya_ref[...], yb_ref[...], yc)):
        g = jax.nn.sigmoid(_dot(h, wmg_ref[:, n * d:(n + 1) * d]) + bmg_ref[:, n * d:(n + 1) * d])
        t = g * _dot(yn, wbr_ref[n])
        acc = t if acc is None else acc + t
    y = _dot(acc.astype(BF16), wout_ref[...])
    o_ref[...] = x_ref[...] + gate_ref[...] * _rms(y, gpost_ref[...])


def _merge(h, ya, yb, yc, x, gate, lw):
    B, L, D = x.shape
    glu = isinstance(yc, tuple)
    tm = L // S5_SEGS if glu else min(512, L)
    tok = lambda w: pl.BlockSpec((None, tm, w), lambda b, i: (b, i, 0))
    if glu:
        seg = pl.BlockSpec((None, tm, S5_WIDTH), lambda b, i: (b, 0, i))
        c_specs = [seg, seg, seg, _full((1, S5_WIDTH)), _full((S5_WIDTH, 2 * S5_WIDTH))]
        c_args = [*yc, lw["s5_d"], lw["w_glu"]]
    else:
        c_specs, c_args = [tok(512)], [yc]
    return pl.pallas_call(
        functools.partial(_merge_kernel, glu=glu),
        grid=(B, L // tm),
        in_specs=[tok(D), tok(512), tok(512)] + c_specs + [
            tok(D), pl.BlockSpec((None, 1, D), lambda b, i: (b, 0, 0)),
            _full((D, N_BRANCHES * D)), _full((1, N_BRANCHES * D)), _full((N_BRANCHES, 512, D)),
            _full((D, D)), _full((1, D))],
        out_specs=tok(D),
        out_shape=jax.ShapeDtypeStruct((B, L, D), F32),
        compiler_params=_cparams(2),
        name="merge_glu" if glu else "merge",
    )(h, ya, yb, *c_args, x, gate, lw["w_mg"], lw["b_mg"], lw["w_branch"], lw["w_out"], lw["g_mix_post"])


def _ffn_kernel(x_ref, sh_ref, sc_ref, gate_ref, gpre_ref, gpost_ref, w1_ref, w3_ref, w2_ref, o_ref,
                h_s, acc_s):
    f = pl.program_id(2)

    @pl.when(f == 0)
    def _():
        h = _rms(x_ref[...], gpre_ref[...]) * (1.0 + sc_ref[...]) + sh_ref[...]
        h_s[...] = h.astype(BF16)
        acc_s[...] = jnp.zeros_like(acc_s)

    h = h_s[...]
    a = _dot(h, w1_ref[...])
    g = (a * jax.nn.sigmoid(a)) * _dot(h, w3_ref[...])
    acc_s[...] += _dot(g.astype(BF16), w2_ref[...])

    @pl.when(f == pl.num_programs(2) - 1)
    def _():
        o_ref[...] = x_ref[...] + gate_ref[...] * _rms(acc_s[...], gpost_ref[...])


def _ffn(x, shift, scale, gate, gpre, gpost, w1, w3, w2):
    B, L, D = x.shape
    dff = w1.shape[1]
    tm = min(512, L)
    tf = dff // 2 if (dff // 2) % LANES == 0 else dff
    tok = pl.BlockSpec((None, tm, D), lambda b, i, f: (b, i, 0))
    vec = pl.BlockSpec((None, 1, D), lambda b, i, f: (b, 0, 0))
    return pl.pallas_call(
        _ffn_kernel,
        grid=(B, L // tm, dff // tf),
        in_specs=[tok, vec, vec, vec, _full((1, D)), _full((1, D)),
                  pl.BlockSpec((D, tf), lambda b, i, f: (0, f)),
                  pl.BlockSpec((D, tf), lambda b, i, f: (0, f)),
                  pl.BlockSpec((tf, D), lambda b, i, f: (f, 0))],
        out_specs=tok,
        out_shape=jax.ShapeDtypeStruct((B, L, D), F32),
        scratch_shapes=[pltpu.VMEM((tm, D), BF16), pltpu.VMEM((tm, D), F32)],
        compiler_params=_cparams(3),
        name="ffn",
    )(x, shift, scale, gate, gpre, gpost, w1, w3, w2)


MOE_BLOCK = 1024
MOE_CHUNK = 64
MOE_TILE = 512
MOE_SORT_ROWS = 512


def _lane_col(a, lane, k):
    return jnp.sum(jnp.where(lane == k, a, 0.0), axis=1, keepdims=True)


def _router_kernel(x_ref, sh_ref, sc_ref, gpre_ref, rw_ref, rb_ref, h_out, route_out, cnt_out):
    lane = lax.broadcasted_iota(jnp.int32, (1, LANES), 1)
    h = _rms(x_ref[...], gpre_ref[...]) * (1.0 + sc_ref[...]) + sh_ref[...]
    h_out[...] = h.astype(BF16)
    logits = jnp.dot(h, rw_ref[...], preferred_element_type=F32,
                     precision=lax.Precision.HIGHEST) + rb_ref[...]
    m1 = jnp.max(logits, axis=1, keepdims=True)
    i1 = jnp.min(jnp.where(logits == m1, lane, LANES), axis=1, keepdims=True)
    rest = jnp.where(lane == i1, -jnp.inf, logits)
    m2 = jnp.max(rest, axis=1, keepdims=True)
    i2 = jnp.min(jnp.where(rest == m2, lane, LANES), axis=1, keepdims=True)
    e2 = jnp.exp(m2 - m1)
    den = 1.0 + e2
    route_out[...] = (jnp.where(lane == 0, i1.astype(F32), 0.0) + jnp.where(lane == 1, i2.astype(F32), 0.0)
                      + jnp.where(lane == 2, 1.0 / den, 0.0) + jnp.where(lane == 3, e2 / den, 0.0))
    member = jnp.where((lane == i1) | (lane == i2), 1.0, 0.0)
    cnt_out[...] = jnp.sum(member, axis=0, keepdims=True)


def _bf16_pieces(w):
    hi = w.astype(BF16).astype(F32)
    r1 = w - hi
    lo = r1.astype(BF16).astype(F32)
    return hi, lo, r1 - lo


def _dispatch_kernel(h_ref, route_ref, ltri_ref, ustr_ref, xs_out, pos_out):
    d = h_ref.shape[1]
    lane = lax.broadcasted_iota(jnp.int32, (1, LANES), 1)
    lanef = lane.astype(F32)
    route = route_ref[...]
    i1, i2 = _lane_col(route, lane, 0), _lane_col(route, lane, 1)
    g1, g2 = _lane_col(route, lane, 2), _lane_col(route, lane, 3)
    m1, m2 = lanef == i1, lanef == i2
    member = jnp.where(m1 | m2, 1.0, 0.0)
    before = _dot(ltri_ref[...], member.astype(BF16))
    counts = jnp.sum(member, axis=0, keepdims=True)
    padded = jnp.floor((counts + (MOE_CHUNK - 1.0)) * (1.0 / MOE_CHUNK)) * MOE_CHUNK
    seg = _dot(jnp.broadcast_to(padded, (8, LANES)).astype(BF16), ustr_ref[...])[0:1, :]
    slot = before + seg
    pos1 = jnp.sum(jnp.where(m1, slot, 0.0), axis=1, keepdims=True)
    pos2 = jnp.sum(jnp.where(m2, slot, 0.0), axis=1, keepdims=True)
    pos = jnp.where(lane == 0, pos1, 0.0) + jnp.where(lane == 1, pos2, 0.0)
    pos_out[...] = pos
    post = pos.T
    row1, row2 = post[0:1, :], post[1:2, :]

    def gate_cols(g):
        a, b, c = _bf16_pieces(g)
        return (jnp.where(lane == 0, a, 0.0) + jnp.where(lane == 1, b, 0.0)
                + jnp.where(lane == 2, c, 0.0)).astype(BF16)

    gc1, gc2 = gate_cols(g1), gate_cols(g2)
    h = h_ref[...]
    for c in range(xs_out.shape[0] // MOE_SORT_ROWS):
        rows = (lax.broadcasted_iota(jnp.int32, (MOE_SORT_ROWS, 1), 0) + c * MOE_SORT_ROWS).astype(F32)
        q1, q2 = rows == row1, rows == row2
        sl = slice(c * MOE_SORT_ROWS, (c + 1) * MOE_SORT_ROWS)
        xs_out[sl, 0:d] = _dot(jnp.where(q1 | q2, 1.0, 0.0).astype(BF16), h).astype(BF16)
        gates = (_dot(jnp.where(q1, 1.0, 0.0).astype(BF16), gc1)
                 + _dot(jnp.where(q2, 1.0, 0.0).astype(BF16), gc2))
        xs_out[sl, d:d + LANES] = gates.astype(BF16)


def _expert_kernel(*refs, n_in):
    nu_ref = refs[1]
    x_refs = refs[3:3 + n_in]
    w1_ref, w3_ref, w2_ref, o_ref, x_s, acc_s = refs[3 + n_in:]
    i = pl.program_id(0)
    f = pl.program_id(1)
    d = o_ref.shape[1]

    @pl.when(i < nu_ref[0])
    def _():
        @pl.when(f == 0)
        def _():
            for k, x_ref in enumerate(x_refs):
                x_s[k * MOE_CHUNK:(k + 1) * MOE_CHUNK, :] = x_ref[...]
            acc_s[...] = jnp.zeros_like(acc_s)

        x = x_s[:, 0:d]
        a = _dot(x, w1_ref[...])
        g = (a * jax.nn.sigmoid(a)) * _dot(x, w3_ref[...])
        acc_s[...] += _dot(g.astype(BF16), w2_ref[...])

        @pl.when(f == pl.num_programs(1) - 1)
        def _():
            gate = jnp.sum(x_s[:, d:d + LANES].astype(F32), axis=1, keepdims=True)
            o_ref[...] = (acc_s[...] * gate).astype(o_ref.dtype)

    @pl.when(i >= nu_ref[0])
    def _():
        o_ref[...] = jnp.zeros_like(o_ref)


def _combine_kernel(*refs, rc, n_in):
    ys_refs = refs[1:1 + n_in]
    pos_ref, x_ref, gate_ref, gpost_ref, o_ref, ys_s = refs[1 + n_in:]
    g = pl.program_id(1)
    for k, ys_ref in enumerate(ys_refs):
        row = pl.multiple_of((g * n_in + k) * MOE_CHUNK, MOE_CHUNK)
        ys_s[pl.ds(row, MOE_CHUNK), :] = ys_ref[...]

    @pl.when(g == rc // n_in - 1)
    def _():
        lane = lax.broadcasted_iota(jnp.int32, (1, LANES), 1)
        pos = pos_ref[...]
        pos1, pos2 = _lane_col(pos, lane, 0), _lane_col(pos, lane, 1)
        acc = None
        for c in range(rc * MOE_CHUNK // MOE_SORT_ROWS):
            cols = (lax.broadcasted_iota(jnp.int32, (1, MOE_SORT_ROWS), 1) + c * MOE_SORT_ROWS).astype(F32)
            sel = jnp.where((cols == pos1) | (cols == pos2), 1.0, 0.0).astype(BF16)
            t = _dot(sel, ys_s[c * MOE_SORT_ROWS:(c + 1) * MOE_SORT_ROWS, :])
            acc = t if acc is None else acc + t
        o_ref[...] = x_ref[...] + gate_ref[...] * _rms(acc, gpost_ref[...])


def _moe_tables(counts, rc, nt):
    ne = counts.shape[1]
    per_tile = MOE_TILE // MOE_CHUNK
    cch = (counts + MOE_CHUNK - 1) // MOE_CHUNK
    tiles_e = (jnp.sum(cch, axis=0) + per_tile - 1) // per_tile
    tile_end = jnp.cumsum(tiles_e)
    n_used = tile_end[-1]
    chunk_start = (tile_end - tiles_e)[None, :] * per_tile + (jnp.cumsum(cch, axis=0) - cch)
    seg_end = jnp.cumsum(cch, axis=1)
    r = jnp.arange(rc)
    e_r = jnp.minimum(jnp.sum(r[None, :, None] >= seg_end[:, None, :], axis=-1), ne - 1)
    onehot = (e_r[:, :, None] == jnp.arange(ne)[None, None, :]).astype(jnp.int32)
    pick = lambda t: jnp.sum(onehot * t[:, None, :], axis=-1)
    dest = pick(chunk_start) + r[None, :] - pick(seg_end - cch)
    n_valid = seg_end[:, -1]
    last = jnp.sum(jnp.where(r[None, :] == n_valid[:, None] - 1, dest, 0), axis=1, keepdims=True)
    dest = jnp.where(r[None, :] < n_valid[:, None], dest, last)
    t = jnp.maximum(jnp.minimum(jnp.arange(nt), n_used - 1), 0)
    tile_e = jnp.minimum(jnp.sum(t[:, None] >= tile_end[None, :], axis=-1), ne - 1)
    e_hot = (tile_e[:, None] == jnp.arange(ne)[None, :]).astype(jnp.int32)
    of_e = lambda a: jnp.sum(e_hot * a[None, :], axis=-1)
    of_eb = lambda a: jnp.sum(e_hot[:, None, :] * a[None, :, :], axis=-1)
    q = (t - of_e(tile_end - tiles_e))[:, None] * per_tile + jnp.arange(per_tile)[None, :]
    q = jnp.where(q < of_e(jnp.sum(cch, axis=0))[:, None], q, q[:, :1])
    blk_end = of_eb(jnp.cumsum(cch, axis=0))
    b_q = jnp.minimum(jnp.sum(q[:, :, None] >= blk_end[:, None, :], axis=-1), cch.shape[0] - 1)
    b_hot = (b_q[:, :, None] == jnp.arange(cch.shape[0])[None, None, :]).astype(jnp.int32)
    of_b = lambda a: jnp.sum(b_hot * a[:, None, :], axis=-1)
    xsrc = b_q * rc + of_b(of_eb(seg_end - cch)) + q - of_b(blk_end - of_eb(cch))
    i32 = lambda a: a.reshape(-1).astype(jnp.int32)
    return i32(dest), i32(xsrc), i32(tile_e), i32(n_used)


def _moe(x, shift, scale, gate, gpre, gpost, rw, rb, w1, w3, w2):
    B, L, D = x.shape
    ne, _, dff = w1.shape
    n = B * L
    t = min(MOE_BLOCK, L)
    nblk, per_batch = n // t, L // t
    per_tile = MOE_TILE // MOE_CHUNK
    rc = pl.cdiv(2 * t + ne * MOE_CHUNK, MOE_TILE) * per_tile
    nt = (2 * n // MOE_CHUNK + nblk * ne + per_tile - 1) // per_tile + ne
    tf = dff // 2 if (dff // 2) % LANES == 0 else dff
    nf = dff // tf
    xf = x.reshape(n, D)
    wide = D + LANES

    blk = lambda w: pl.BlockSpec((t, w), lambda i: (i, 0))
    vec1 = pl.BlockSpec((None, 1, D), lambda i: (i // per_batch, 0, 0))
    h, route, cnt = pl.pallas_call(
        _router_kernel,
        grid=(nblk,),
        in_specs=[blk(D), vec1, vec1, _full((1, D)), _full((D, LANES)), _full((1, LANES))],
        out_specs=[blk(D), blk(LANES), pl.BlockSpec((None, 1, LANES), lambda i: (i, 0, 0))],
        out_shape=[jax.ShapeDtypeStruct((n, D), BF16), jax.ShapeDtypeStruct((n, LANES), F32),
                   jax.ShapeDtypeStruct((nblk, 1, LANES), F32)],
        compiler_params=_cparams(1),
        name="moe_router",
    )(xf, shift, scale, gpre, rw, rb)

    counts = cnt[:, 0, :ne].astype(jnp.int32)
    dest, xsrc, tile_e, n_used = _moe_tables(counts, rc, nt)

    ltri = jnp.asarray(np.tril(np.ones((t, t), np.float32), -1), BF16)
    ustr = jnp.asarray(np.triu(np.ones((LANES, LANES), np.float32), 1), BF16)
    xs, pos = pl.pallas_call(
        _dispatch_kernel,
        grid=(nblk,),
        in_specs=[blk(D), blk(LANES), _full((t, t)), _full((LANES, LANES))],
        out_specs=[pl.BlockSpec((rc * MOE_CHUNK, wide), lambda b: (b, 0)), blk(LANES)],
        out_shape=[jax.ShapeDtypeStruct((nblk * rc * MOE_CHUNK, wide), BF16),
                   jax.ShapeDtypeStruct((n, LANES), F32)],
        compiler_params=_cparams(1),
        name="moe_dispatch",
    )(h, route, ltri, ustr)

    def tile(i, nu):
        return jnp.maximum(jnp.minimum(i, nu[0] - 1), 0)

    def fcol(i, f, nu):
        return jnp.where(i < nu[0], f, nf - 1)

    def chunk_in(k):
        return pl.BlockSpec((MOE_CHUNK, wide), lambda i, f, te, nu, src: (src[tile(i, nu) * per_tile + k], 0))

    ys = pl.pallas_call(
        functools.partial(_expert_kernel, n_in=per_tile),
        grid_spec=pltpu.PrefetchScalarGridSpec(
            num_scalar_prefetch=3,
            grid=(nt, nf),
            in_specs=[chunk_in(k) for k in range(per_tile)] + [
                pl.BlockSpec((None, D, tf), lambda i, f, te, nu, src: (te[i], 0, fcol(i, f, nu))),
                pl.BlockSpec((None, D, tf), lambda i, f, te, nu, src: (te[i], 0, fcol(i, f, nu))),
                pl.BlockSpec((None, tf, D), lambda i, f, te, nu, src: (te[i], fcol(i, f, nu), 0))],
            out_specs=pl.BlockSpec((MOE_TILE, D), lambda i, f, te, nu, src: (i, 0)),
            scratch_shapes=[pltpu.VMEM((MOE_TILE, wide), BF16), pltpu.VMEM((MOE_TILE, D), F32)]),
        out_shape=jax.ShapeDtypeStruct((nt * MOE_TILE, D), BF16),
        compiler_params=_cparams(2),
        name="moe_experts",
    )(tile_e, n_used, xsrc, *([xs] * per_tile), w1, w3, w2)

    def chunk_out(k):
        return pl.BlockSpec((MOE_CHUNK, D), lambda b, g, src: (src[b * rc + g * per_tile + k], 0))

    out = pl.pallas_call(
        functools.partial(_combine_kernel, rc=rc, n_in=per_tile),
        grid_spec=pltpu.PrefetchScalarGridSpec(
            num_scalar_prefetch=1,
            grid=(nblk, rc // per_tile),
            in_specs=[chunk_out(k) for k in range(per_tile)] + [
                pl.BlockSpec((t, LANES), lambda b, g, src: (b, 0)),
                pl.BlockSpec((t, D), lambda b, g, src: (b, 0)),
                pl.BlockSpec((None, 1, D), lambda b, g, src: (b // per_batch, 0, 0)),
                pl.BlockSpec((1, D), lambda b, g, src: (0, 0))],
            out_specs=pl.BlockSpec((t, D), lambda b, g, src: (b, 0)),
            scratch_shapes=[pltpu.VMEM((rc * MOE_CHUNK, D), BF16)]),
        out_shape=jax.ShapeDtypeStruct((n, D), F32),
        compiler_params=_cparams(2),
        name="moe_combine",
    )(dest, *([ys] * per_tile), pos, xf, gate, gpost)
    return out.reshape(B, L, D)


def _rope_tables(n_lat):
    pos = jnp.arange(n_lat, dtype=jnp.int32)
    row = (pos // GRID_W).astype(F32)
    col = (pos % GRID_W).astype(F32)

    def cos_sin(rot_dim):
        axis_dim = rot_dim // 2
        inv_freq = ROPE_THETA ** (-jnp.arange(0, axis_dim, 2, dtype=F32) / axis_dim)
        ang = jnp.concatenate([row[:, None] * inv_freq, col[:, None] * inv_freq], axis=-1)
        return jnp.cos(ang), jnp.sin(ang)

    def table(cos, sin, start, stop):
        reps = (stop - start) // (2 * cos.shape[1])
        zero = jnp.zeros_like(sin)

        def lanes(first, second, fill):
            body = jnp.tile(jnp.concatenate([first, second], axis=1), (1, reps))
            return jnp.pad(body, ((0, 0), (start, LANES - stop)), constant_values=fill)

        return jnp.stack([lanes(cos, cos, 1.0), lanes(-sin, zero, 0.0), lanes(zero, sin, 0.0)]).astype(F32)

    gc, gs = cos_sin(GQA_HEAD_DIM)
    mc, ms = cos_sin(MLA_ROPE_DIM)
    t_gqa = table(gc, gs, 0, LANES)
    t_mq = table(mc, ms, MLA_NOPE_DIM, MLA_NOPE_DIM + MLA_ROPE_DIM)
    t_kr = table(mc, ms, 0, MLA_ROPE_DIM)
    return t_gqa, t_mq, t_kr


def _pack_layer(p, l):
    D = p["w_in"].shape[1]
    w = p["w_in"][l]
    offs = np.cumsum([0, 512, MLA_Q_RANK, 128, 128, MLA_KV_RANK, MLA_ROPE_DIM, S5_WIDTH])
    qg, qm, kg, vg, ckv, kr, u = [w[:, offs[i]:offs[i + 1]] for i in range(7)]
    dup = lambda t: jnp.concatenate([t[:, 0:64], t[:, 0:64], t[:, 64:128], t[:, 64:128]], axis=1)
    w_in = jnp.concatenate([qg, qm, dup(kg), dup(vg), ckv, u, kr,
                            jnp.zeros((D, LANES - MLA_ROPE_DIM), F32)], axis=1).astype(BF16)

    wq = p["w_uq"][l].reshape(MLA_Q_RANK, MLA_HEADS, MLA_NOPE_DIM + MLA_ROPE_DIM)
    w_uq = jnp.pad(wq, ((0, 0), (0, 0), (0, HEAD_SLOT - wq.shape[-1]))).reshape(MLA_Q_RANK, -1).astype(BF16)
    wkv = p["w_ukv"][l].reshape(MLA_KV_RANK, MLA_HEADS, MLA_NOPE_DIM + MLA_V_DIM)
    w_uk = jnp.pad(wkv[:, :, :MLA_NOPE_DIM], ((0, 0), (0, 0), (0, HEAD_SLOT - MLA_NOPE_DIM)))
    w_uk = w_uk.reshape(MLA_KV_RANK, -1).astype(BF16)
    w_uv = wkv[:, :, MLA_NOPE_DIM:].reshape(MLA_KV_RANK, -1).astype(BF16)
    e = np.zeros((LANES, MLA_HEADS, HEAD_SLOT), np.float32)
    for r in range(MLA_ROPE_DIM):
        e[r, :, MLA_NOPE_DIM + r] = 1.0
    bd = np.kron(np.eye(512 // GQA_HEAD_DIM, dtype=np.float32), np.ones((GQA_HEAD_DIM, GQA_HEAD_DIM), np.float32))
    return {
        "w_in": w_in,
        "g_q": jnp.tile(p["g_q"][l], GQA_HEADS).reshape(1, -1),
        "g_k": jnp.tile(p["g_k"][l], 2 * GQA_KV_HEADS).reshape(1, -1),
        "g_cq": p["g_cq"][l].reshape(1, -1),
        "g_ckv": p["g_ckv"][l].reshape(1, -1),
        "w_uq": w_uq, "w_uk": w_uk, "w_uv": w_uv,
        "e_kr": jnp.asarray(e.reshape(LANES, -1), BF16),
        "bd64": jnp.asarray(bd, BF16),
        "w_mg": p["w_merge_gate"][l].astype(BF16),
        "b_mg": p["b_merge_gate"][l].reshape(1, -1),
        "w_branch": p["w_branch"][l].astype(BF16),
        "w_out": p["w_out"][l].astype(BF16),
        "g_mix_pre": p["g_mix_pre"][l].reshape(1, -1),
        "g_mix_post": p["g_mix_post"][l].reshape(1, -1),
        "g_ffn_pre": p["g_ffn_pre"][l].reshape(1, -1),
        "g_ffn_post": p["g_ffn_post"][l].reshape(1, -1),
        "w_glu": p["w_glu"][l].astype(BF16),
        "s5_d": p["s5_d"][l].reshape(1, -1),
    }


def _cpow(ar, ai, n):
    rr, ri = None, None
    br, bi = ar, ai
    while n:
        if n & 1:
            rr, ri = (br, bi) if rr is None else (rr * br - ri * bi, rr * bi + ri * br)
        n >>= 1
        if n:
            br, bi = br * br - bi * bi, 2.0 * br * bi
    return rr, ri


def _s5_weights(p, l, seg_lens):
    G, P, H = S5_GROUPS, S5_STATE, S5_GROUP
    eye = jnp.eye(G, dtype=F32)
    out = {"b_re": [], "b_im": [], "c_re": [], "c_im_neg": [], "a": [], "a_t": {}}
    for d in range(2):
        lr = jnp.minimum(p["s5_lambda_re"][l, d], -1e-4)
        li = p["s5_lambda_im"][l, d]
        dt = jnp.exp(p["s5_log_dt"][l, d])[:, None]
        mag = jnp.exp(lr * dt)
        abr, abi = mag * jnp.cos(li * dt), mag * jnp.sin(li * dt)
        den = lr * lr + li * li
        fr = ((abr - 1.0) * lr + abi * li) / den
        fi = (abi * lr - (abr - 1.0) * li) / den
        br, bi = p["s5_b_re"][l, d], p["s5_b_im"][l, d]
        bbr = fr[..., None] * br - fi[..., None] * bi
        bbi = fr[..., None] * bi + fi[..., None] * br
        blk_b = lambda t: jnp.einsum("gph,gk->ghkp", t, eye).reshape(G * H, G * P).astype(BF16)
        blk_c = lambda t: jnp.einsum("ghp,gk->gpkh", t, eye).reshape(G * P, G * H).astype(BF16)
        out["b_re"].append(blk_b(bbr))
        out["b_im"].append(blk_b(bbi))
        out["c_re"].append(blk_c(p["s5_c_re"][l, d]))
        out["c_im_neg"].append(blk_c(-p["s5_c_im"][l, d]))
        out["a"].append(jnp.stack([abr.reshape(1, -1), abi.reshape(1, -1)]))
        for t in seg_lens:
            tr, ti = _cpow(abr, abi, t)
            out["a_t"][(d, t)] = jnp.stack([tr.reshape(1, -1), ti.reshape(1, -1)])
    return out


def _seg_view(t):
    B, T, _ = t.shape
    return t.reshape(B, T, S5_SEGS, S5_WIDTH)


def _to_segments(t):
    B, L, W = t.shape
    return t.reshape(B, S5_SEGS, L // S5_SEGS, W).transpose(0, 2, 1, 3).reshape(B, L // S5_SEGS, S5_SEGS * W)


def kernel(x, c, ctx, c_ctx, w_mod, b_mod, g_mix_pre, g_mix_post, g_ffn_pre, g_ffn_post, w_in, g_q, g_k, g_cq, g_ckv, w_uq, w_ukv, s5_lambda_re, s5_lambda_im, s5_log_dt, s5_b_re, s5_b_im, s5_c_re, s5_c_im, s5_d, w_glu, w_branch, w_merge_gate, b_merge_gate, w_out, ffn_w1, ffn_w3, ffn_w2, router_w, router_b, moe_w1, moe_w3, moe_w2):
    p = dict(w_in=w_in, g_q=g_q, g_k=g_k, g_cq=g_cq, g_ckv=g_ckv, w_uq=w_uq, w_ukv=w_ukv,
             s5_lambda_re=s5_lambda_re, s5_lambda_im=s5_lambda_im, s5_log_dt=s5_log_dt,
             s5_b_re=s5_b_re, s5_b_im=s5_b_im, s5_c_re=s5_c_re, s5_c_im=s5_c_im, s5_d=s5_d, w_glu=w_glu,
             w_branch=w_branch, w_merge_gate=w_merge_gate, b_merge_gate=b_merge_gate, w_out=w_out,
             g_mix_pre=g_mix_pre, g_mix_post=g_mix_post, g_ffn_pre=g_ffn_pre, g_ffn_post=g_ffn_post)
    B, L, D = x.shape
    Lc = ctx.shape[1]
    depth = w_mod.shape[0]
    tabs = _rope_tables(L)
    t_lat, t_ctx = L // S5_SEGS, Lc // S5_SEGS

    cvec = jnp.concatenate([c, c_ctx[None, :], jnp.zeros((8 - B - 1, D), F32)], axis=0)
    lat, cx = x, ctx
    for layer in range(depth):
        last = layer == depth - 1
        lw = _pack_layer(p, layer)
        sw = _s5_weights(p, layer, (t_lat, t_ctx))
        mod = _modulation(cvec, w_mod[layer], b_mod[layer])
        mod_l = [mod[:B, i * D:(i + 1) * D].reshape(B, 1, D) for i in range(6)]
        mod_c = [jnp.broadcast_to(mod[B, i * D:(i + 1) * D], (B, 1, D)) for i in range(6)]

        h_l, qg_l, qm_l, kd_l, vd_l, km_l, vm_l, u_l = _in_proj(
            lat, mod_l[0], mod_l[1], lw["g_mix_pre"], lw, tabs, S5_SEGS)
        h_c, qg_c, qm_c, kd_c, vd_c, km_c, vm_c, u_c = _in_proj(
            cx, mod_c[0], mod_c[1], lw["g_mix_pre"], lw, None, 1)
        ya_l = _attention(qg_l, (kd_c, kd_l), (vd_c, vd_l), gqa=True)
        yb_l = _attention(qm_l, (km_c, km_l), (vm_c, vm_l), gqa=False)

        u_cs = _to_segments(u_c)
        zero = jnp.zeros((B, 2, 1, S5_N), F32)
        yf_c, fin_f = _s5_scan(_seg_view(u_cs), sw, zero, rev=False)
        yb_c, fin_b = _s5_scan(_seg_view(u_cs), sw, zero, rev=True)
        yf_l, _ = _s5_scan(_seg_view(u_l), sw, fin_f, rev=False)
        yr_l, _ = _s5_scan(_seg_view(u_l), sw, fin_b, rev=True)
        flat = lambda t: t.reshape(t.shape[0], t.shape[1], S5_SEGS * S5_WIDTH)

        lat = _merge(h_l, ya_l, yb_l, (u_l, flat(yf_l), flat(yr_l)), lat, mod_l[2], lw)
        if not last:
            ya_c = _attention(qg_c, (kd_c,), (vd_c,), gqa=True)
            yb_c2 = _attention(qm_c, (km_c,), (vm_c,), gqa=False)
            yc_c = _s5_glu(u_cs, flat(yf_c), flat(yb_c), lw["s5_d"], lw["w_glu"], S5_SEGS)
            cx = _merge(h_c, ya_c, yb_c2, yc_c, cx, mod_c[2], lw)

        i = layer // 2
        if layer % 2 == 0:
            w1, w3, w2 = ffn_w1[i].astype(BF16), ffn_w3[i].astype(BF16), ffn_w2[i].astype(BF16)
            mix = lambda t, m: _ffn(t, m[3], m[4], m[5], lw["g_ffn_pre"], lw["g_ffn_post"], w1, w3, w2)
        else:
            w1, w3, w2 = moe_w1[i].astype(BF16), moe_w3[i].astype(BF16), moe_w2[i].astype(BF16)
            rw = jnp.pad(router_w[i], ((0, 0), (0, LANES - N_EXPERTS)))
            rb = jnp.concatenate([router_b[i], jnp.full((LANES - N_EXPERTS,), -1e30, F32)]).reshape(1, LANES)
            mix = lambda t, m: _moe(t, m[3], m[4], m[5], lw["g_ffn_pre"], lw["g_ffn_post"], rw, rb, w1, w3, w2)
        lat = mix(lat, mod_l)
        if not last:
            cx = mix(cx, mod_c)
    return lat
```
